```python
import jax, jax.numpy as jnp
from jax import lax
import numpy as np

D_MODEL = 4096
BATCH = 4
SEQ = 2048
DEPTH = 1
DEC_BATCH = 4
DEC_SEQ = 4096
PAST_LEN = 128

HEAD_DIM = 128
ATTN_PATTERNS = ((128, 1), (512, 4), (2048, 16))
N_ATT_GROUPS = 3
HEADS_PER_GROUP = 8
N_ATT_HEADS = N_ATT_GROUPS * HEADS_PER_GROUP
ATT_WIDTH = N_ATT_HEADS * HEAD_DIM
ATT_OUT_WIDTH = HEADS_PER_GROUP * HEAD_DIM
QUERY_BLOCK = 64
N_FOURIER_GROUPS = 4
FOURIER_GROUP_WIDTH = D_MODEL // 8
FOURIER_WIDTH = N_FOURIER_GROUPS * FOURIER_GROUP_WIDTH
N_BRANCHES = 2
IN_WIDTH = 3 * ATT_WIDTH + FOURIER_WIDTH + N_BRANCHES * D_MODEL
N_EXPERT_GROUPS = 4
EXPERTS_PER_GROUP = 4
N_EXPERTS = N_EXPERT_GROUPS * EXPERTS_PER_GROUP
EXPERT_TOP_K = 2
D_FF_EXPERT = D_MODEL // 4
RMS_EPS = 1e-6
NEG_INF = -1e30

kernel_name = 'dilated_fourier_hmoe_encoder'


def rms_norm(x, g):
    xf = x.astype(jnp.float32)
    y = xf * lax.rsqrt(jnp.mean(xf * xf, axis=-1, keepdims=True) + RMS_EPS) * g.astype(jnp.float32)
    return y.astype(x.dtype)


def alibi_slopes():
    s = 2.0 ** (-8.0 * np.arange(1, N_ATT_HEADS + 1) / N_ATT_HEADS)
    return jnp.asarray(s, dtype=jnp.float32).reshape(N_ATT_GROUPS, HEADS_PER_GROUP)


def dilated_band_attention(q, k, v, window, dilation, slopes):
    B, S, H, Dh = q.shape
    half = window // (2 * dilation)
    L = S // dilation
    nb = -(-L // QUERY_BLOCK)
    Lp = nb * QUERY_BLOCK
    kb_len = QUERY_BLOCK + 2 * half

    def to_classes(t):
        return t.reshape(B, L, dilation, H, Dh).transpose(0, 2, 1, 3, 4)

    qc = jnp.pad(to_classes(q), ((0, 0), (0, 0), (0, Lp - L), (0, 0), (0, 0)))
    kpad = ((0, 0), (0, 0), (half, Lp - L + half), (0, 0), (0, 0))
    kc = jnp.pad(to_classes(k), kpad)
    vc = jnp.pad(to_classes(v), kpad)
    key_idx = jnp.arange(nb)[:, None] * QUERY_BLOCK + jnp.arange(kb_len)[None, :]
    kb = kc[:, :, key_idx]
    vb = vc[:, :, key_idx]
    qb = qc.reshape(B, dilation, nb, QUERY_BLOCK, H, Dh)
    s = jnp.einsum('bcnqhe,bcnkhe->bcnhqk', qb, kb).astype(jnp.float32) * (Dh ** -0.5)
    rel = jnp.arange(kb_len)[None, :] - half - jnp.arange(QUERY_BLOCK)[:, None]
    key_pos = key_idx - half
    valid = (jnp.abs(rel) <= half)[None] & ((key_pos >= 0) & (key_pos < L))[:, None, :]
    alibi = -slopes[:, None, None] * (jnp.abs(rel) * dilation).astype(jnp.float32)[None]
    s = jnp.where(valid[None, None, :, None], s + alibi[None, None, None], NEG_INF)
    m = jnp.max(s, axis=-1, keepdims=True)
    p = jnp.exp(s - m)
    den = jnp.sum(p, axis=-1, keepdims=True)
    o = jnp.einsum('bcnhqk,bcnkhe->bcnqhe', (p / den).astype(v.dtype), vb)
    lse = (m + jnp.log(den))[..., 0]
    o = o.reshape(B, dilation, Lp, H, Dh)[:, :, :L].transpose(0, 2, 1, 3, 4).reshape(B, S, H, Dh)
    lse = lse.transpose(0, 1, 2, 4, 3).reshape(B, dilation, Lp, H)[:, :, :L]
    lse = lse.transpose(0, 2, 1, 3).reshape(B, S, H)
    return o, lse


def mixer_branches(h, w_in, w_branch_attn, w_branch_fourier, b_gate, slopes):
    B, S, _ = h.shape
    proj = h @ w_in
    cuts = [ATT_WIDTH, 2 * ATT_WIDTH, 3 * ATT_WIDTH, 3 * ATT_WIDTH + FOURIER_WIDTH]
    q, k, v, f, gate_pre = jnp.split(proj, cuts, axis=-1)
    q = q.reshape(B, S, N_ATT_GROUPS, HEADS_PER_GROUP, HEAD_DIM)
    k = k.reshape(B, S, N_ATT_GROUPS, HEADS_PER_GROUP, HEAD_DIM)
    v = v.reshape(B, S, N_ATT_GROUPS, HEADS_PER_GROUP, HEAD_DIM)
    outs, lses = [], []
    for g, (window, dilation) in enumerate(ATTN_PATTERNS):
        o_g, lse_g = dilated_band_attention(q[:, :, g], k[:, :, g], v[:, :, g], window, dilation, slopes[g])
        outs.append(o_g)
        lses.append(lse_g)
    w_grp = jax.nn.softmax(jnp.stack(lses, axis=0), axis=0)
    o = jnp.einsum('gbsh,gbshe->bshe', w_grp, jnp.stack(outs, axis=0).astype(jnp.float32))
    attn = o.astype(h.dtype).reshape(B, S, ATT_OUT_WIDTH) @ w_branch_attn
    fg = f.astype(jnp.float32).reshape(B, S, N_FOURIER_GROUPS, FOURIER_GROUP_WIDTH)
    fr = jnp.fft.fft2(fg, axes=(1, 3), norm='ortho').real.astype(h.dtype).reshape(B, S, FOURIER_WIDTH)
    four = fr @ w_branch_fourier
    gates = jax.nn.sigmoid(gate_pre.astype(jnp.float32) + b_gate.astype(jnp.float32)).astype(h.dtype)
    return gates[..., :D_MODEL] * attn + gates[..., D_MODEL:] * four


def hierarchical_moe(h, w_router_group, b_router_group, w_router_expert, b_router_expert,
                     w_expert_gate, w_expert_up, w_expert_down):
    lg = (h @ w_router_group).astype(jnp.float32) + b_router_group.astype(jnp.float32)
    pg = jax.nn.softmax(lg, axis=-1)
    gsel = jnp.argmax(lg, axis=-1)
    p_group = jnp.take_along_axis(pg, gsel[..., None], axis=-1)
    le = jnp.einsum('bsd,gde->bsge', h, w_router_expert).astype(jnp.float32) + b_router_expert.astype(jnp.float32)
    le_sel = jnp.take_along_axis(le, gsel[..., None, None], axis=2)[:, :, 0]
    top_vals, top_idx = lax.top_k(le_sel, EXPERT_TOP_K)
    p_exp = jax.nn.softmax(top_vals, axis=-1)
    expert_id = gsel[..., None] * EXPERTS_PER_GROUP + top_idx
    combine = jnp.sum(jax.nn.one_hot(expert_id, N_EXPERTS, dtype=jnp.float32) * (p_group * p_exp)[..., None], axis=-2)
    combine = combine.astype(h.dtype)
    out = jnp.zeros_like(h)
    for e in range(N_EXPERTS):
        a = jax.nn.silu(h @ w_expert_gate[e]) * (h @ w_expert_up[e])
        out = out + combine[..., e:e + 1] * (a @ w_expert_down[e])
    return out


def encoder_trunk(x, attn_norm_g, w_in, w_branch_attn, w_branch_fourier, b_gate, w_out, ffn_norm_g,
                  w_router_group, b_router_group, w_router_expert, b_router_expert,
                  w_expert_gate, w_expert_up, w_expert_down, final_norm_g):
    slopes = alibi_slopes()
    for l in range(DEPTH):
        h = rms_norm(x, attn_norm_g[l])
        merged = mixer_branches(h, w_in[l], w_branch_attn[l], w_branch_fourier[l], b_gate[l], slopes)
        x = x + merged @ w_out[l]
        h2 = rms_norm(x, ffn_norm_g[l])
        x = x + hierarchical_moe(h2, w_router_group[l], b_router_group[l], w_router_expert[l], b_router_expert[l],
                                 w_expert_gate[l], w_expert_up[l], w_expert_down[l])
    return rms_norm(x, final_norm_g)


def setup_inputs(seed: int = 0) -> dict:
    key = jax.random.key(seed)
    ks = jax.random.split(key, 18)
    f32 = jnp.float32

    def nrm(k, shape, scale):
        return jax.random.normal(k, shape, f32) * scale

    return {
        'x_prompt': nrm(ks[0], (BATCH, SEQ, D_MODEL), 1.0),
        'x_sample': nrm(ks[1], (DEC_BATCH, DEC_SEQ, D_MODEL), 1.0),
        'attn_norm_g': 1.0 + nrm(ks[2], (DEPTH, D_MODEL), 0.01),
        'w_in': nrm(ks[3], (DEPTH, D_MODEL, IN_WIDTH), D_MODEL ** -0.5),
        'w_branch_attn': nrm(ks[4], (DEPTH, ATT_OUT_WIDTH, D_MODEL), ATT_OUT_WIDTH ** -0.5),
        'w_branch_fourier': nrm(ks[5], (DEPTH, FOURIER_WIDTH, D_MODEL), FOURIER_WIDTH ** -0.5),
        'b_gate': nrm(ks[6], (DEPTH, N_BRANCHES * D_MODEL), 0.01),
        'w_out': nrm(ks[7], (DEPTH, D_MODEL, D_MODEL), D_MODEL ** -0.5),
        'ffn_norm_g': 1.0 + nrm(ks[8], (DEPTH, D_MODEL), 0.01),
        'w_router_group': nrm(ks[9], (DEPTH, D_MODEL, N_EXPERT_GROUPS), D_MODEL ** -0.5),
        'b_router_group': nrm(ks[10], (DEPTH, N_EXPERT_GROUPS), 0.01),
        'w_router_expert': nrm(ks[11], (DEPTH, N_EXPERT_GROUPS, D_MODEL, EXPERTS_PER_GROUP), D_MODEL ** -0.5),
        'b_router_expert': nrm(ks[12], (DEPTH, N_EXPERT_GROUPS, EXPERTS_PER_GROUP), 0.01),
        'w_expert_gate': nrm(ks[13], (DEPTH, N_EXPERTS, D_MODEL, D_FF_EXPERT), D_MODEL ** -0.5),
        'w_expert_up': nrm(ks[14], (DEPTH, N_EXPERTS, D_MODEL, D_FF_EXPERT), D_MODEL ** -0.5),
        'w_expert_down': nrm(ks[15], (DEPTH, N_EXPERTS, D_FF_EXPERT, D_MODEL), D_FF_EXPERT ** -0.5),
        'final_norm_g': 1.0 + nrm(ks[16], (D_MODEL,), 0.01),
    }


def reference(x_prompt, x_sample, attn_norm_g, w_in, w_branch_attn, w_branch_fourier, b_gate, w_out,
              ffn_norm_g, w_router_group, b_router_group, w_router_expert, b_router_expert,
              w_expert_gate, w_expert_up, w_expert_down, final_norm_g):
    y_prompt = encoder_trunk(x_prompt, attn_norm_g, w_in, w_branch_attn, w_branch_fourier, b_gate, w_out,
                             ffn_norm_g, w_router_group, b_router_group, w_router_expert, b_router_expert,
                             w_expert_gate, w_expert_up, w_expert_down, final_norm_g)
    y_sample = encoder_trunk(x_sample, attn_norm_g, w_in, w_branch_attn, w_branch_fourier, b_gate, w_out,
                             ffn_norm_g, w_router_group, b_router_group, w_router_expert, b_router_expert,
                             w_expert_gate, w_expert_up, w_expert_down, final_norm_g)
    return (y_prompt, y_sample)
```

```python
import functools
import math

import numpy as np
import jax
import jax.numpy as jnp
from jax import lax
from jax.experimental import pallas as pl
from jax.experimental.pallas import tpu as pltpu

F32 = jnp.float32
BF16 = jnp.bfloat16

RMS_EPS = 1e-6
NEG_INF = -1e30
HEAD_DIM = 128
ATTN_PATTERNS = ((128, 1), (512, 4), (2048, 16))
N_FOURIER_GROUPS = 4
N_EXPERT_GROUPS = 4
EXPERTS_PER_GROUP = 4
N_EXPERTS = N_EXPERT_GROUPS * EXPERTS_PER_GROUP
LANES = 128
V7X_VMEM_LIMIT = 56 * 1024 * 1024
ATTN_Q_TILE = 128
EXPERT_ROW_TILE = 512


def _tile(n, pref, mult=LANES):
    if n <= pref:
        return n
    t = (pref // mult) * mult
    while t >= mult:
        if n % t == 0:
            return t
        t -= mult
    raise ValueError(f"no tile for {n} <= {pref}")


def _params(sem, vmem=V7X_VMEM_LIMIT):
    return pltpu.CompilerParams(dimension_semantics=sem, vmem_limit_bytes=vmem)


def _rms_kernel(x_ref, g_ref, o_ref):
    x = x_ref[...]
    ms = jnp.mean(x * x, axis=-1, keepdims=True)
    o_ref[...] = (x * lax.rsqrt(ms + RMS_EPS) * g_ref[...]).astype(o_ref.dtype)


def rmsnorm_cast(x, g, out_dtype=BF16):
    T, D = x.shape
    tm = _tile(T, 256, 8)
    return pl.pallas_call(
        _rms_kernel,
        grid=(T // tm,),
        in_specs=[pl.BlockSpec((tm, D), lambda i: (i, 0)), pl.BlockSpec((1, D), lambda i: (0, 0))],
        out_specs=pl.BlockSpec((tm, D), lambda i: (i, 0)),
        out_shape=jax.ShapeDtypeStruct((T, D), out_dtype),
        compiler_params=_params(("parallel",)),
    )(x, g.reshape(1, D).astype(F32))


def _in_proj_kernel(h_ref, w_ref, b_ref, o_ref, *, gate_tile0):
    acc = jnp.dot(h_ref[...], w_ref[...], preferred_element_type=F32)
    j = pl.program_id(1)

    @pl.when(j < gate_tile0)
    def _():
        o_ref[...] = acc.astype(o_ref.dtype)

    @pl.when(j >= gate_tile0)
    def _():
        o_ref[...] = jax.nn.sigmoid(acc + b_ref[...]).astype(o_ref.dtype)


def in_projection(h, w, b_full, gate_start):
    T, K = h.shape
    N = w.shape[1]
    tm = _tile(T, 1024)
    tn = _tile(math.gcd(N, gate_start), 1024)
    return pl.pallas_call(
        functools.partial(_in_proj_kernel, gate_tile0=gate_start // tn),
        grid=(T // tm, N // tn),
        in_specs=[pl.BlockSpec((tm, K), lambda i, j: (i, 0)),
                  pl.BlockSpec((K, tn), lambda i, j: (0, j)),
                  pl.BlockSpec((1, tn), lambda i, j: (0, j))],
        out_specs=pl.BlockSpec((tm, tn), lambda i, j: (i, j)),
        out_shape=jax.ShapeDtypeStruct((T, N), BF16),
        compiler_params=_params(("parallel", "arbitrary")),
    )(h, w, b_full)


def _attn_kernel(q_ref, kp_ref, kc_ref, kn_ref, vp_ref, vc_ref, vn_ref, o_ref, lse_ref, kbuf, vbuf,
                 *, tq, half, hpg, dil, cls_len, slopes):
    i = pl.program_id(2)
    kbuf[0:half, :] = kp_ref[0]
    kbuf[half:half + tq, :] = kc_ref[0]
    kbuf[half + tq:, :] = kn_ref[0]
    vbuf[0:half, :] = vp_ref[0]
    vbuf[half:half + tq, :] = vc_ref[0]
    vbuf[half + tq:, :] = vn_ref[0]
    sub = ATTN_Q_TILE
    span = sub + 2 * half
    scale = HEAD_DIM ** -0.5
    qidx = lax.broadcasted_iota(jnp.int32, (sub, span), 0)
    kidx = lax.broadcasted_iota(jnp.int32, (sub, span), 1)
    absrel = jnp.abs(kidx - half - qidx)
    absrel_f = absrel.astype(F32)
    lane = lax.broadcasted_iota(jnp.int32, (sub, LANES), 1)
    for u in range(tq // sub):
        kpos = i * tq + (u * sub - half) + kidx
        valid = (absrel <= half) & (kpos >= 0) & (kpos < cls_len)
        lse_tile = jnp.zeros((sub, LANES), F32)
        for h in range(hpg):
            cols = slice(h * HEAD_DIM, (h + 1) * HEAD_DIM)
            q = q_ref[0, u * sub:(u + 1) * sub, cols]
            k = kbuf[u * sub:u * sub + span, cols]
            v = vbuf[u * sub:u * sub + span, cols]
            s = lax.dot_general(q, k, (((1,), (1,)), ((), ())), preferred_element_type=F32)
            s = s * scale + absrel_f * (-slopes[h] * dil)
            s = jnp.where(valid, s, NEG_INF)
            m = jnp.max(s, axis=-1, keepdims=True)
            p = jnp.exp(s - m)
            den = jnp.sum(p, axis=-1, keepdims=True)
            o = jnp.dot(p.astype(v.dtype), v, preferred_element_type=F32) / den
            o_ref[0, u * sub:(u + 1) * sub, cols] = o.astype(o_ref.dtype)
            lse_tile = jnp.where(lane == h, m + jnp.log(den), lse_tile)
        lse_ref[0, u * sub:(u + 1) * sub, :] = lse_tile


def band_attention(proj, B, S, group, hpg, n_heads_total):
    window, dil = ATTN_PATTERNS[group]
    half = window // (2 * dil)
    in_width = proj.shape[1]
    W = hpg * HEAD_DIM
    att_width = len(ATTN_PATTERNS) * W
    L = S // dil
    tq = _tile(L, 256)
    assert S % dil == 0 and L % tq == 0 and tq % ATTN_Q_TILE == 0 and tq % half == 0 and half % 16 == 0
    assert in_width % W == 0
    cpr = in_width // W
    hb = tq // half
    n_hb = L // half
    s_all = 2.0 ** (-8.0 * np.arange(1, n_heads_total + 1) / n_heads_total)
    slopes = tuple(float(np.float32(v)) for v in s_all[group * hpg:(group + 1) * hpg])
    pv = proj.reshape(B, L, dil * in_width)

    def col(part):
        return lambda b, r, i: (b, i, r * cpr + part * len(ATTN_PATTERNS) + group)

    def col_prev(part):
        return lambda b, r, i: (b, jnp.maximum(i * hb - 1, 0), r * cpr + part * len(ATTN_PATTERNS) + group)

    def col_next(part):
        return lambda b, r, i: (b, jnp.minimum((i + 1) * hb, n_hb - 1), r * cpr + part * len(ATTN_PATTERNS) + group)

    cur = lambda part: pl.BlockSpec((1, tq, W), col(part))
    prv = lambda part: pl.BlockSpec((1, half, W), col_prev(part))
    nxt = lambda part: pl.BlockSpec((1, half, W), col_next(part))
    o, lse = pl.pallas_call(
        functools.partial(_attn_kernel, tq=tq, half=half, hpg=hpg, dil=dil, cls_len=L, slopes=slopes),
        grid=(B, dil, L // tq),
        in_specs=[cur(0), prv(1), cur(1), nxt(1), prv(2), cur(2), nxt(2)],
        out_specs=[pl.BlockSpec((1, tq, W), lambda b, r, i: (b, i, r)),
                   pl.BlockSpec((1, tq, LANES), lambda b, r, i: (b, i, r))],
        out_shape=[jax.ShapeDtypeStruct((B, L, dil * W), BF16),
                   jax.ShapeDtypeStruct((B, L, dil * LANES), F32)],
        scratch_shapes=[pltpu.VMEM((tq + 2 * half, W), BF16), pltpu.VMEM((tq + 2 * half, W), BF16)],
        compiler_params=_params(("parallel", "parallel", "arbitrary")),
    )(pv, pv, pv, pv, pv, pv, pv)
    return o.reshape(B * S, W), lse.reshape(B * S, LANES)


def _dft_tables(n):
    k = jnp.arange(n, dtype=jnp.int32)
    ang = ((k[:, None] * k[None, :]) % n).astype(F32) * np.float32(2.0 * np.pi / n)
    return jnp.cos(ang), jnp.sin(ang)


def _fourier_chan_kernel(x_ref, t_ref, pc_ref, ps_ref):
    r = jnp.dot(x_ref[...], t_ref[...], preferred_element_type=F32)
    c = pc_ref.shape[-1]
    pc_ref[...] = r[:, :c].astype(pc_ref.dtype)
    ps_ref[...] = r[:, c:].astype(ps_ref.dtype)


def fourier_channel_stage(proj, f_start, cg, table):
    T = proj.shape[0]
    tm = _tile(T, 1024)
    c0 = f_start // cg
    return pl.pallas_call(
        _fourier_chan_kernel,
        grid=(T // tm, N_FOURIER_GROUPS),
        in_specs=[pl.BlockSpec((tm, cg), lambda i, g: (i, c0 + g)),
                  pl.BlockSpec((cg, 2 * cg), lambda i, g: (0, 0))],
        out_specs=[pl.BlockSpec((tm, cg), lambda i, g: (i, g)), pl.BlockSpec((tm, cg), lambda i, g: (i, g))],
        out_shape=[jax.ShapeDtypeStruct((T, N_FOURIER_GROUPS * cg), BF16)] * 2,
        compiler_params=_params(("parallel", "arbitrary")),
    )(proj, table)


def _fourier_seq_kernel(cs_ref, ss_ref, pc_ref, ps_ref, o_ref, acc_ref, *, scale):
    k = pl.program_id(3)

    @pl.when(k == 0)
    def _():
        acc_ref[...] = jnp.zeros_like(acc_ref)

    acc_ref[...] += (jnp.dot(cs_ref[...], pc_ref[0], preferred_element_type=F32)
                     + jnp.dot(ss_ref[...], ps_ref[0], preferred_element_type=F32))

    @pl.when(k == pl.num_programs(3) - 1)
    def _():
        o_ref[0] = (acc_ref[...] * scale).astype(o_ref.dtype)


def fourier_sequence_stage(pc, ps, cos_s, neg_sin_s, B, S, scale):
    N = pc.shape[1]
    tm, tn, tk = _tile(S, 1024), _tile(N, 1024), _tile(S, 2048)
    pc3, ps3 = pc.reshape(B, S, N), ps.reshape(B, S, N)
    out = pl.pallas_call(
        functools.partial(_fourier_seq_kernel, scale=scale),
        grid=(B, S // tm, N // tn, S // tk),
        in_specs=[pl.BlockSpec((tm, tk), lambda b, i, j, k: (i, k)),
                  pl.BlockSpec((tm, tk), lambda b, i, j, k: (i, k)),
                  pl.BlockSpec((1, tk, tn), lambda b, i, j, k: (b, k, j)),
                  pl.BlockSpec((1, tk, tn), lambda b, i, j, k: (b, k, j))],
        out_specs=pl.BlockSpec((1, tm, tn), lambda b, i, j, k: (b, i, j)),
        out_shape=jax.ShapeDtypeStruct((B, S, N), BF16),
        scratch_shapes=[pltpu.VMEM((tm, tn), F32)],
        compiler_params=_params(("parallel", "parallel", "parallel", "arbitrary")),
    )(cos_s, neg_sin_s, pc3, ps3)
    return out.reshape(B * S, N)


def _merge_kernel(o0_ref, o1_ref, o2_ref, l0_ref, l1_ref, l2_ref, fr_ref, wa_ref, wf_ref, ga_ref, gf_ref,
                  m_ref, oc_ref, *, hpg):
    @pl.when(pl.program_id(1) == 0)
    def _():
        l0, l1, l2 = l0_ref[...], l1_ref[...], l2_ref[...]
        m = jnp.maximum(jnp.maximum(l0, l1), l2)
        e0, e1, e2 = jnp.exp(l0 - m), jnp.exp(l1 - m), jnp.exp(l2 - m)
        den = e0 + e1 + e2
        w0, w1, w2 = e0 / den, e1 / den, e2 / den
        for h in range(hpg):
            cols = slice(h * HEAD_DIM, (h + 1) * HEAD_DIM)
            oc = (w0[:, h:h + 1] * o0_ref[:, cols].astype(F32)
                  + w1[:, h:h + 1] * o1_ref[:, cols].astype(F32)
                  + w2[:, h:h + 1] * o2_ref[:, cols].astype(F32))
            oc_ref[:, cols] = oc.astype(oc_ref.dtype)

    attn = jnp.dot(oc_ref[...], wa_ref[...], preferred_element_type=F32)
    four = jnp.dot(fr_ref[...], wf_ref[...], preferred_element_type=F32)
    m_ref[...] = (ga_ref[...].astype(F32) * attn + gf_ref[...].astype(F32) * four).astype(m_ref.dtype)


def merge_branches(os_, lses, fr, wa, wf, proj, gate_start, hpg):
    T, W = os_[0].shape
    D = wa.shape[1]
    FW = fr.shape[1]
    tm, tn = _tile(T, 512), _tile(math.gcd(D, gate_start), 1024)
    g0 = gate_start // tn
    row = lambda w: pl.BlockSpec((tm, w), lambda i, j: (i, 0))
    return pl.pallas_call(
        functools.partial(_merge_kernel, hpg=hpg),
        grid=(T // tm, D // tn),
        in_specs=[row(W), row(W), row(W), row(LANES), row(LANES), row(LANES), row(FW),
                  pl.BlockSpec((W, tn), lambda i, j: (0, j)),
                  pl.BlockSpec((FW, tn), lambda i, j: (0, j)),
                  pl.BlockSpec((tm, tn), lambda i, j: (i, g0 + j)),
                  pl.BlockSpec((tm, tn), lambda i, j: (i, g0 + D // tn + j))],
        out_specs=pl.BlockSpec((tm, tn), lambda i, j: (i, j)),
        out_shape=jax.ShapeDtypeStruct((T, D), BF16),
        scratch_shapes=[pltpu.VMEM((tm, W), BF16)],
        compiler_params=_params(("parallel", "arbitrary")),
    )(*os_, *lses, fr, wa, wf, proj, proj)


def _out_proj_kernel(m_ref, w_ref, x_ref, o_ref):
    o_ref[...] = x_ref[...] + jnp.dot(m_ref[...], w_ref[...], preferred_element_type=F32)


def out_projection(merged, w, x):
    T, K = merged.shape
    N = w.shape[1]
    tm, tn = _tile(T, 1024), _tile(N, 512)
    return pl.pallas_call(
        _out_proj_kernel,
        grid=(T // tm, N // tn),
        in_specs=[pl.BlockSpec((tm, K), lambda i, j: (i, 0)),
                  pl.BlockSpec((K, tn), lambda i, j: (0, j)),
                  pl.BlockSpec((tm, tn), lambda i, j: (i, j))],
        out_specs=pl.BlockSpec((tm, tn), lambda i, j: (i, j)),
        out_shape=jax.ShapeDtypeStruct((T, N), F32),
        compiler_params=_params(("parallel", "arbitrary")),
    )(merged, w, x)


def _router_kernel(x_ref, g_ref, w_ref, b_ref, eid_ref, cw_ref):
    x = x_ref[...]
    ms = jnp.mean(x * x, axis=-1, keepdims=True)
    h = (x * lax.rsqrt(ms + RMS_EPS) * g_ref[...]).astype(BF16)
    lg = jnp.dot(h, w_ref[...], preferred_element_type=F32) + b_ref[...]
    lane = lax.broadcasted_iota(jnp.int32, lg.shape, 1)
    big = jnp.int32(LANES)
    in_grp = lane < N_EXPERT_GROUPS
    gl = jnp.where(in_grp, lg, -jnp.inf)
    gmax = jnp.max(gl, axis=-1, keepdims=True)
    gsel = jnp.min(jnp.where(gl == gmax, lane, big), axis=-1, keepdims=True)
    p_group = 1.0 / jnp.sum(jnp.where(in_grp, jnp.exp(gl - gmax), 0.0), axis=-1, keepdims=True)
    lo = N_EXPERT_GROUPS + gsel * EXPERTS_PER_GROUP
    in_sel = (lane >= lo) & (lane < lo + EXPERTS_PER_GROUP)
    el = jnp.where(in_sel, lg, -jnp.inf)
    t1 = jnp.max(el, axis=-1, keepdims=True)
    i1 = jnp.min(jnp.where(el == t1, lane, big), axis=-1, keepdims=True)
    el2 = jnp.where(lane == i1, -jnp.inf, el)
    t2 = jnp.max(el2, axis=-1, keepdims=True)
    i2 = jnp.min(jnp.where(el2 == t2, lane, big), axis=-1, keepdims=True)
    e21 = jnp.exp(t2 - t1)
    p1 = 1.0 / (1.0 + e21)
    p2 = e21 / (1.0 + e21)
    eid_ref[...] = jnp.where(lane == 0, i1 - N_EXPERT_GROUPS, jnp.where(lane == 1, i2 - N_EXPERT_GROUPS, 0))
    cw_ref[...] = jnp.where(lane == 0, p_group * p1, jnp.where(lane == 1, p_group * p2, 0.0))


def route(x1, g, w_router, b_router):
    T, D = x1.shape
    tm = _tile(T, 256, 8)
    eid, cw = pl.pallas_call(
        _router_kernel,
        grid=(T // tm,),
        in_specs=[pl.BlockSpec((tm, D), lambda i: (i, 0)),
                  pl.BlockSpec((1, D), lambda i: (0, 0)),
                  pl.BlockSpec((D, LANES), lambda i: (0, 0)),
                  pl.BlockSpec((1, LANES), lambda i: (0, 0))],
        out_specs=[pl.BlockSpec((tm, LANES), lambda i: (i, 0)), pl.BlockSpec((tm, LANES), lambda i: (i, 0))],
        out_shape=[jax.ShapeDtypeStruct((T, LANES), jnp.int32), jax.ShapeDtypeStruct((T, LANES), F32)],
        compiler_params=_params(("parallel",)),
    )(x1, g.reshape(1, D).astype(F32), w_router, b_router)
    return eid[:, :2], cw[:, :2]


def _gather_norm_kernel(src_ref, nused_ref, x_hbm, g_ref, o_ref, buf, sem, *, tme):
    i = pl.program_id(0)

    def row_copy(r):
        row = src_ref[i * tme + r]
        return pltpu.make_async_copy(x_hbm.at[pl.ds(row, 1), :], buf.at[pl.ds(r, 1), :], sem)

    @pl.when(i < nused_ref[0])
    def _():
        def issue(r, c):
            row_copy(r).start()
            return c

        def drain(r, c):
            row_copy(r).wait()
            return c

        lax.fori_loop(0, tme, issue, 0, unroll=8)
        lax.fori_loop(0, tme, drain, 0, unroll=8)
        x = buf[...]
        ms = jnp.mean(x * x, axis=-1, keepdims=True)
        o_ref[...] = (x * lax.rsqrt(ms + RMS_EPS) * g_ref[...]).astype(o_ref.dtype)

    @pl.when(i >= nused_ref[0])
    def _():
        o_ref[...] = jnp.zeros_like(o_ref)


def gather_norm(x1, g, src, n_used, n_tiles, tme):
    T, D = x1.shape
    return pl.pallas_call(
        functools.partial(_gather_norm_kernel, tme=tme),
        grid_spec=pltpu.PrefetchScalarGridSpec(
            num_scalar_prefetch=2,
            grid=(n_tiles,),
            in_specs=[pl.BlockSpec(memory_space=pl.ANY), pl.BlockSpec((1, D), lambda i, s, n: (0, 0))],
            out_specs=pl.BlockSpec((tme, D), lambda i, s, n: (i, 0)),
            scratch_shapes=[pltpu.VMEM((tme, D), F32), pltpu.SemaphoreType.DMA],
        ),
        out_shape=jax.ShapeDtypeStruct((n_tiles * tme, D), BF16),
        compiler_params=_params(("arbitrary",)),
    )(src, n_used, x1, g.reshape(1, D).astype(F32))


def _expert_up_kernel(te_ref, nused_ref, x_ref, wg_ref, wu_ref, a_ref):
    @pl.when(pl.program_id(0) < nused_ref[0])
    def _():
        x = x_ref[...]
        gate = jnp.dot(x, wg_ref[0], preferred_element_type=F32)
        up = jnp.dot(x, wu_ref[0], preferred_element_type=F32)
        a_ref[...] = (jax.nn.silu(gate) * up).astype(a_ref.dtype)

    @pl.when(pl.program_id(0) >= nused_ref[0])
    def _():
        a_ref[...] = jnp.zeros_like(a_ref)


def _expert_down_kernel(te_ref, nused_ref, a_ref, wd_ref, cw_ref, y_ref):
    @pl.when(pl.program_id(0) < nused_ref[0])
    def _():
        y_ref[...] = cw_ref[...] * jnp.dot(a_ref[...], wd_ref[0], preferred_element_type=F32)

    @pl.when(pl.program_id(0) >= nused_ref[0])
    def _():
        y_ref[...] = jnp.zeros_like(y_ref)


def expert_ffn(xs, w_gate, w_up, w_down, cws, tile_expert, n_used, tme):
    P, D = xs.shape
    F = w_gate.shape[2]
    n_tiles = P // tme
    act = pl.pallas_call(
        _expert_up_kernel,
        grid_spec=pltpu.PrefetchScalarGridSpec(
            num_scalar_prefetch=2,
            grid=(n_tiles,),
            in_specs=[pl.BlockSpec((tme, D), lambda i, te, n: (i, 0)),
                      pl.BlockSpec((1, D, F), lambda i, te, n: (te[i], 0, 0)),
                      pl.BlockSpec((1, D, F), lambda i, te, n: (te[i], 0, 0))],
            out_specs=pl.BlockSpec((tme, F), lambda i, te, n: (i, 0)),
        ),
        out_shape=jax.ShapeDtypeStruct((P, F), BF16),
        compiler_params=_params(("arbitrary",)),
    )(tile_expert, n_used, xs, w_gate, w_up)
    return pl.pallas_call(
        _expert_down_kernel,
        grid_spec=pltpu.PrefetchScalarGridSpec(
            num_scalar_prefetch=2,
            grid=(n_tiles,),
            in_specs=[pl.BlockSpec((tme, F), lambda i, te, n: (i, 0)),
                      pl.BlockSpec((1, F, D), lambda i, te, n: (te[i], 0, 0)),
                      pl.BlockSpec((tme, 1), lambda i, te, n: (i, 0))],
            out_specs=pl.BlockSpec((tme, D), lambda i, te, n: (i, 0)),
        ),
        out_shape=jax.ShapeDtypeStruct((P, D), F32),
        compiler_params=_params(("arbitrary",)),
    )(tile_expert, n_used, act, w_down, cws)


def _final_kernel(p0_ref, p1_ref, x_ref, y_hbm, g_ref, o_ref, b0, b1, sem, *, tm):
    i = pl.program_id(0)

    def copies(r):
        t = i * tm + r
        return (pltpu.make_async_copy(y_hbm.at[pl.ds(p0_ref[t], 1), :], b0.at[pl.ds(r, 1), :], sem.at[0]),
                pltpu.make_async_copy(y_hbm.at[pl.ds(p1_ref[t], 1), :], b1.at[pl.ds(r, 1), :], sem.at[1]))

    def issue(r, c):
        c0, c1 = copies(r)
        c0.start()
        c1.start()
        return c

    def drain(r, c):
        c0, c1 = copies(r)
        c0.wait()
        c1.wait()
        return c

    lax.fori_loop(0, tm, issue, 0, unroll=8)
    lax.fori_loop(0, tm, drain, 0, unroll=8)
    x = x_ref[...] + (b0[...] + b1[...])
    ms = jnp.mean(x * x, axis=-1, keepdims=True)
    o_ref[...] = x * lax.rsqrt(ms + RMS_EPS) * g_ref[...]


def final_combine(x1, y, pos0, pos1, g):
    T, D = x1.shape
    tm = _tile(T, 256, 8)
    return pl.pallas_call(
        functools.partial(_final_kernel, tm=tm),
        grid_spec=pltpu.PrefetchScalarGridSpec(
            num_scalar_prefetch=2,
            grid=(T // tm,),
            in_specs=[pl.BlockSpec((tm, D), lambda i, a, b: (i, 0)),
                      pl.BlockSpec(memory_space=pl.ANY),
                      pl.BlockSpec((1, D), lambda i, a, b: (0, 0))],
            out_specs=pl.BlockSpec((tm, D), lambda i, a, b: (i, 0)),
            scratch_shapes=[pltpu.VMEM((tm, D), F32), pltpu.VMEM((tm, D), F32), pltpu.SemaphoreType.DMA((2,))],
        ),
        out_shape=jax.ShapeDtypeStruct((T, D), F32),
        compiler_params=_params(("arbitrary",)),
    )(pos0, pos1, x1, y, g.reshape(1, D).astype(F32))


def _sorted_layout(eid, cw, tme):
    T = eid.shape[0]
    n_tiles = (2 * T) // tme + N_EXPERTS
    P = n_tiles * tme
    flat_e = eid.reshape(-1)
    onehot = (flat_e[:, None] == jnp.arange(N_EXPERTS, dtype=jnp.int32)[None, :]).astype(jnp.int32)
    csum = jnp.cumsum(onehot, axis=0)
    rank = jnp.take_along_axis(csum, flat_e[:, None], axis=1)[:, 0] - 1
    counts = csum[-1]
    padded = ((counts + tme - 1) // tme) * tme
    ends = jnp.cumsum(padded)
    pos = (ends - padded)[flat_e] + rank
    src = jnp.zeros((P,), jnp.int32).at[pos].set(jnp.arange(2 * T, dtype=jnp.int32) // 2)
    cws = jnp.zeros((P,), F32).at[pos].set(cw.reshape(-1))
    n_used = (ends[-1] // tme).astype(jnp.int32)
    tile_start = jnp.minimum(jnp.arange(n_tiles, dtype=jnp.int32), n_used - 1) * tme
    tile_expert = jnp.minimum(jnp.searchsorted(ends, tile_start, side="right"), N_EXPERTS - 1).astype(jnp.int32)
    pos2 = pos.reshape(T, 2)
    return src, cws.reshape(P, 1), tile_expert, n_used.reshape(1), pos2[:, 0], pos2[:, 1], n_tiles


def _encoder_trunk(x, wts):
    B, S, D = x.shape
    T = B * S
    hpg, cg = wts["hpg"], wts["cg"]
    W = hpg * HEAD_DIM
    att_width = len(ATTN_PATTERNS) * W
    f_start = 3 * att_width
    gate_start = f_start + N_FOURIER_GROUPS * cg
    x2d = x.reshape(T, D)

    h = rmsnorm_cast(x2d, wts["attn_norm_g"])
    proj = in_projection(h, wts["w_in"], wts["b_full"], gate_start)

    outs, lses = [], []
    for g in range(len(ATTN_PATTERNS)):
        o_g, lse_g = band_attention(proj, B, S, g, hpg, len(ATTN_PATTERNS) * hpg)
        outs.append(o_g)
        lses.append(lse_g)

    pc, ps = fourier_channel_stage(proj, f_start, cg, wts["chan_table"])
    cos_s, sin_s = _dft_tables(S)
    fr = fourier_sequence_stage(pc, ps, cos_s.astype(BF16), (-sin_s).astype(BF16), B, S,
                                float(1.0 / math.sqrt(S * cg)))

    merged = merge_branches(outs, lses, fr, wts["w_branch_attn"], wts["w_branch_fourier"], proj, gate_start, hpg)
    x1 = out_projection(merged, wts["w_out"], x2d)

    eid, cw = route(x1, wts["ffn_norm_g"], wts["w_router"], wts["b_router"])
    tme = _tile(2 * T, EXPERT_ROW_TILE, 8)
    src, cws, tile_expert, n_used, pos0, pos1, n_tiles = _sorted_layout(eid, cw, tme)
    xs = gather_norm(x1, wts["ffn_norm_g"], src, n_used, n_tiles, tme)
    y = expert_ffn(xs, wts["w_expert_gate"], wts["w_expert_up"], wts["w_expert_down"], cws, tile_expert, n_used, tme)
    out = final_combine(x1, y, pos0, pos1, wts["final_norm_g"])
    return out.reshape(B, S, D)


def kernel(x_prompt, x_sample, attn_norm_g, w_in, w_branch_attn, w_branch_fourier, b_gate, w_out, ffn_norm_g, w_router_group, b_router_group, w_router_expert, b_router_expert, w_expert_gate, w_expert_up, w_expert_down, final_norm_g):
    assert w_in.shape[0] == 1, "the final norm is fused into the layer's last kernel: one layer only"
    l = 0
    D = x_prompt.shape[-1]
    hpg = w_branch_attn.shape[1] // HEAD_DIM
    cg = w_branch_fourier.shape[1] // N_FOURIER_GROUPS
    in_width = w_in.shape[2]
    n_gate = b_gate.shape[1]
    cos_c, sin_c = _dft_tables(cg)
    w_r = jnp.concatenate(
        [w_router_group[l], jnp.transpose(w_router_expert[l], (1, 0, 2)).reshape(D, N_EXPERTS)], axis=1)
    b_r = jnp.concatenate([b_router_group[l], b_router_expert[l].reshape(-1)])
    pad = LANES - w_r.shape[1]
    wts = dict(
        hpg=hpg, cg=cg, chan_table=jnp.concatenate([cos_c, sin_c], axis=1).astype(BF16),
        attn_norm_g=attn_norm_g[l], ffn_norm_g=ffn_norm_g[l], final_norm_g=final_norm_g,
        w_in=w_in[l].astype(BF16),
        b_full=jnp.concatenate([jnp.zeros((in_width - n_gate,), F32), b_gate[l].astype(F32)]).reshape(1, in_width),
        w_branch_attn=w_branch_attn[l].astype(BF16),
        w_branch_fourier=w_branch_fourier[l].astype(BF16),
        w_out=w_out[l].astype(BF16),
        w_router=jnp.pad(w_r, ((0, 0), (0, pad))).astype(BF16),
        b_router=jnp.pad(b_r, (0, pad)).reshape(1, LANES).astype(F32),
        w_expert_gate=w_expert_gate[l].astype(BF16),
        w_expert_up=w_expert_up[l].astype(BF16),
        w_expert_down=w_expert_down[l].astype(BF16),
    )
    return tuple(_encoder_trunk(x, wts) for x in (x_prompt, x_sample))
```

```python
import functools
import math

import numpy as np
import jax
import jax.numpy as jnp
from jax import lax
from jax.experimental import pallas as pl
from jax.experimental.pallas import tpu as pltpu

F32 = jnp.float32
BF16 = jnp.bfloat16
U32 = jnp.uint32

RMS_EPS = 1e-6
NEG_INF = -1e30
HEAD_DIM = 128
ATTN_PATTERNS = ((128, 1), (512, 4), (2048, 16))
N_FOURIER_GROUPS = 4
N_EXPERT_GROUPS = 4
EXPERTS_PER_GROUP = 4
N_EXPERTS = N_EXPERT_GROUPS * EXPERTS_PER_GROUP
LANES = 128
V7X_VMEM_LIMIT = 56 * 1024 * 1024
ATTN_Q_TILE = 128
ATTN_ROW_TILE = 1024
EXPERT_ROW_TILE = 512
HI_MASK = 0xFFFF0000


def _tile(n, pref, mult=LANES):
    if n <= pref:
        return n
    t = (pref // mult) * mult
    while t >= mult:
        if n % t == 0:
            return t
        t -= mult
    raise ValueError(f"no tile for {n} <= {pref}")


def _params(sem, vmem=V7X_VMEM_LIMIT):
    return pltpu.CompilerParams(dimension_semantics=sem, vmem_limit_bytes=vmem)


def _pack_pairs(lo, hi):
    a = pltpu.bitcast(lo.astype(BF16).astype(F32), U32) >> 16
    b = pltpu.bitcast(hi.astype(BF16).astype(F32), U32) & jnp.uint32(HI_MASK)
    return a | b


def _unpack_pairs(w):
    return pltpu.bitcast(w << 16, F32), pltpu.bitcast(w & jnp.uint32(HI_MASK), F32)


def _rms_kernel(x_ref, g_ref, *refs, dils):
    o_ref = refs[0]
    cm_refs = refs[1:1 + len(dils)]
    scr = refs[-1] if dils else None
    x = x_ref[...]
    ms = jnp.mean(x * x, axis=-1, keepdims=True)
    h = x * lax.rsqrt(ms + RMS_EPS) * g_ref[...]
    o_ref[...] = h.astype(o_ref.dtype)
    if not dils:
        return
    tm = x.shape[0]
    nch = x.shape[1] // LANES
    for c in range(nch):
        scr[c] = h[:, c * LANES:(c + 1) * LANES]
    for cm_ref, d in zip(cm_refs, dils):
        for r in range(d):
            for c in range(nch):
                cm_ref[0, r, :, c * LANES:(c + 1) * LANES] = scr[c, pl.ds(r, tm // d, stride=d), :].astype(cm_ref.dtype)


def rmsnorm_cast(x, g, B, S, dils):
    T, D = x.shape
    tm = _tile(S, 256, 16 * max(dils, default=1))
    spb = S // tm
    outs = pl.pallas_call(
        functools.partial(_rms_kernel, dils=tuple(dils)),
        grid=(T // tm,),
        in_specs=[pl.BlockSpec((tm, D), lambda i: (i, 0)), pl.BlockSpec((1, D), lambda i: (0, 0))],
        out_specs=[pl.BlockSpec((tm, D), lambda i: (i, 0))]
        + [pl.BlockSpec((1, d, tm // d, D), lambda i: (i // spb, 0, i % spb, 0)) for d in dils],
        out_shape=[jax.ShapeDtypeStruct((T, D), BF16)]
        + [jax.ShapeDtypeStruct((B, d, S // d, D), BF16) for d in dils],
        scratch_shapes=[pltpu.VMEM((D // LANES, tm, LANES), F32)] if dils else [],
        compiler_params=_params(("parallel",)),
    )(x, g.reshape(1, D).astype(F32))
    return outs[0], {d: o.reshape(T, D) for d, o in zip(dils, outs[1:])}


def _in_proj_kernel(h_ref, w_ref, b_ref, o_ref, *, gate_tile0):
    acc = jnp.dot(h_ref[...], w_ref[...], preferred_element_type=F32)
    if gate_tile0 is None:
        o_ref[...] = acc.astype(o_ref.dtype)
        return
    j = pl.program_id(1)

    @pl.when(j < gate_tile0)
    def _():
        o_ref[...] = acc.astype(o_ref.dtype)

    @pl.when(j >= gate_tile0)
    def _():
        o_ref[...] = jax.nn.sigmoid(acc + b_ref[...]).astype(o_ref.dtype)


def in_projection(h, w, b_full, tn, n_out, w_col, gate_tile0=None):
    T, K = h.shape
    tm = _tile(T, 1024)
    return pl.pallas_call(
        functools.partial(_in_proj_kernel, gate_tile0=gate_tile0),
        grid=(T // tm, n_out),
        in_specs=[pl.BlockSpec((tm, K), lambda i, j: (i, 0)),
                  pl.BlockSpec((K, tn), lambda i, j: (0, w_col(j))),
                  pl.BlockSpec((1, tn), lambda i, j: (0, w_col(j)))],
        out_specs=pl.BlockSpec((tm, tn), lambda i, j: (i, j)),
        out_shape=jax.ShapeDtypeStruct((T, n_out * tn), BF16),
        compiler_params=_params(("parallel", "arbitrary")),
    )(h, w, b_full)


def _attn_kernel(q_ref, kp_ref, kc_ref, kn_ref, vp_ref, vc_ref, vn_ref, o_ref, lse_ref, so, sl,
                 *, tq, sub, half, hps, dil, cls_len, slopes):
    i = pl.program_id(1)
    hh = pl.program_id(2)
    key_pad = -(sub + 2 * half) % LANES
    span = sub + 2 * half + key_pad
    scale = HEAD_DIM ** -0.5
    qidx = lax.broadcasted_iota(jnp.int32, (sub, span), 0)
    kidx = lax.broadcasted_iota(jnp.int32, (sub, span), 1)
    absrel = jnp.abs(kidx - half - qidx)
    absrel_f = absrel.astype(F32)
    lane = lax.broadcasted_iota(jnp.int32, (sub, LANES), 1)
    n_sub = tq // sub
    valids = []
    for u in range(n_sub):
        kpos = i * tq + (u * sub - half) + kidx
        valids.append((absrel <= half) & (kpos >= 0) & (kpos < cls_len))
    neg_slopes = []
    for h in range(hps):
        s_h = jnp.float32(0.0)
        for b in range(len(slopes) // hps):
            s_h = jnp.where(hh == b, jnp.float32(-slopes[b * hps + h] * dil), s_h)
        neg_slopes.append(s_h)

    @pl.when(hh == 0)
    def _():
        sl[...] = jnp.zeros_like(sl)

    def keys(prev_ref, cur_ref, next_ref, r, u, cols):
        lo = u * sub - half if u > 0 else 0
        hi = (u + 1) * sub + half if u < n_sub - 1 else tq
        parts = [cur_ref[0, r, lo:hi, cols]]
        if u == 0:
            parts.insert(0, prev_ref[0, r, :, cols])
        if u == n_sub - 1:
            parts.append(next_ref[0, r, :, cols])
        if key_pad:
            parts.append(jnp.zeros((key_pad, HEAD_DIM), cur_ref.dtype))
        return parts[0] if len(parts) == 1 else jnp.concatenate(parts, axis=0)

    for r in range(dil):
        for u in range(n_sub):
            rows = pl.ds(u * sub * dil + r, sub, stride=dil) if dil > 1 else pl.ds(u * sub, sub)
            lse_tile = sl[rows, :]
            for h in range(hps):
                cols = slice(h * HEAD_DIM, (h + 1) * HEAD_DIM)
                q = q_ref[0, r, u * sub:(u + 1) * sub, cols]
                k = keys(kp_ref, kc_ref, kn_ref, r, u, cols)
                v = keys(vp_ref, vc_ref, vn_ref, r, u, cols)
                s = lax.dot_general(q, k, (((1,), (1,)), ((), ())), preferred_element_type=F32)
                s = s * scale + absrel_f * neg_slopes[h]
                s = jnp.where(valids[u], s, NEG_INF)
                m = jnp.max(s, axis=-1, keepdims=True)
                p = jnp.exp(s - m)
                den = jnp.sum(p, axis=-1, keepdims=True)
                so[h, rows, :] = jnp.dot(p.astype(v.dtype), v, preferred_element_type=F32) / den
                lse_tile = jnp.where(lane == hh * hps + h, m + jnp.log(den), lse_tile)
            sl[rows, :] = lse_tile
    for h in range(hps):
        o_ref[0, :, h * HEAD_DIM:(h + 1) * HEAD_DIM] = so[h].astype(o_ref.dtype)
    lse_ref[0] = sl[...]


def band_attention(qkv, B, S, group, hpg, n_heads_total):
    window, dil = ATTN_PATTERNS[group]
    half = window // (2 * dil)
    W = hpg * HEAD_DIM
    L = S // dil
    tq = min(L, max(ATTN_Q_TILE, ATTN_ROW_TILE // dil))
    rows = tq * dil
    sub = min(ATTN_Q_TILE, tq)
    hps = max(1, hpg * ATTN_ROW_TILE // max(rows, ATTN_ROW_TILE))
    assert S % dil == 0 and L % tq == 0 and tq % sub == 0 and tq % half == 0 and half % 16 == 0 and hpg % hps == 0
    n_hh = hpg // hps
    Wb = hps * HEAD_DIM
    hb = tq // half
    n_hb = L // half
    s_all = 2.0 ** (-8.0 * np.arange(1, n_heads_total + 1) / n_heads_total)
    slopes = tuple(float(np.float32(v)) for v in s_all[group * hpg:(group + 1) * hpg])
    qv = qkv.reshape(B, dil, L, 3 * W)

    cur = lambda part: pl.BlockSpec((1, dil, tq, Wb), lambda b, i, hh: (b, 0, i, part * n_hh + hh))
    prv = lambda part: pl.BlockSpec(
        (1, dil, half, Wb), lambda b, i, hh: (b, 0, jnp.maximum(i * hb - 1, 0), part * n_hh + hh))
    nxt = lambda part: pl.BlockSpec(
        (1, dil, half, Wb), lambda b, i, hh: (b, 0, jnp.minimum((i + 1) * hb, n_hb - 1), part * n_hh + hh))
    o, lse = pl.pallas_call(
        functools.partial(_attn_kernel, tq=tq, sub=sub, half=half, hps=hps, dil=dil, cls_len=L, slopes=slopes),
        grid=(B, L // tq, n_hh),
        in_specs=[cur(0), prv(1), cur(1), nxt(1), prv(2), cur(2), nxt(2)],
        out_specs=[pl.BlockSpec((1, rows, Wb), lambda b, i, hh: (b, i, hh)),
                   pl.BlockSpec((1, rows, LANES), lambda b, i, hh: (b, i, 0))],
        out_shape=[jax.ShapeDtypeStruct((B, S, W), BF16), jax.ShapeDtypeStruct((B, S, LANES), F32)],
        scratch_shapes=[pltpu.VMEM((hps, rows, LANES), F32), pltpu.VMEM((rows, LANES), F32)],
        compiler_params=_params(("parallel", "arbitrary", "arbitrary")),
    )(qv, qv, qv, qv, qv, qv, qv)
    return o.reshape(B * S, W), lse.reshape(B * S, LANES)


def _dft_tables(n):
    k = jnp.arange(n, dtype=jnp.int32)
    ang = ((k[:, None] * k[None, :]) % n).astype(F32) * np.float32(2.0 * np.pi / n)
    return jnp.cos(ang), jnp.sin(ang)


def _fourier_chan_kernel(x_ref, t_ref, pc_ref, ps_ref):
    r = jnp.dot(x_ref[...], t_ref[...], preferred_element_type=F32)
    c = pc_ref.shape[-1]
    pc_ref[...] = r[:, :c].astype(pc_ref.dtype)
    ps_ref[...] = r[:, c:].astype(ps_ref.dtype)


def fourier_channel_stage(fg, cg, table):
    T = fg.shape[0]
    tm = _tile(T, 1024)
    return pl.pallas_call(
        _fourier_chan_kernel,
        grid=(T // tm, N_FOURIER_GROUPS),
        in_specs=[pl.BlockSpec((tm, cg), lambda i, g: (i, g)),
                  pl.BlockSpec((cg, 2 * cg), lambda i, g: (0, 0))],
        out_specs=[pl.BlockSpec((tm, cg), lambda i, g: (i, g)), pl.BlockSpec((tm, cg), lambda i, g: (i, g))],
        out_shape=[jax.ShapeDtypeStruct((T, N_FOURIER_GROUPS * cg), BF16)] * 2,
        compiler_params=_params(("parallel", "arbitrary")),
    )(fg, table)


def _fourier_seq_kernel(cs_ref, ss_ref, pc_ref, ps_ref, o_ref, acc_ref, *, scale):
    k = pl.program_id(3)

    @pl.when(k == 0)
    def _():
        acc_ref[...] = jnp.zeros_like(acc_ref)

    acc_ref[...] += (jnp.dot(cs_ref[...], pc_ref[0], preferred_element_type=F32)
                     + jnp.dot(ss_ref[...], ps_ref[0], preferred_element_type=F32))

    @pl.when(k == pl.num_programs(3) - 1)
    def _():
        o_ref[0] = (acc_ref[...] * scale).astype(o_ref.dtype)


def fourier_sequence_stage(pc, ps, cos_s, neg_sin_s, B, S, scale):
    N = pc.shape[1]
    tm, tn, tk = _tile(S, 1024), _tile(N, 1024), _tile(S, 2048)
    pc3, ps3 = pc.reshape(B, S, N), ps.reshape(B, S, N)
    out = pl.pallas_call(
        functools.partial(_fourier_seq_kernel, scale=scale),
        grid=(B, S // tm, N // tn, S // tk),
        in_specs=[pl.BlockSpec((tm, tk), lambda b, i, j, k: (i, k)),
                  pl.BlockSpec((tm, tk), lambda b, i, j, k: (i, k)),
                  pl.BlockSpec((1, tk, tn), lambda b, i, j, k: (b, k, j)),
                  pl.BlockSpec((1, tk, tn), lambda b, i, j, k: (b, k, j))],
        out_specs=pl.BlockSpec((1, tm, tn), lambda b, i, j, k: (b, i, j)),
        out_shape=jax.ShapeDtypeStruct((B, S, N), BF16),
        scratch_shapes=[pltpu.VMEM((tm, tn), F32)],
        compiler_params=_params(("parallel", "parallel", "parallel", "arbitrary")),
    )(cos_s, neg_sin_s, pc3, ps3)
    return out.reshape(B * S, N)


def _merge_kernel(o0_ref, o1_ref, o2_ref, l0_ref, l1_ref, l2_ref, fr_ref, wa_ref, wf_ref, ga_ref, gf_ref,
                  m_ref, oc_ref, *, hpg):
    @pl.when(pl.program_id(1) == 0)
    def _():
        l0, l1, l2 = l0_ref[...], l1_ref[...], l2_ref[...]
        m = jnp.maximum(jnp.maximum(l0, l1), l2)
        e0, e1, e2 = jnp.exp(l0 - m), jnp.exp(l1 - m), jnp.exp(l2 - m)
        den = e0 + e1 + e2
        w0, w1, w2 = e0 / den, e1 / den, e2 / den
        for h in range(hpg):
            cols = slice(h * HEAD_DIM, (h + 1) * HEAD_DIM)
            oc = (w0[:, h:h + 1] * o0_ref[:, cols].astype(F32)
                  + w1[:, h:h + 1] * o1_ref[:, cols].astype(F32)
                  + w2[:, h:h + 1] * o2_ref[:, cols].astype(F32))
            oc_ref[:, cols] = oc.astype(oc_ref.dtype)

    attn = jnp.dot(oc_ref[...], wa_ref[...], preferred_element_type=F32)
    four = jnp.dot(fr_ref[...], wf_ref[...], preferred_element_type=F32)
    m_ref[...] = (ga_ref[...].astype(F32) * attn + gf_ref[...].astype(F32) * four).astype(m_ref.dtype)


def merge_branches(os_, lses, fr, wa, wf, fg, gate_start, hpg):
    T, W = os_[0].shape
    D = wa.shape[1]
    FW = fr.shape[1]
    tm, tn = _tile(T, 512), _tile(math.gcd(D, gate_start), 1024)
    g0 = gate_start // tn
    row = lambda w: pl.BlockSpec((tm, w), lambda i, j: (i, 0))
    return pl.pallas_call(
        functools.partial(_merge_kernel, hpg=hpg),
        grid=(T // tm, D // tn),
        in_specs=[row(W), row(W), row(W), row(LANES), row(LANES), row(LANES), row(FW),
                  pl.BlockSpec((W, tn), lambda i, j: (0, j)),
                  pl.BlockSpec((FW, tn), lambda i, j: (0, j)),
                  pl.BlockSpec((tm, tn), lambda i, j: (i, g0 + j)),
                  pl.BlockSpec((tm, tn), lambda i, j: (i, g0 + D // tn + j))],
        out_specs=pl.BlockSpec((tm, tn), lambda i, j: (i, j)),
        out_shape=jax.ShapeDtypeStruct((T, D), BF16),
        scratch_shapes=[pltpu.VMEM((tm, W), BF16)],
        compiler_params=_params(("parallel", "arbitrary")),
    )(*os_, *lses, fr, wa, wf, fg, fg)


def _out_proj_kernel(m_ref, w_ref, x_ref, o_ref):
    o_ref[...] = x_ref[...] + jnp.dot(m_ref[...], w_ref[...], preferred_element_type=F32)


def out_projection(merged, w, x):
    T, K = merged.shape
    N = w.shape[1]
    tm, tn = _tile(T, 1024), _tile(N, 512)
    return pl.pallas_call(
        _out_proj_kernel,
        grid=(T // tm, N // tn),
        in_specs=[pl.BlockSpec((tm, K), lambda i, j: (i, 0)),
                  pl.BlockSpec((K, tn), lambda i, j: (0, j)),
                  pl.BlockSpec((tm, tn), lambda i, j: (i, j))],
        out_specs=pl.BlockSpec((tm, tn), lambda i, j: (i, j)),
        out_shape=jax.ShapeDtypeStruct((T, N), F32),
        compiler_params=_params(("parallel", "arbitrary")),
    )(merged, w, x)


def _router_kernel(x_ref, g_ref, w_ref, b_ref, eid_ref, cw_ref, hp_ref):
    x = x_ref[...]
    ms = jnp.mean(x * x, axis=-1, keepdims=True)
    hf = x * lax.rsqrt(ms + RMS_EPS) * g_ref[...]
    h = hf.astype(BF16)
    half_d = hf.shape[1] // 2
    packed = _pack_pairs(hf[:, :half_d], hf[:, half_d:])
    for c in range(hp_ref.shape[1]):
        hp_ref[:, c, :] = packed[:, c * LANES:(c + 1) * LANES]
    lg = jnp.dot(h, w_ref[...], preferred_element_type=F32) + b_ref[...]
    lane = lax.broadcasted_iota(jnp.int32, lg.shape, 1)
    big = jnp.int32(LANES)
    in_grp = lane < N_EXPERT_GROUPS
    gl = jnp.where(in_grp, lg, -jnp.inf)
    gmax = jnp.max(gl, axis=-1, keepdims=True)
    gsel = jnp.min(jnp.where(gl == gmax, lane, big), axis=-1, keepdims=True)
    p_group = 1.0 / jnp.sum(jnp.where(in_grp, jnp.exp(gl - gmax), 0.0), axis=-1, keepdims=True)
    lo = N_EXPERT_GROUPS + gsel * EXPERTS_PER_GROUP
    in_sel = (lane >= lo) & (lane < lo + EXPERTS_PER_GROUP)
    el = jnp.where(in_sel, lg, -jnp.inf)
    t1 = jnp.max(el, axis=-1, keepdims=True)
    i1 = jnp.min(jnp.where(el == t1, lane, big), axis=-1, keepdims=True)
    el2 = jnp.where(lane == i1, -jnp.inf, el)
    t2 = jnp.max(el2, axis=-1, keepdims=True)
    i2 = jnp.min(jnp.where(el2 == t2, lane, big), axis=-1, keepdims=True)
    e21 = jnp.exp(t2 - t1)
    p1 = 1.0 / (1.0 + e21)
    p2 = e21 / (1.0 + e21)
    eid_ref[...] = jnp.where(lane == 0, i1 - N_EXPERT_GROUPS, jnp.where(lane == 1, i2 - N_EXPERT_GROUPS, 0))
    cw_ref[...] = jnp.where(lane == 0, p_group * p1, jnp.where(lane == 1, p_group * p2, 0.0))


def route(x1, g, w_router, b_router):
    T, D = x1.shape
    tm = _tile(T, 256, 8)
    nc = D // (2 * LANES)
    return pl.pallas_call(
        _router_kernel,
        grid=(T // tm,),
        in_specs=[pl.BlockSpec((tm, D), lambda i: (i, 0)),
                  pl.BlockSpec((1, D), lambda i: (0, 0)),
                  pl.BlockSpec((D, LANES), lambda i: (0, 0)),
                  pl.BlockSpec((1, LANES), lambda i: (0, 0))],
        out_specs=[pl.BlockSpec((tm, LANES), lambda i: (i, 0)), pl.BlockSpec((tm, LANES), lambda i: (i, 0)),
                   pl.BlockSpec((tm, nc, LANES), lambda i: (i, 0, 0))],
        out_shape=[jax.ShapeDtypeStruct((T, LANES), jnp.int32), jax.ShapeDtypeStruct((T, LANES), F32),
                   jax.ShapeDtypeStruct((T, nc, LANES), U32)],
        compiler_params=_params(("parallel",)),
    )(x1, g.reshape(1, D).astype(F32), w_router, b_router)


def _gather_kernel(src_ref, nused_ref, hp_hbm, o_ref, buf, sem, *, tme, nc):
    i = pl.program_id(0)

    def row_copy(r):
        return pltpu.make_async_copy(hp_hbm.at[src_ref[i * tme + r]], buf.at[pl.ds(r * nc, nc), :], sem)

    @pl.when(i < nused_ref[0])
    def _():
        def issue(r, c):
            row_copy(r).start()
            return c

        def drain(r, c):
            row_copy(r).wait()
            return c

        lax.fori_loop(0, tme, issue, 0, unroll=8)
        lax.fori_loop(0, tme, drain, 0, unroll=8)
        for c in range(nc):
            lo, hi = _unpack_pairs(buf[pl.ds(c, tme, stride=nc), :])
            o_ref[:, c * LANES:(c + 1) * LANES] = lo.astype(o_ref.dtype)
            o_ref[:, (nc + c) * LANES:(nc + c + 1) * LANES] = hi.astype(o_ref.dtype)

    @pl.when(i >= nused_ref[0])
    def _():
        o_ref[...] = jnp.zeros_like(o_ref)


def gather_rows(hp, src, n_used, n_tiles, tme):
    T, nc, _ = hp.shape
    D = nc * 2 * LANES
    return pl.pallas_call(
        functools.partial(_gather_kernel, tme=tme, nc=nc),
        grid_spec=pltpu.PrefetchScalarGridSpec(
            num_scalar_prefetch=2,
            grid=(n_tiles,),
            in_specs=[pl.BlockSpec(memory_space=pl.ANY)],
            out_specs=pl.BlockSpec((tme, D), lambda i, s, n: (i, 0)),
            scratch_shapes=[pltpu.VMEM((tme * nc, LANES), U32), pltpu.SemaphoreType.DMA],
        ),
        out_shape=jax.ShapeDtypeStruct((n_tiles * tme, D), BF16),
        compiler_params=_params(("arbitrary",)),
    )(src, n_used, hp)


def _expert_up_kernel(te_ref, nused_ref, x_ref, wg_ref, wu_ref, a_ref):
    @pl.when(pl.program_id(0) < nused_ref[0])
    def _():
        x = x_ref[...]
        gate = jnp.dot(x, wg_ref[0], preferred_element_type=F32)
        up = jnp.dot(x, wu_ref[0], preferred_element_type=F32)
        a_ref[...] = (jax.nn.silu(gate) * up).astype(a_ref.dtype)

    @pl.when(pl.program_id(0) >= nused_ref[0])
    def _():
        a_ref[...] = jnp.zeros_like(a_ref)


def _expert_down_kernel(te_ref, nused_ref, a_ref, wd_ref, y_ref, *, n_split):
    nc = y_ref.shape[1]
    half_d = nc * LANES
    wc = half_d // n_split
    cpc = wc // LANES

    @pl.when(pl.program_id(0) < nused_ref[0])
    def _():
        a = a_ref[...]
        for k in range(n_split):
            lo = jnp.dot(a, wd_ref[0, :, k * wc:(k + 1) * wc], preferred_element_type=F32)
            hi = jnp.dot(a, wd_ref[0, :, half_d + k * wc:half_d + (k + 1) * wc], preferred_element_type=F32)
            packed = _pack_pairs(lo, hi)
            for cc in range(cpc):
                y_ref[:, k * cpc + cc, :] = packed[:, cc * LANES:(cc + 1) * LANES]

    @pl.when(pl.program_id(0) >= nused_ref[0])
    def _():
        y_ref[...] = jnp.zeros_like(y_ref)


def expert_ffn(xs, w_gate, w_up, w_down, tile_expert, n_used, tme):
    P, D = xs.shape
    F = w_gate.shape[2]
    n_tiles = P // tme
    nc = D // (2 * LANES)
    act = pl.pallas_call(
        _expert_up_kernel,
        grid_spec=pltpu.PrefetchScalarGridSpec(
            num_scalar_prefetch=2,
            grid=(n_tiles,),
            in_specs=[pl.BlockSpec((tme, D), lambda i, te, n: (i, 0)),
                      pl.BlockSpec((1, D, F), lambda i, te, n: (te[i], 0, 0)),
                      pl.BlockSpec((1, D, F), lambda i, te, n: (te[i], 0, 0))],
            out_specs=pl.BlockSpec((tme, F), lambda i, te, n: (i, 0)),
        ),
        out_shape=jax.ShapeDtypeStruct((P, F), BF16),
        compiler_params=_params(("arbitrary",)),
    )(tile_expert, n_used, xs, w_gate, w_up)
    n_split = max(1, (nc * LANES) // 512)
    return pl.pallas_call(
        functools.partial(_expert_down_kernel, n_split=n_split),
        grid_spec=pltpu.PrefetchScalarGridSpec(
            num_scalar_prefetch=2,
            grid=(n_tiles,),
            in_specs=[pl.BlockSpec((tme, F), lambda i, te, n: (i, 0)),
                      pl.BlockSpec((1, F, D), lambda i, te, n: (te[i], 0, 0))],
            out_specs=pl.BlockSpec((tme, nc, LANES), lambda i, te, n: (i, 0, 0)),
        ),
        out_shape=jax.ShapeDtypeStruct((P, nc, LANES), U32),
        compiler_params=_params(("arbitrary",)),
    )(tile_expert, n_used, act, w_down)


def _final_kernel(p0_ref, p1_ref, x_ref, cw_ref, y_hbm, g_ref, o_ref, b0, b1, sem, *, tm, nc):
    i = pl.program_id(0)

    def copies(r):
        t = i * tm + r
        dst = pl.ds(r * nc, nc)
        return (pltpu.make_async_copy(y_hbm.at[p0_ref[t]], b0.at[dst, :], sem.at[0]),
                pltpu.make_async_copy(y_hbm.at[p1_ref[t]], b1.at[dst, :], sem.at[1]))

    def issue(r, c):
        c0, c1 = copies(r)
        c0.start()
        c1.start()
        return c

    def drain(r, c):
        c0, c1 = copies(r)
        c0.wait()
        c1.wait()
        return c

    lax.fori_loop(0, tm, issue, 0, unroll=8)
    lax.fori_loop(0, tm, drain, 0, unroll=8)
    cw0 = cw_ref[:, 0:1]
    cw1 = cw_ref[:, 1:2]
    ss = jnp.zeros((tm, 1), F32)
    for c in range(nc):
        lo0, hi0 = _unpack_pairs(b0[pl.ds(c, tm, stride=nc), :])
        lo1, hi1 = _unpack_pairs(b1[pl.ds(c, tm, stride=nc), :])
        cl = slice(c * LANES, (c + 1) * LANES)
        ch = slice((nc + c) * LANES, (nc + c + 1) * LANES)
        xl = x_ref[:, cl] + (cw0 * lo0 + cw1 * lo1)
        xh = x_ref[:, ch] + (cw0 * hi0 + cw1 * hi1)
        o_ref[:, cl] = xl
        o_ref[:, ch] = xh
        ss = ss + jnp.sum(xl * xl, axis=-1, keepdims=True) + jnp.sum(xh * xh, axis=-1, keepdims=True)
    inv = lax.rsqrt(ss / (2 * nc * LANES) + RMS_EPS)
    o_ref[...] = o_ref[...] * inv * g_ref[...]


def final_combine(x1, cw, y, pos0, pos1, g):
    T, D = x1.shape
    nc = y.shape[1]
    tm = _tile(T, 256, 8)
    return pl.pallas_call(
        functools.partial(_final_kernel, tm=tm, nc=nc),
        grid_spec=pltpu.PrefetchScalarGridSpec(
            num_scalar_prefetch=2,
            grid=(T // tm,),
            in_specs=[pl.BlockSpec((tm, D), lambda i, a, b: (i, 0)),
                      pl.BlockSpec((tm, LANES), lambda i, a, b: (i, 0)),
                      pl.BlockSpec(memory_space=pl.ANY),
                      pl.BlockSpec((1, D), lambda i, a, b: (0, 0))],
            out_specs=pl.BlockSpec((tm, D), lambda i, a, b: (i, 0)),
            scratch_shapes=[pltpu.VMEM((tm * nc, LANES), U32), pltpu.VMEM((tm * nc, LANES), U32),
                            pltpu.SemaphoreType.DMA((2,))],
        ),
        out_shape=jax.ShapeDtypeStruct((T, D), F32),
        compiler_params=_params(("arbitrary",)),
    )(pos0, pos1, x1, cw, y, g.reshape(1, D).astype(F32))


def _sorted_layout(eid, tme):
    T = eid.shape[0]
    n_tiles = (2 * T) // tme + N_EXPERTS
    P = n_tiles * tme
    flat_e = eid.reshape(-1)
    onehot = (flat_e[:, None] == jnp.arange(N_EXPERTS, dtype=jnp.int32)[None, :]).astype(jnp.int32)
    csum = jnp.cumsum(onehot, axis=0)
    rank = jnp.take_along_axis(csum, flat_e[:, None], axis=1)[:, 0] - 1
    counts = csum[-1]
    padded = ((counts + tme - 1) // tme) * tme
    ends = jnp.cumsum(padded)
    pos = (ends - padded)[flat_e] + rank
    src = jnp.zeros((P,), jnp.int32).at[pos].set(jnp.arange(2 * T, dtype=jnp.int32) // 2)
    n_used = (ends[-1] // tme).astype(jnp.int32)
    tile_start = jnp.minimum(jnp.arange(n_tiles, dtype=jnp.int32), n_used - 1) * tme
    tile_expert = jnp.minimum(jnp.searchsorted(ends, tile_start, side="right"), N_EXPERTS - 1).astype(jnp.int32)
    pos2 = pos.reshape(T, 2)
    return src, tile_expert, n_used.reshape(1), pos2[:, 0], pos2[:, 1], n_tiles


def _encoder_trunk(x, wts):
    B, S, D = x.shape
    T = B * S
    hpg, cg = wts["hpg"], wts["cg"]
    n_grp = len(ATTN_PATTERNS)
    W = hpg * HEAD_DIM
    f_start = 3 * n_grp * W
    f_width = N_FOURIER_GROUPS * cg
    x2d = x.reshape(T, D)

    dils = sorted({d for _, d in ATTN_PATTERNS if d > 1})
    h, h_cm = rmsnorm_cast(x2d, wts["attn_norm_g"], B, S, dils)
    h_cm[1] = h

    outs, lses = [], []
    for g, (_, dil) in enumerate(ATTN_PATTERNS):
        qkv = in_projection(h_cm[dil], wts["w_in"], wts["b_full"], W, 3, lambda j, g=g: j * n_grp + g)
        o_g, lse_g = band_attention(qkv, B, S, g, hpg, n_grp * hpg)
        outs.append(o_g)
        lses.append(lse_g)

    tn = _tile(math.gcd(math.gcd(f_start, f_width), 2 * D), 1024)
    c0 = f_start // tn
    fg = in_projection(h, wts["w_in"], wts["b_full"], tn, (f_width + 2 * D) // tn, lambda j: c0 + j,
                       gate_tile0=f_width // tn)

    pc, ps = fourier_channel_stage(fg, cg, wts["chan_table"])
    cos_s, sin_s = _dft_tables(S)
    fr = fourier_sequence_stage(pc, ps, cos_s.astype(BF16), (-sin_s).astype(BF16), B, S,
                                float(1.0 / math.sqrt(S * cg)))

    merged = merge_branches(outs, lses, fr, wts["w_branch_attn"], wts["w_branch_fourier"], fg, f_width, hpg)
    x1 = out_projection(merged, wts["w_out"], x2d)

    eid, cw, hp = route(x1, wts["ffn_norm_g"], wts["w_router"], wts["b_router"])
    tme = _tile(2 * T, EXPERT_ROW_TILE, 8)
    src, tile_expert, n_used, pos0, pos1, n_tiles = _sorted_layout(eid[:, :2], tme)
    xs = gather_rows(hp, src, n_used, n_tiles, tme)
    y = expert_ffn(xs, wts["w_expert_gate"], wts["w_expert_up"], wts["w_expert_down"], tile_expert, n_used, tme)
    out = final_combine(x1, cw, y, pos0, pos1, wts["final_norm_g"])
    return out.reshape(B, S, D)


def kernel(x_prompt, x_sample, attn_norm_g, w_in, w_branch_attn, w_branch_fourier, b_gate, w_out, ffn_norm_g, w_router_group, b_router_group, w_router_expert, b_router_expert, w_expert_gate, w_expert_up, w_expert_down, final_norm_g):
    assert w_in.shape[0] == 1, "the final norm is fused into the layer's last kernel: one layer only"
    l = 0
    D = x_prompt.shape[-1]
    hpg = w_branch_attn.shape[1] // HEAD_DIM
    cg = w_branch_fourier.shape[1] // N_FOURIER_GROUPS
    in_width = w_in.shape[2]
    n_gate = b_gate.shape[1]
    cos_c, sin_c = _dft_tables(cg)
    w_r = jnp.concatenate(
        [w_router_group[l], jnp.transpose(w_router_expert[l], (1, 0, 2)).reshape(D, N_EXPERTS)], axis=1)
    b_r = jnp.concatenate([b_router_group[l], b_router_expert[l].reshape(-1)])
    pad = LANES - w_r.shape[1]
    wts = dict(
        hpg=hpg, cg=cg, chan_table=jnp.concatenate([cos_c, sin_c], axis=1).astype(BF16),
        attn_norm_g=attn_norm_g[l], ffn_norm_g=ffn_norm_g[l], final_norm_g=final_norm_g,
        w_in=w_in[l].astype(BF16),
        b_full=jnp.concatenate([jnp.zeros((in_width - n_gate,), F32), b_gate[l].astype(F32)]).reshape(1, in_width),
        w_branch_attn=w_branch_attn[l].astype(BF16),
        w_branch_fourier=w_branch_fourier[l].astype(BF16),
        w_out=w_out[l].astype(BF16),
        w_router=jnp.pad(w_r, ((0, 0), (0, pad))).astype(BF16),
        b_router=jnp.pad(b_r, (0, pad)).reshape(1, LANES).astype(F32),
        w_expert_gate=w_expert_gate[l].astype(BF16),
        w_expert_up=w_expert_up[l].astype(BF16),
        w_expert_down=w_expert_down[l].astype(BF16),
    )
    return tuple(_encoder_trunk(x, wts) for x in (x_prompt, x_sample))
```

```python
import functools
import math

import numpy as np
import jax
import jax.numpy as jnp
from jax import lax
from jax.experimental import pallas as pl
from jax.experimental.pallas import tpu as pltpu

F32 = jnp.float32
BF16 = jnp.bfloat16
U32 = jnp.uint32

RMS_EPS = 1e-6
NEG_INF = -1e30
HEAD_DIM = 128
ATTN_PATTERNS = ((128, 1), (512, 4), (2048, 16))
N_FOURIER_GROUPS = 4
N_EXPERT_GROUPS = 4
EXPERTS_PER_GROUP = 4
N_EXPERTS = N_EXPERT_GROUPS * EXPERTS_PER_GROUP
LANES = 128
V7X_VMEM_LIMIT = 56 * 1024 * 1024
ATTN_Q_TILE = 128
ATTN_ROW_TILE = 1024
EXPERT_ROW_TILE = 512
HI_MASK = 0xFFFF0000


def _tile(n, pref, mult=LANES):
    if n <= pref:
        return n
    t = (pref // mult) * mult
    while t >= mult:
        if n % t == 0:
            return t
        t -= mult
    raise ValueError(f"no tile for {n} <= {pref}")


def _params(sem, vmem=V7X_VMEM_LIMIT):
    return pltpu.CompilerParams(dimension_semantics=sem, vmem_limit_bytes=vmem)


def _pack_pairs(lo, hi):
    a = pltpu.bitcast(lo.astype(BF16).astype(F32), U32) >> 16
    b = pltpu.bitcast(hi.astype(BF16).astype(F32), U32) & jnp.uint32(HI_MASK)
    return a | b


def _unpack_pairs(w):
    return pltpu.bitcast(w << 16, F32), pltpu.bitcast(w & jnp.uint32(HI_MASK), F32)


def _rms_kernel(x_ref, g_ref, *refs, dils):
    o_ref = refs[0]
    cm_refs = refs[1:1 + len(dils)]
    scr = refs[-1] if dils else None
    x = x_ref[...]
    ms = jnp.mean(x * x, axis=-1, keepdims=True)
    h = x * lax.rsqrt(ms + RMS_EPS) * g_ref[...]
    o_ref[...] = h.astype(o_ref.dtype)
    if not dils:
        return
    tm = x.shape[0]
    nch = x.shape[1] // LANES
    for c in range(nch):
        scr[c] = h[:, c * LANES:(c + 1) * LANES]
    for cm_ref, d in zip(cm_refs, dils):
        for r in range(d):
            for c in range(nch):
                cm_ref[0, r, :, c * LANES:(c + 1) * LANES] = scr[c, pl.ds(r, tm // d, stride=d), :].astype(cm_ref.dtype)


def rmsnorm_cast(x, g, B, S, dils):
    T, D = x.shape
    tm = _tile(S, 256, 16 * max(dils, default=1))
    spb = S // tm
    outs = pl.pallas_call(
        functools.partial(_rms_kernel, dils=tuple(dils)),
        grid=(T // tm,),
        in_specs=[pl.BlockSpec((tm, D), lambda i: (i, 0)), pl.BlockSpec((1, D), lambda i: (0, 0))],
        out_specs=[pl.BlockSpec((tm, D), lambda i: (i, 0))]
        + [pl.BlockSpec((1, d, tm // d, D), lambda i: (i // spb, 0, i % spb, 0)) for d in dils],
        out_shape=[jax.ShapeDtypeStruct((T, D), BF16)]
        + [jax.ShapeDtypeStruct((B, d, S // d, D), BF16) for d in dils],
        scratch_shapes=[pltpu.VMEM((D // LANES, tm, LANES), F32)] if dils else [],
        compiler_params=_params(("parallel",)),
    )(x, g.reshape(1, D).astype(F32))
    return outs[0], {d: o.reshape(T, D) for d, o in zip(dils, outs[1:])}


def _in_proj_kernel(h_ref, w_ref, b_ref, o_ref, *, gate_tile0):
    acc = jnp.dot(h_ref[...], w_ref[...], preferred_element_type=F32)
    if gate_tile0 is None:
        o_ref[...] = acc.astype(o_ref.dtype)
        return
    j = pl.program_id(1)

    @pl.when(j < gate_tile0)
    def _():
        o_ref[...] = acc.astype(o_ref.dtype)

    @pl.when(j >= gate_tile0)
    def _():
        o_ref[...] = jax.nn.sigmoid(acc + b_ref[...]).astype(o_ref.dtype)


def in_projection(h, w, b_full, tn, n_out, w_col, gate_tile0=None):
    T, K = h.shape
    tm = _tile(T, 1024)
    return pl.pallas_call(
        functools.partial(_in_proj_kernel, gate_tile0=gate_tile0),
        grid=(T // tm, n_out),
        in_specs=[pl.BlockSpec((tm, K), lambda i, j: (i, 0)),
                  pl.BlockSpec((K, tn), lambda i, j: (0, w_col(j))),
                  pl.BlockSpec((1, tn), lambda i, j: (0, w_col(j)))],
        out_specs=pl.BlockSpec((tm, tn), lambda i, j: (i, j)),
        out_shape=jax.ShapeDtypeStruct((T, n_out * tn), BF16),
        compiler_params=_params(("parallel", "arbitrary")),
    )(h, w, b_full)


def _attn_kernel(q_ref, kp_ref, kc_ref, kn_ref, vp_ref, vc_ref, vn_ref, o_ref, lse_ref, so, sl,
                 *, tq, sub, half, hps, dil, cls_len, slopes):
    i = pl.program_id(1)
    hh = pl.program_id(2)
    key_pad = -(sub + 2 * half) % LANES
    span = sub + 2 * half + key_pad
    scale = HEAD_DIM ** -0.5
    qidx = lax.broadcasted_iota(jnp.int32, (sub, span), 0)
    kidx = lax.broadcasted_iota(jnp.int32, (sub, span), 1)
    absrel = jnp.abs(kidx - half - qidx)
    absrel_f = absrel.astype(F32)
    lane = lax.broadcasted_iota(jnp.int32, (sub, LANES), 1)
    n_sub = tq // sub
    valids = []
    for u in range(n_sub):
        kpos = i * tq + (u * sub - half) + kidx
        valids.append((absrel <= half) & (kpos >= 0) & (kpos < cls_len))
    neg_slopes = []
    for h in range(hps):
        s_h = jnp.float32(0.0)
        for b in range(len(slopes) // hps):
            s_h = jnp.where(hh == b, jnp.float32(-slopes[b * hps + h] * dil), s_h)
        neg_slopes.append(s_h)

    @pl.when(hh == 0)
    def _():
        sl[...] = jnp.zeros_like(sl)

    def keys(prev_ref, cur_ref, next_ref, r, u, cols):
        lo = u * sub - half if u > 0 else 0
        hi = (u + 1) * sub + half if u < n_sub - 1 else tq
        parts = [cur_ref[0, r, lo:hi, cols]]
        if u == 0:
            parts.insert(0, prev_ref[0, r, :, cols])
        if u == n_sub - 1:
            parts.append(next_ref[0, r, :, cols])
        if key_pad:
            parts.append(jnp.zeros((key_pad, HEAD_DIM), cur_ref.dtype))
        return parts[0] if len(parts) == 1 else jnp.concatenate(parts, axis=0)

    for r in range(dil):
        for u in range(n_sub):
            rows = pl.ds(u * sub * dil + r, sub, stride=dil) if dil > 1 else pl.ds(u * sub, sub)
            lse_tile = sl[rows, :]
            for h in range(hps):
                cols = slice(h * HEAD_DIM, (h + 1) * HEAD_DIM)
                q = q_ref[0, r, u * sub:(u + 1) * sub, cols]
                k = keys(kp_ref, kc_ref, kn_ref, r, u, cols)
                v = keys(vp_ref, vc_ref, vn_ref, r, u, cols)
                s = lax.dot_general(q, k, (((1,), (1,)), ((), ())), preferred_element_type=F32)
                s = s * scale + absrel_f * neg_slopes[h]
                s = jnp.where(valids[u], s, NEG_INF)
                m = jnp.max(s, axis=-1, keepdims=True)
                p = jnp.exp(s - m)
                den = jnp.sum(p, axis=-1, keepdims=True)
                so[h, rows, :] = jnp.dot(p.astype(v.dtype), v, preferred_element_type=F32) / den
                lse_tile = jnp.where(lane == hh * hps + h, m + jnp.log(den), lse_tile)
            sl[rows, :] = lse_tile
    for h in range(hps):
        o_ref[0, :, h * HEAD_DIM:(h + 1) * HEAD_DIM] = so[h].astype(o_ref.dtype)
    lse_ref[0] = sl[...]


def band_attention(qkv, B, S, group, hpg, n_heads_total):
    window, dil = ATTN_PATTERNS[group]
    half = window // (2 * dil)
    W = hpg * HEAD_DIM
    L = S // dil
    tq = min(L, max(ATTN_Q_TILE, ATTN_ROW_TILE // dil))
    rows = tq * dil
    sub = min(ATTN_Q_TILE, tq)
    hps = max(1, hpg * ATTN_ROW_TILE // max(rows, ATTN_ROW_TILE))
    assert S % dil == 0 and L % tq == 0 and tq % sub == 0 and tq % half == 0 and half % 16 == 0 and hpg % hps == 0
    n_hh = hpg // hps
    Wb = hps * HEAD_DIM
    hb = tq // half
    n_hb = L // half
    s_all = 2.0 ** (-8.0 * np.arange(1, n_heads_total + 1) / n_heads_total)
    slopes = tuple(float(np.float32(v)) for v in s_all[group * hpg:(group + 1) * hpg])
    qv = qkv.reshape(B, dil, L, 3 * W)

    cur = lambda part: pl.BlockSpec((1, dil, tq, Wb), lambda b, i, hh: (b, 0, i, part * n_hh + hh))
    prv = lambda part: pl.BlockSpec(
        (1, dil, half, Wb), lambda b, i, hh: (b, 0, jnp.maximum(i * hb - 1, 0), part * n_hh + hh))
    nxt = lambda part: pl.BlockSpec(
        (1, dil, half, Wb), lambda b, i, hh: (b, 0, jnp.minimum((i + 1) * hb, n_hb - 1), part * n_hh + hh))
    o, lse = pl.pallas_call(
        functools.partial(_attn_kernel, tq=tq, sub=sub, half=half, hps=hps, dil=dil, cls_len=L, slopes=slopes),
        grid=(B, L // tq, n_hh),
        in_specs=[cur(0), prv(1), cur(1), nxt(1), prv(2), cur(2), nxt(2)],
        out_specs=[pl.BlockSpec((1, rows, Wb), lambda b, i, hh: (b, i, hh)),
                   pl.BlockSpec((1, rows, LANES), lambda b, i, hh: (b, i, 0))],
        out_shape=[jax.ShapeDtypeStruct((B, S, W), BF16), jax.ShapeDtypeStruct((B, S, LANES), F32)],
        scratch_shapes=[pltpu.VMEM((hps, rows, LANES), F32), pltpu.VMEM((rows, LANES), F32)],
        compiler_params=_params(("parallel", "arbitrary", "arbitrary")),
    )(qv, qv, qv, qv, qv, qv, qv)
    return o.reshape(B * S, W), lse.reshape(B * S, LANES)


def _dft_tables(n):
    k = jnp.arange(n, dtype=jnp.int32)

    def thin(cols):
        ang = ((k[:, None] * cols[None, :]) % n).astype(F32) * np.float32(2.0 * np.pi / n)
        return jnp.cos(ang), jnp.sin(ang)

    m = 64
    if n <= m or n % m:
        return thin(k)
    c_hi, s_hi = thin(jnp.arange(n // m, dtype=jnp.int32) * m)
    c_lo, s_lo = thin(jnp.arange(m, dtype=jnp.int32))
    cos = c_hi[:, :, None] * c_lo[:, None, :] - s_hi[:, :, None] * s_lo[:, None, :]
    sin = s_hi[:, :, None] * c_lo[:, None, :] + c_hi[:, :, None] * s_lo[:, None, :]
    return cos.reshape(n, n), sin.reshape(n, n)


def _fourier_chan_kernel(x_ref, t_ref, pc_ref, ps_ref):
    r = jnp.dot(x_ref[...], t_ref[...], preferred_element_type=F32)
    c = pc_ref.shape[-1]
    pc_ref[...] = r[:, :c].astype(pc_ref.dtype)
    ps_ref[...] = r[:, c:].astype(ps_ref.dtype)


def fourier_channel_stage(fg, cg, table):
    T = fg.shape[0]
    tm = _tile(T, 1024)
    return pl.pallas_call(
        _fourier_chan_kernel,
        grid=(T // tm, N_FOURIER_GROUPS),
        in_specs=[pl.BlockSpec((tm, cg), lambda i, g: (i, g)),
                  pl.BlockSpec((cg, 2 * cg), lambda i, g: (0, 0))],
        out_specs=[pl.BlockSpec((tm, cg), lambda i, g: (i, g)), pl.BlockSpec((tm, cg), lambda i, g: (i, g))],
        out_shape=[jax.ShapeDtypeStruct((T, N_FOURIER_GROUPS * cg), BF16)] * 2,
        compiler_params=_params(("parallel", "arbitrary")),
    )(fg, table)


def _fourier_seq_kernel(cs_ref, ss_ref, pc_ref, ps_ref, o_ref, acc_ref, *, scale):
    k = pl.program_id(3)

    @pl.when(k == 0)
    def _():
        acc_ref[...] = jnp.zeros_like(acc_ref)

    acc_ref[...] += (jnp.dot(cs_ref[...], pc_ref[0], preferred_element_type=F32)
                     + jnp.dot(ss_ref[...], ps_ref[0], preferred_element_type=F32))

    @pl.when(k == pl.num_programs(3) - 1)
    def _():
        o_ref[0] = (acc_ref[...] * scale).astype(o_ref.dtype)


def fourier_sequence_stage(pc, ps, cos_s, neg_sin_s, B, S, scale):
    N = pc.shape[1]
    tm, tn, tk = _tile(S, 1024), _tile(N, 1024), _tile(S, 2048)
    pc3, ps3 = pc.reshape(B, S, N), ps.reshape(B, S, N)
    out = pl.pallas_call(
        functools.partial(_fourier_seq_kernel, scale=scale),
        grid=(B, S // tm, N // tn, S // tk),
        in_specs=[pl.BlockSpec((tm, tk), lambda b, i, j, k: (i, k)),
                  pl.BlockSpec((tm, tk), lambda b, i, j, k: (i, k)),
                  pl.BlockSpec((1, tk, tn), lambda b, i, j, k: (b, k, j)),
                  pl.BlockSpec((1, tk, tn), lambda b, i, j, k: (b, k, j))],
        out_specs=pl.BlockSpec((1, tm, tn), lambda b, i, j, k: (b, i, j)),
        out_shape=jax.ShapeDtypeStruct((B, S, N), BF16),
        scratch_shapes=[pltpu.VMEM((tm, tn), F32)],
        compiler_params=_params(("parallel", "parallel", "parallel", "arbitrary")),
    )(cos_s, neg_sin_s, pc3, ps3)
    return out.reshape(B * S, N)


def _merge_kernel(o0_ref, o1_ref, o2_ref, l0_ref, l1_ref, l2_ref, fr_ref, wa_ref, wf_ref, ga_ref, gf_ref,
                  m_ref, oc_ref, *, hpg):
    @pl.when(pl.program_id(1) == 0)
    def _():
        l0, l1, l2 = l0_ref[...], l1_ref[...], l2_ref[...]
        m = jnp.maximum(jnp.maximum(l0, l1), l2)
        e0, e1, e2 = jnp.exp(l0 - m), jnp.exp(l1 - m), jnp.exp(l2 - m)
        den = e0 + e1 + e2
        w0, w1, w2 = e0 / den, e1 / den, e2 / den
        for h in range(hpg):
            cols = slice(h * HEAD_DIM, (h + 1) * HEAD_DIM)
            oc = (w0[:, h:h + 1] * o0_ref[:, cols].astype(F32)
                  + w1[:, h:h + 1] * o1_ref[:, cols].astype(F32)
                  + w2[:, h:h + 1] * o2_ref[:, cols].astype(F32))
            oc_ref[:, cols] = oc.astype(oc_ref.dtype)

    attn = jnp.dot(oc_ref[...], wa_ref[...], preferred_element_type=F32)
    four = jnp.dot(fr_ref[...], wf_ref[...], preferred_element_type=F32)
    m_ref[...] = (ga_ref[...].astype(F32) * attn + gf_ref[...].astype(F32) * four).astype(m_ref.dtype)


def merge_branches(os_, lses, fr, wa, wf, fg, gate_start, hpg):
    T, W = os_[0].shape
    D = wa.shape[1]
    FW = fr.shape[1]
    tm, tn = _tile(T, 512), _tile(math.gcd(D, gate_start), 1024)
    g0 = gate_start // tn
    row = lambda w: pl.BlockSpec((tm, w), lambda i, j: (i, 0))
    return pl.pallas_call(
        functools.partial(_merge_kernel, hpg=hpg),
        grid=(T // tm, D // tn),
        in_specs=[row(W), row(W), row(W), row(LANES), row(LANES), row(LANES), row(FW),
                  pl.BlockSpec((W, tn), lambda i, j: (0, j)),
                  pl.BlockSpec((FW, tn), lambda i, j: (0, j)),
                  pl.BlockSpec((tm, tn), lambda i, j: (i, g0 + j)),
                  pl.BlockSpec((tm, tn), lambda i, j: (i, g0 + D // tn + j))],
        out_specs=pl.BlockSpec((tm, tn), lambda i, j: (i, j)),
        out_shape=jax.ShapeDtypeStruct((T, D), BF16),
        scratch_shapes=[pltpu.VMEM((tm, W), BF16)],
        compiler_params=_params(("parallel", "arbitrary")),
    )(*os_, *lses, fr, wa, wf, fg, fg)


def _out_proj_kernel(m_ref, w_ref, x_ref, o_ref):
    o_ref[...] = x_ref[...] + jnp.dot(m_ref[...], w_ref[...], preferred_element_type=F32)


def out_projection(merged, w, x):
    T, K = merged.shape
    N = w.shape[1]
    tm, tn = _tile(T, 1024), _tile(N, 512)
    return pl.pallas_call(
        _out_proj_kernel,
        grid=(T // tm, N // tn),
        in_specs=[pl.BlockSpec((tm, K), lambda i, j: (i, 0)),
                  pl.BlockSpec((K, tn), lambda i, j: (0, j)),
                  pl.BlockSpec((tm, tn), lambda i, j: (i, j))],
        out_specs=pl.BlockSpec((tm, tn), lambda i, j: (i, j)),
        out_shape=jax.ShapeDtypeStruct((T, N), F32),
        compiler_params=_params(("parallel", "arbitrary")),
    )(merged, w, x)


def _router_kernel(x_ref, g_ref, w_ref, b_ref, eid_ref, cw_ref, hp_ref):
    x = x_ref[...]
    ms = jnp.mean(x * x, axis=-1, keepdims=True)
    hf = x * lax.rsqrt(ms + RMS_EPS) * g_ref[...]
    h = hf.astype(BF16)
    half_d = hf.shape[1] // 2
    packed = _pack_pairs(hf[:, :half_d], hf[:, half_d:])
    for c in range(hp_ref.shape[1]):
        hp_ref[:, c, :] = packed[:, c * LANES:(c + 1) * LANES]
    lg = jnp.dot(h, w_ref[...], preferred_element_type=F32) + b_ref[...]
    lane = lax.broadcasted_iota(jnp.int32, lg.shape, 1)
    big = jnp.int32(LANES)
    in_grp = lane < N_EXPERT_GROUPS
    gl = jnp.where(in_grp, lg, -jnp.inf)
    gmax = jnp.max(gl, axis=-1, keepdims=True)
    gsel = jnp.min(jnp.where(gl == gmax, lane, big), axis=-1, keepdims=True)
    p_group = 1.0 / jnp.sum(jnp.where(in_grp, jnp.exp(gl - gmax), 0.0), axis=-1, keepdims=True)
    lo = N_EXPERT_GROUPS + gsel * EXPERTS_PER_GROUP
    in_sel = (lane >= lo) & (lane < lo + EXPERTS_PER_GROUP)
    el = jnp.where(in_sel, lg, -jnp.inf)
    t1 = jnp.max(el, axis=-1, keepdims=True)
    i1 = jnp.min(jnp.where(el == t1, lane, big), axis=-1, keepdims=True)
    el2 = jnp.where(lane == i1, -jnp.inf, el)
    t2 = jnp.max(el2, axis=-1, keepdims=True)
    i2 = jnp.min(jnp.where(el2 == t2, lane, big), axis=-1, keepdims=True)
    e21 = jnp.exp(t2 - t1)
    p1 = 1.0 / (1.0 + e21)
    p2 = e21 / (1.0 + e21)
    eid_ref[...] = jnp.where(lane == 0, i1 - N_EXPERT_GROUPS, jnp.where(lane == 1, i2 - N_EXPERT_GROUPS, 0))
    cw_ref[...] = jnp.where(lane == 0, p_group * p1, jnp.where(lane == 1, p_group * p2, 0.0))


def route(x1, g, w_router, b_router):
    T, D = x1.shape
    tm = _tile(T, 256, 8)
    nc = D // (2 * LANES)
    return pl.pallas_call(
        _router_kernel,
        grid=(T // tm,),
        in_specs=[pl.BlockSpec((tm, D), lambda i: (i, 0)),
                  pl.BlockSpec((1, D), lambda i: (0, 0)),
                  pl.BlockSpec((D, LANES), lambda i: (0, 0)),
                  pl.BlockSpec((1, LANES), lambda i: (0, 0))],
        out_specs=[pl.BlockSpec((tm, LANES), lambda i: (i, 0)), pl.BlockSpec((tm, LANES), lambda i: (i, 0)),
                   pl.BlockSpec((tm, nc, LANES), lambda i: (i, 0, 0))],
        out_shape=[jax.ShapeDtypeStruct((T, LANES), jnp.int32), jax.ShapeDtypeStruct((T, LANES), F32),
                   jax.ShapeDtypeStruct((T, nc, LANES), U32)],
        compiler_params=_params(("parallel",)),
    )(x1, g.reshape(1, D).astype(F32), w_router, b_router)


def _expert_up_kernel(src_ref, te_ref, nused_ref, hp_hbm, wg_ref, wu_ref, a_ref, buf, xs, sem, *, tme, nc):
    i = pl.program_id(0)
    n_used = nused_ref[0]

    def row_copy(tile, slot, r):
        return pltpu.make_async_copy(hp_hbm.at[src_ref[tile * tme + r]], buf.at[slot, pl.ds(r * nc, nc), :],
                                     sem.at[slot])

    def issue(tile, slot):
        def body(r, c):
            row_copy(tile, slot, r).start()
            return c
        lax.fori_loop(0, tme, body, 0, unroll=8)

    @pl.when(i == 0)
    def _():
        issue(0, 0)

    @pl.when(i + 1 < n_used)
    def _():
        issue(i + 1, (i + 1) % 2)

    @pl.when(i < n_used)
    def _():
        slot = i % 2

        def drain(r, c):
            row_copy(i, slot, r).wait()
            return c
        lax.fori_loop(0, tme, drain, 0, unroll=8)
        for c in range(nc):
            lo, hi = _unpack_pairs(buf[slot, pl.ds(c, tme, stride=nc), :])
            xs[:, c * LANES:(c + 1) * LANES] = lo.astype(xs.dtype)
            xs[:, (nc + c) * LANES:(nc + c + 1) * LANES] = hi.astype(xs.dtype)
        x = xs[...]
        gate = jnp.dot(x, wg_ref[0], preferred_element_type=F32)
        up = jnp.dot(x, wu_ref[0], preferred_element_type=F32)
        a_ref[...] = (jax.nn.silu(gate) * up).astype(a_ref.dtype)

    @pl.when(i >= n_used)
    def _():
        a_ref[...] = jnp.zeros_like(a_ref)


def _expert_down_kernel(dst_ref, te_ref, nused_ref, a_ref, wd_ref, yt_hbm, ybuf, zbuf, sem, zsem,
                        *, tme, nc, n_split, n_real):
    i = pl.program_id(0)
    n_used = nused_ref[0]
    last = pl.num_programs(0) - 1
    half_d = nc * LANES
    wc = half_d // n_split
    cpc = wc // LANES

    def row_copy(tile, slot, r):
        return pltpu.make_async_copy(ybuf.at[slot, pl.ds(r * nc, nc), :], yt_hbm.at[dst_ref[tile * tme + r]],
                                     sem.at[slot])

    def drain(tile):
        def body(r, c):
            row_copy(tile, tile % 2, r).wait()
            return c
        lax.fori_loop(0, tme, body, 0, unroll=8)

    @pl.when(i == 0)
    def _():
        zbuf[...] = jnp.zeros_like(zbuf)

        def zero_copy(r):
            return pltpu.make_async_copy(zbuf, yt_hbm.at[n_real + r], zsem)

        def start(r, c):
            zero_copy(r).start()
            return c

        def wait(r, c):
            zero_copy(r).wait()
            return c
        lax.fori_loop(0, tme, start, 0, unroll=8)
        lax.fori_loop(0, tme, wait, 0, unroll=8)

    @pl.when((i >= 2) & (i - 2 < n_used))
    def _():
        drain(i - 2)

    @pl.when(i < n_used)
    def _():
        slot = i % 2
        a = a_ref[...]
        for k in range(n_split):
            lo = jnp.dot(a, wd_ref[0, :, k * wc:(k + 1) * wc], preferred_element_type=F32)
            hi = jnp.dot(a, wd_ref[0, :, half_d + k * wc:half_d + (k + 1) * wc], preferred_element_type=F32)
            packed = _pack_pairs(lo, hi)
            for cc in range(cpc):
                ybuf[slot, pl.ds(k * cpc + cc, tme, stride=nc), :] = packed[:, cc * LANES:(cc + 1) * LANES]

        def body(r, c):
            row_copy(i, slot, r).start()
            return c
        lax.fori_loop(0, tme, body, 0, unroll=8)

    @pl.when((i == last) & (i >= 1) & (i - 1 < n_used))
    def _():
        drain(i - 1)

    @pl.when((i == last) & (i < n_used))
    def _():
        drain(i)


def expert_ffn(hp, w_gate, w_up, w_down, src, dst, tile_expert, n_used, n_tiles, tme, n_rows_out):
    _, nc, _ = hp.shape
    D = nc * 2 * LANES
    F = w_gate.shape[2]
    P = n_tiles * tme
    act = pl.pallas_call(
        functools.partial(_expert_up_kernel, tme=tme, nc=nc),
        grid_spec=pltpu.PrefetchScalarGridSpec(
            num_scalar_prefetch=3,
            grid=(n_tiles,),
            in_specs=[pl.BlockSpec(memory_space=pl.ANY),
                      pl.BlockSpec((1, D, F), lambda i, s, te, n: (te[i], 0, 0)),
                      pl.BlockSpec((1, D, F), lambda i, s, te, n: (te[i], 0, 0))],
            out_specs=pl.BlockSpec((tme, F), lambda i, s, te, n: (i, 0)),
            scratch_shapes=[pltpu.VMEM((2, tme * nc, LANES), U32), pltpu.VMEM((tme, D), BF16),
                            pltpu.SemaphoreType.DMA((2,))],
        ),
        out_shape=jax.ShapeDtypeStruct((P, F), BF16),
        compiler_params=_params(("arbitrary",)),
    )(src, tile_expert, n_used, hp, w_gate, w_up)
    n_split = max(1, (nc * LANES) // 512)
    return pl.pallas_call(
        functools.partial(_expert_down_kernel, tme=tme, nc=nc, n_split=n_split, n_real=n_rows_out - tme),
        grid_spec=pltpu.PrefetchScalarGridSpec(
            num_scalar_prefetch=3,
            grid=(n_tiles,),
            in_specs=[pl.BlockSpec((tme, F), lambda i, d, te, n: (i, 0)),
                      pl.BlockSpec((1, F, D), lambda i, d, te, n: (te[i], 0, 0))],
            out_specs=pl.BlockSpec(memory_space=pl.ANY),
            scratch_shapes=[pltpu.VMEM((2, tme * nc, LANES), U32), pltpu.VMEM((nc, LANES), U32),
                            pltpu.SemaphoreType.DMA((2,)), pltpu.SemaphoreType.DMA],
        ),
        out_shape=jax.ShapeDtypeStruct((n_rows_out, nc, LANES), U32),
        compiler_params=_params(("arbitrary",)),
    )(dst, tile_expert, n_used, act, w_down)


def _final_kernel(x_ref, cw_ref, y_ref, g_ref, o_ref, *, tm, nc):
    cw0 = cw_ref[:, 0:1]
    cw1 = cw_ref[:, 1:2]
    ss = jnp.zeros((tm, 1), F32)
    for c in range(nc):
        lo0, hi0 = _unpack_pairs(y_ref[pl.ds(c, tm, stride=2 * nc), :])
        lo1, hi1 = _unpack_pairs(y_ref[pl.ds(nc + c, tm, stride=2 * nc), :])
        cl = slice(c * LANES, (c + 1) * LANES)
        ch = slice((nc + c) * LANES, (nc + c + 1) * LANES)
        xl = x_ref[:, cl] + (cw0 * lo0 + cw1 * lo1)
        xh = x_ref[:, ch] + (cw0 * hi0 + cw1 * hi1)
        o_ref[:, cl] = xl
        o_ref[:, ch] = xh
        ss = ss + jnp.sum(xl * xl, axis=-1, keepdims=True) + jnp.sum(xh * xh, axis=-1, keepdims=True)
    inv = lax.rsqrt(ss / (2 * nc * LANES) + RMS_EPS)
    o_ref[...] = o_ref[...] * inv * g_ref[...]


def final_combine(x1, cw, yt, g):
    T, D = x1.shape
    nc = yt.shape[1]
    tm = _tile(T, 256, 8)
    y2d = yt.reshape(yt.shape[0] * nc, LANES)
    return pl.pallas_call(
        functools.partial(_final_kernel, tm=tm, nc=nc),
        grid=(T // tm,),
        in_specs=[pl.BlockSpec((tm, D), lambda i: (i, 0)),
                  pl.BlockSpec((tm, LANES), lambda i: (i, 0)),
                  pl.BlockSpec((2 * tm * nc, LANES), lambda i: (i, 0)),
                  pl.BlockSpec((1, D), lambda i: (0, 0))],
        out_specs=pl.BlockSpec((tm, D), lambda i: (i, 0)),
        out_shape=jax.ShapeDtypeStruct((T, D), F32),
        compiler_params=_params(("parallel",)),
    )(x1, cw, y2d, g.reshape(1, D).astype(F32))


def _sorted_layout(eid, tme):
    T = eid.shape[0]
    n_tiles = (2 * T) // tme + N_EXPERTS
    P = n_tiles * tme
    flat_e = eid.reshape(-1)
    onehot = (flat_e[:, None] == jnp.arange(N_EXPERTS, dtype=jnp.int32)[None, :]).astype(jnp.int32)
    csum = jnp.cumsum(onehot, axis=0)
    rank = jnp.take_along_axis(csum, flat_e[:, None], axis=1)[:, 0] - 1
    counts = csum[-1]
    padded = ((counts + tme - 1) // tme) * tme
    ends = jnp.cumsum(padded)
    pos = (ends - padded)[flat_e] + rank
    spare = 2 * T + jnp.arange(P, dtype=jnp.int32) % tme
    dst = spare.at[pos].set(jnp.arange(2 * T, dtype=jnp.int32))
    src = jnp.where(dst < 2 * T, dst // 2, 0)
    n_used = (ends[-1] // tme).astype(jnp.int32)
    tile_start = jnp.minimum(jnp.arange(n_tiles, dtype=jnp.int32), n_used - 1) * tme
    tile_expert = jnp.minimum(jnp.searchsorted(ends, tile_start, side="right"), N_EXPERTS - 1).astype(jnp.int32)
    return src, dst, tile_expert, n_used.reshape(1), n_tiles


def _encoder_trunk(x, wts):
    B, S, D = x.shape
    T = B * S
    hpg, cg = wts["hpg"], wts["cg"]
    n_grp = len(ATTN_PATTERNS)
    W = hpg * HEAD_DIM
    f_start = 3 * n_grp * W
    f_width = N_FOURIER_GROUPS * cg
    x2d = x.reshape(T, D)

    dils = sorted({d for _, d in ATTN_PATTERNS if d > 1})
    h, h_cm = rmsnorm_cast(x2d, wts["attn_norm_g"], B, S, dils)
    h_cm[1] = h

    outs, lses = [], []
    for g, (_, dil) in enumerate(ATTN_PATTERNS):
        qkv = in_projection(h_cm[dil], wts["w_in"], wts["b_full"], W, 3, lambda j, g=g: j * n_grp + g)
        o_g, lse_g = band_attention(qkv, B, S, g, hpg, n_grp * hpg)
        outs.append(o_g)
        lses.append(lse_g)

    tn = _tile(math.gcd(math.gcd(f_start, f_width), 2 * D), 1024)
    c0 = f_start // tn
    fg = in_projection(h, wts["w_in"], wts["b_full"], tn, (f_width + 2 * D) // tn, lambda j: c0 + j,
                       gate_tile0=f_width // tn)

    pc, ps = fourier_channel_stage(fg, cg, wts["chan_table"])
    cos_s, sin_s = _dft_tables(S)
    fr = fourier_sequence_stage(pc, ps, cos_s.astype(BF16), (-sin_s).astype(BF16), B, S,
                                float(1.0 / math.sqrt(S * cg)))

    merged = merge_branches(outs, lses, fr, wts["w_branch_attn"], wts["w_branch_fourier"], fg, f_width, hpg)
    x1 = out_projection(merged, wts["w_out"], x2d)

    eid, cw, hp = route(x1, wts["ffn_norm_g"], wts["w_router"], wts["b_router"])
    tme = _tile(2 * T, EXPERT_ROW_TILE, 8)
    src, dst, tile_expert, n_used, n_tiles = _sorted_layout(eid[:, :2], tme)
    yt = expert_ffn(hp, wts["w_expert_gate"], wts["w_expert_up"], wts["w_expert_down"], src, dst, tile_expert,
                    n_used, n_tiles, tme, 2 * T + tme)
    out = final_combine(x1, cw, yt, wts["final_norm_g"])
    return out.reshape(B, S, D)


def kernel(x_prompt, x_sample, attn_norm_g, w_in, w_branch_attn, w_branch_fourier, b_gate, w_out, ffn_norm_g, w_router_group, b_router_group, w_router_expert, b_router_expert, w_expert_gate, w_expert_up, w_expert_down, final_norm_g):
    assert w_in.shape[0] == 1, "the final norm is fused into the layer's last kernel: one layer only"
    l = 0
    D = x_prompt.shape[-1]
    hpg = w_branch_attn.shape[1] // HEAD_DIM
    cg = w_branch_fourier.shape[1] // N_FOURIER_GROUPS
    in_width = w_in.shape[2]
    n_gate = b_gate.shape[1]
    cos_c, sin_c = _dft_tables(cg)
    w_r = jnp.concatenate(
        [w_router_group[l], jnp.transpose(w_router_expert[l], (1, 0, 2)).reshape(D, N_EXPERTS)], axis=1)
    b_r = jnp.concatenate([b_router_group[l], b_router_expert[l].reshape(-1)])
    pad = LANES - w_r.shape[1]
    wts = dict(
        hpg=hpg, cg=cg, chan_table=jnp.concatenate([cos_c, sin_c], axis=1).astype(BF16),
        attn_norm_g=attn_norm_g[l], ffn_norm_g=ffn_norm_g[l], final_norm_g=final_norm_g,
        w_in=w_in[l].astype(BF16),
        b_full=jnp.concatenate([jnp.zeros((in_width - n_gate,), F32), b_gate[l].astype(F32)]).reshape(1, in_width),
        w_branch_attn=w_branch_attn[l].astype(BF16),
        w_branch_fourier=w_branch_fourier[l].astype(BF16),
        w_out=w_out[l].astype(BF16),
        w_router=jnp.pad(w_r, ((0, 0), (0, pad))).astype(BF16),
        b_router=jnp.pad(b_r, (0, pad)).reshape(1, LANES).astype(F32),
        w_expert_gate=w_expert_gate[l].astype(BF16),
        w_expert_up=w_expert_up[l].astype(BF16),
        w_expert_down=w_expert_down[l].astype(BF16),
    )
    return tuple(_encoder_trunk(x, wts) for x in (x_prompt, x_sample))
```

```python
import functools
import math

import numpy as np
import jax
import jax.numpy as jnp
from jax import lax
from jax.experimental import pallas as pl
from jax.experimental.pallas import tpu as pltpu

F32 = jnp.float32
BF16 = jnp.bfloat16
U32 = jnp.uint32

RMS_EPS = 1e-6
NEG_INF = -1e30
HEAD_DIM = 128
ATTN_PATTERNS = ((128, 1), (512, 4), (2048, 16))
N_FOURIER_GROUPS = 4
N_EXPERT_GROUPS = 4
EXPERTS_PER_GROUP = 4
N_EXPERTS = N_EXPERT_GROUPS * EXPERTS_PER_GROUP
LANES = 128
V7X_VMEM_LIMIT = 56 * 1024 * 1024
ATTN_Q_TILE = 128
ATTN_ROW_TILE = 1024
EXPERT_ROW_TILE = 512
HI_MASK = 0xFFFF0000


def _tile(n, pref, mult=LANES):
    if n <= pref:
        return n
    t = (pref // mult) * mult
    while t >= mult:
        if n % t == 0:
            return t
        t -= mult
    raise ValueError(f"no tile for {n} <= {pref}")


def _params(sem, vmem=V7X_VMEM_LIMIT):
    return pltpu.CompilerParams(dimension_semantics=sem, vmem_limit_bytes=vmem)


def _pack_pairs(lo, hi):
    a = pltpu.bitcast(lo.astype(BF16).astype(F32), U32) >> 16
    b = pltpu.bitcast(hi.astype(BF16).astype(F32), U32) & jnp.uint32(HI_MASK)
    return a | b


def _unpack_pairs(w):
    return pltpu.bitcast(w << 16, F32), pltpu.bitcast(w & jnp.uint32(HI_MASK), F32)


def _rms_kernel(x_ref, g_ref, *refs, dils):
    o_ref = refs[0]
    cm_refs = refs[1:1 + len(dils)]
    scr = refs[-1] if dils else None
    x = x_ref[...]
    ms = jnp.mean(x * x, axis=-1, keepdims=True)
    h = x * lax.rsqrt(ms + RMS_EPS) * g_ref[...]
    o_ref[...] = h.astype(o_ref.dtype)
    if not dils:
        return
    tm = x.shape[0]
    nch = x.shape[1] // LANES
    for c in range(nch):
        scr[c] = h[:, c * LANES:(c + 1) * LANES]
    for cm_ref, d in zip(cm_refs, dils):
        for r in range(d):
            for c in range(nch):
                cm_ref[0, r, :, c * LANES:(c + 1) * LANES] = scr[c, pl.ds(r, tm // d, stride=d), :].astype(cm_ref.dtype)


def rmsnorm_cast(x, g, B, S, dils):
    T, D = x.shape
    tm = _tile(S, 256, 16 * max(dils, default=1))
    spb = S // tm
    outs = pl.pallas_call(
        functools.partial(_rms_kernel, dils=tuple(dils)),
        grid=(T // tm,),
        in_specs=[pl.BlockSpec((tm, D), lambda i: (i, 0)), pl.BlockSpec((1, D), lambda i: (0, 0))],
        out_specs=[pl.BlockSpec((tm, D), lambda i: (i, 0))]
        + [pl.BlockSpec((1, d, tm // d, D), lambda i: (i // spb, 0, i % spb, 0)) for d in dils],
        out_shape=[jax.ShapeDtypeStruct((T, D), BF16)]
        + [jax.ShapeDtypeStruct((B, d, S // d, D), BF16) for d in dils],
        scratch_shapes=[pltpu.VMEM((D // LANES, tm, LANES), F32)] if dils else [],
        compiler_params=_params(("parallel",)),
    )(x, g.reshape(1, D).astype(F32))
    return outs[0], {d: o.reshape(T, D) for d, o in zip(dils, outs[1:])}


def _in_proj_kernel(h_ref, w_ref, b_ref, o_ref, *, gate_tile0):
    acc = jnp.dot(h_ref[...], w_ref[...], preferred_element_type=F32)
    if gate_tile0 is None:
        o_ref[...] = acc.astype(o_ref.dtype)
        return
    j = pl.program_id(1)

    @pl.when(j < gate_tile0)
    def _():
        o_ref[...] = acc.astype(o_ref.dtype)

    @pl.when(j >= gate_tile0)
    def _():
        o_ref[...] = jax.nn.sigmoid(acc + b_ref[...]).astype(o_ref.dtype)


def in_projection(h, w, b_full, tn, n_out, w_col, gate_tile0=None):
    T, K = h.shape
    tm = _tile(T, 1024)
    return pl.pallas_call(
        functools.partial(_in_proj_kernel, gate_tile0=gate_tile0),
        grid=(T // tm, n_out),
        in_specs=[pl.BlockSpec((tm, K), lambda i, j: (i, 0)),
                  pl.BlockSpec((K, tn), lambda i, j: (0, w_col(j))),
                  pl.BlockSpec((1, tn), lambda i, j: (0, w_col(j)))],
        out_specs=pl.BlockSpec((tm, tn), lambda i, j: (i, j)),
        out_shape=jax.ShapeDtypeStruct((T, n_out * tn), BF16),
        compiler_params=_params(("parallel", "arbitrary")),
    )(h, w, b_full)


def _attn_kernel(q_ref, kp_ref, kc_ref, kn_ref, vp_ref, vc_ref, vn_ref, o_ref, lse_ref, so, sl,
                 *, tq, sub, half, hps, dil, cls_len, slopes):
    i = pl.program_id(1)
    hh = pl.program_id(2)
    key_pad = -(sub + 2 * half) % LANES
    span = sub + 2 * half + key_pad
    scale = HEAD_DIM ** -0.5
    qidx = lax.broadcasted_iota(jnp.int32, (sub, span), 0)
    kidx = lax.broadcasted_iota(jnp.int32, (sub, span), 1)
    absrel = jnp.abs(kidx - half - qidx)
    absrel_f = absrel.astype(F32)
    lane = lax.broadcasted_iota(jnp.int32, (sub, LANES), 1)
    n_sub = tq // sub
    valids = []
    for u in range(n_sub):
        kpos = i * tq + (u * sub - half) + kidx
        valids.append((absrel <= half) & (kpos >= 0) & (kpos < cls_len))
    neg_slopes = []
    for h in range(hps):
        s_h = jnp.float32(0.0)
        for b in range(len(slopes) // hps):
            s_h = jnp.where(hh == b, jnp.float32(-slopes[b * hps + h] * dil), s_h)
        neg_slopes.append(s_h)

    @pl.when(hh == 0)
    def _():
        sl[...] = jnp.zeros_like(sl)

    def keys(prev_ref, cur_ref, next_ref, r, u, cols):
        lo = u * sub - half if u > 0 else 0
        hi = (u + 1) * sub + half if u < n_sub - 1 else tq
        parts = [cur_ref[0, r, lo:hi, cols]]
        if u == 0:
            parts.insert(0, prev_ref[0, r, :, cols])
        if u == n_sub - 1:
            parts.append(next_ref[0, r, :, cols])
        if key_pad:
            parts.append(jnp.zeros((key_pad, HEAD_DIM), cur_ref.dtype))
        return parts[0] if len(parts) == 1 else jnp.concatenate(parts, axis=0)

    for r in range(dil):
        for u in range(n_sub):
            rows = pl.ds(u * sub * dil + r, sub, stride=dil) if dil > 1 else pl.ds(u * sub, sub)
            lse_tile = sl[rows, :]
            for h in range(hps):
                cols = slice(h * HEAD_DIM, (h + 1) * HEAD_DIM)
                q = q_ref[0, r, u * sub:(u + 1) * sub, cols]
                k = keys(kp_ref, kc_ref, kn_ref, r, u, cols)
                v = keys(vp_ref, vc_ref, vn_ref, r, u, cols)
                s = lax.dot_general(q, k, (((1,), (1,)), ((), ())), preferred_element_type=F32)
                s = s * scale + absrel_f * neg_slopes[h]
                s = jnp.where(valids[u], s, NEG_INF)
                m = jnp.max(s, axis=-1, keepdims=True)
                p = jnp.exp(s - m)
                den = jnp.sum(p, axis=-1, keepdims=True)
                so[h, rows, :] = jnp.dot(p.astype(v.dtype), v, preferred_element_type=F32) / den
                lse_tile = jnp.where(lane == hh * hps + h, m + jnp.log(den), lse_tile)
            sl[rows, :] = lse_tile
    for h in range(hps):
        o_ref[0, :, h * HEAD_DIM:(h + 1) * HEAD_DIM] = so[h].astype(o_ref.dtype)
    lse_ref[0] = sl[...]


def band_attention(qkv, B, S, group, hpg, n_heads_total):
    window, dil = ATTN_PATTERNS[group]
    half = window // (2 * dil)
    W = hpg * HEAD_DIM
    L = S // dil
    tq = min(L, max(ATTN_Q_TILE, ATTN_ROW_TILE // dil))
    rows = tq * dil
    sub = min(ATTN_Q_TILE, tq)
    hps = max(1, hpg * ATTN_ROW_TILE // max(rows, ATTN_ROW_TILE))
    assert S % dil == 0 and L % tq == 0 and tq % sub == 0 and tq % half == 0 and half % 16 == 0 and hpg % hps == 0
    n_hh = hpg // hps
    Wb = hps * HEAD_DIM
    hb = tq // half
    n_hb = L // half
    s_all = 2.0 ** (-8.0 * np.arange(1, n_heads_total + 1) / n_heads_total)
    slopes = tuple(float(np.float32(v)) for v in s_all[group * hpg:(group + 1) * hpg])
    qv = qkv.reshape(B, dil, L, 3 * W)

    cur = lambda part: pl.BlockSpec((1, dil, tq, Wb), lambda b, i, hh: (b, 0, i, part * n_hh + hh))
    prv = lambda part: pl.BlockSpec(
        (1, dil, half, Wb), lambda b, i, hh: (b, 0, jnp.maximum(i * hb - 1, 0), part * n_hh + hh))
    nxt = lambda part: pl.BlockSpec(
        (1, dil, half, Wb), lambda b, i, hh: (b, 0, jnp.minimum((i + 1) * hb, n_hb - 1), part * n_hh + hh))
    o, lse = pl.pallas_call(
        functools.partial(_attn_kernel, tq=tq, sub=sub, half=half, hps=hps, dil=dil, cls_len=L, slopes=slopes),
        grid=(B, L // tq, n_hh),
        in_specs=[cur(0), prv(1), cur(1), nxt(1), prv(2), cur(2), nxt(2)],
        out_specs=[pl.BlockSpec((1, rows, Wb), lambda b, i, hh: (b, i, hh)),
                   pl.BlockSpec((1, rows, LANES), lambda b, i, hh: (b, i, 0))],
        out_shape=[jax.ShapeDtypeStruct((B, S, W), BF16), jax.ShapeDtypeStruct((B, S, LANES), F32)],
        scratch_shapes=[pltpu.VMEM((hps, rows, LANES), F32), pltpu.VMEM((rows, LANES), F32)],
        compiler_params=_params(("parallel", "arbitrary", "arbitrary")),
    )(qv, qv, qv, qv, qv, qv, qv)
    return o.reshape(B * S, W), lse.reshape(B * S, LANES)


def _dft_tables(n):
    k = jnp.arange(n, dtype=jnp.int32)

    def thin(cols):
        ang = ((k[:, None] * cols[None, :]) % n).astype(F32) * np.float32(2.0 * np.pi / n)
        return jnp.cos(ang), jnp.sin(ang)

    m = 64
    if n <= m or n % m:
        return thin(k)
    c_hi, s_hi = thin(jnp.arange(n // m, dtype=jnp.int32) * m)
    c_lo, s_lo = thin(jnp.arange(m, dtype=jnp.int32))
    cos = c_hi[:, :, None] * c_lo[:, None, :] - s_hi[:, :, None] * s_lo[:, None, :]
    sin = s_hi[:, :, None] * c_lo[:, None, :] + c_hi[:, :, None] * s_lo[:, None, :]
    return cos.reshape(n, n), sin.reshape(n, n)


def _fourier_chan_kernel(x_ref, t_ref, pc_ref, ps_ref):
    r = jnp.dot(x_ref[...], t_ref[...], preferred_element_type=F32)
    c = pc_ref.shape[-1]
    pc_ref[...] = r[:, :c].astype(pc_ref.dtype)
    ps_ref[...] = r[:, c:].astype(ps_ref.dtype)


def fourier_channel_stage(fg, cg, table):
    T = fg.shape[0]
    tm = _tile(T, 1024)
    return pl.pallas_call(
        _fourier_chan_kernel,
        grid=(T // tm, N_FOURIER_GROUPS),
        in_specs=[pl.BlockSpec((tm, cg), lambda i, g: (i, g)),
                  pl.BlockSpec((cg, 2 * cg), lambda i, g: (0, 0))],
        out_specs=[pl.BlockSpec((tm, cg), lambda i, g: (i, g)), pl.BlockSpec((tm, cg), lambda i, g: (i, g))],
        out_shape=[jax.ShapeDtypeStruct((T, N_FOURIER_GROUPS * cg), BF16)] * 2,
        compiler_params=_params(("parallel", "arbitrary")),
    )(fg, table)


def _fourier_seq_kernel(cs_ref, ss_ref, pc_ref, ps_ref, o_ref, acc_ref, *, scale):
    k = pl.program_id(3)

    @pl.when(k == 0)
    def _():
        acc_ref[...] = jnp.zeros_like(acc_ref)

    acc_ref[...] += (jnp.dot(cs_ref[...], pc_ref[0], preferred_element_type=F32)
                     + jnp.dot(ss_ref[...], ps_ref[0], preferred_element_type=F32))

    @pl.when(k == pl.num_programs(3) - 1)
    def _():
        o_ref[0] = (acc_ref[...] * scale).astype(o_ref.dtype)


def fourier_sequence_stage(pc, ps, cos_s, neg_sin_s, B, S, scale):
    N = pc.shape[1]
    tm, tn, tk = _tile(S, 1024), _tile(N, 1024), _tile(S, 2048)
    pc3, ps3 = pc.reshape(B, S, N), ps.reshape(B, S, N)
    out = pl.pallas_call(
        functools.partial(_fourier_seq_kernel, scale=scale),
        grid=(B, S // tm, N // tn, S // tk),
        in_specs=[pl.BlockSpec((tm, tk), lambda b, i, j, k: (i, k)),
                  pl.BlockSpec((tm, tk), lambda b, i, j, k: (i, k)),
                  pl.BlockSpec((1, tk, tn), lambda b, i, j, k: (b, k, j)),
                  pl.BlockSpec((1, tk, tn), lambda b, i, j, k: (b, k, j))],
        out_specs=pl.BlockSpec((1, tm, tn), lambda b, i, j, k: (b, i, j)),
        out_shape=jax.ShapeDtypeStruct((B, S, N), BF16),
        scratch_shapes=[pltpu.VMEM((tm, tn), F32)],
        compiler_params=_params(("parallel", "parallel", "parallel", "arbitrary")),
    )(cos_s, neg_sin_s, pc3, ps3)
    return out.reshape(B * S, N)


def _merge_kernel(o0_ref, o1_ref, o2_ref, l0_ref, l1_ref, l2_ref, fr_ref, wa_ref, wf_ref, ga_ref, gf_ref,
                  m_ref, oc_ref, *, hpg):
    @pl.when(pl.program_id(1) == 0)
    def _():
        l0, l1, l2 = l0_ref[...], l1_ref[...], l2_ref[...]
        m = jnp.maximum(jnp.maximum(l0, l1), l2)
        e0, e1, e2 = jnp.exp(l0 - m), jnp.exp(l1 - m), jnp.exp(l2 - m)
        den = e0 + e1 + e2
        w0, w1, w2 = e0 / den, e1 / den, e2 / den
        for h in range(hpg):
            cols = slice(h * HEAD_DIM, (h + 1) * HEAD_DIM)
            oc = (w0[:, h:h + 1] * o0_ref[:, cols].astype(F32)
                  + w1[:, h:h + 1] * o1_ref[:, cols].astype(F32)
                  + w2[:, h:h + 1] * o2_ref[:, cols].astype(F32))
            oc_ref[:, cols] = oc.astype(oc_ref.dtype)

    attn = jnp.dot(oc_ref[...], wa_ref[...], preferred_element_type=F32)
    four = jnp.dot(fr_ref[...], wf_ref[...], preferred_element_type=F32)
    m_ref[...] = (ga_ref[...].astype(F32) * attn + gf_ref[...].astype(F32) * four).astype(m_ref.dtype)


def merge_branches(os_, lses, fr, wa, wf, fg, gate_start, hpg):
    T, W = os_[0].shape
    D = wa.shape[1]
    FW = fr.shape[1]
    tm, tn = _tile(T, 512), _tile(math.gcd(D, gate_start), 1024)
    g0 = gate_start // tn
    row = lambda w: pl.BlockSpec((tm, w), lambda i, j: (i, 0))
    return pl.pallas_call(
        functools.partial(_merge_kernel, hpg=hpg),
        grid=(T // tm, D // tn),
        in_specs=[row(W), row(W), row(W), row(LANES), row(LANES), row(LANES), row(FW),
                  pl.BlockSpec((W, tn), lambda i, j: (0, j)),
                  pl.BlockSpec((FW, tn), lambda i, j: (0, j)),
                  pl.BlockSpec((tm, tn), lambda i, j: (i, g0 + j)),
                  pl.BlockSpec((tm, tn), lambda i, j: (i, g0 + D // tn + j))],
        out_specs=pl.BlockSpec((tm, tn), lambda i, j: (i, j)),
        out_shape=jax.ShapeDtypeStruct((T, D), BF16),
        scratch_shapes=[pltpu.VMEM((tm, W), BF16)],
        compiler_params=_params(("parallel", "arbitrary")),
    )(*os_, *lses, fr, wa, wf, fg, fg)


def _out_proj_kernel(m_ref, w_ref, x_ref, o_ref):
    o_ref[...] = x_ref[...] + jnp.dot(m_ref[...], w_ref[...], preferred_element_type=F32)


def out_projection(merged, w, x):
    T, K = merged.shape
    N = w.shape[1]
    tm, tn = _tile(T, 1024), _tile(N, 512)
    return pl.pallas_call(
        _out_proj_kernel,
        grid=(T // tm, N // tn),
        in_specs=[pl.BlockSpec((tm, K), lambda i, j: (i, 0)),
                  pl.BlockSpec((K, tn), lambda i, j: (0, j)),
                  pl.BlockSpec((tm, tn), lambda i, j: (i, j))],
        out_specs=pl.BlockSpec((tm, tn), lambda i, j: (i, j)),
        out_shape=jax.ShapeDtypeStruct((T, N), F32),
        compiler_params=_params(("parallel", "arbitrary")),
    )(merged, w, x)


def _router_kernel(*refs, bounds):
    n_in = len(bounds) - 1
    i = pl.program_id(0)
    for k in range(n_in):
        @pl.when((i >= bounds[k]) & (i < bounds[k + 1]))
        def _(k=k):
            _route_tile(refs[k], *refs[n_in:])


def _route_tile(x_ref, g_ref, w_ref, b_ref, eid_ref, cw_ref, hp_ref):
    x = x_ref[...]
    ms = jnp.mean(x * x, axis=-1, keepdims=True)
    hf = x * lax.rsqrt(ms + RMS_EPS) * g_ref[...]
    h = hf.astype(BF16)
    half_d = hf.shape[1] // 2
    packed = _pack_pairs(hf[:, :half_d], hf[:, half_d:])
    for c in range(hp_ref.shape[1]):
        hp_ref[:, c, :] = packed[:, c * LANES:(c + 1) * LANES]
    lg = jnp.dot(h, w_ref[...], preferred_element_type=F32) + b_ref[...]
    lane = lax.broadcasted_iota(jnp.int32, lg.shape, 1)
    big = jnp.int32(LANES)
    in_grp = lane < N_EXPERT_GROUPS
    gl = jnp.where(in_grp, lg, -jnp.inf)
    gmax = jnp.max(gl, axis=-1, keepdims=True)
    gsel = jnp.min(jnp.where(gl == gmax, lane, big), axis=-1, keepdims=True)
    p_group = 1.0 / jnp.sum(jnp.where(in_grp, jnp.exp(gl - gmax), 0.0), axis=-1, keepdims=True)
    lo = N_EXPERT_GROUPS + gsel * EXPERTS_PER_GROUP
    in_sel = (lane >= lo) & (lane < lo + EXPERTS_PER_GROUP)
    el = jnp.where(in_sel, lg, -jnp.inf)
    t1 = jnp.max(el, axis=-1, keepdims=True)
    i1 = jnp.min(jnp.where(el == t1, lane, big), axis=-1, keepdims=True)
    el2 = jnp.where(lane == i1, -jnp.inf, el)
    t2 = jnp.max(el2, axis=-1, keepdims=True)
    i2 = jnp.min(jnp.where(el2 == t2, lane, big), axis=-1, keepdims=True)
    e21 = jnp.exp(t2 - t1)
    p1 = 1.0 / (1.0 + e21)
    p2 = e21 / (1.0 + e21)
    eid_ref[...] = jnp.where(lane == 0, i1 - N_EXPERT_GROUPS, jnp.where(lane == 1, i2 - N_EXPERT_GROUPS, 0))
    cw_ref[...] = jnp.where(lane == 0, p_group * p1, jnp.where(lane == 1, p_group * p2, 0.0))


def route(x1s, g, w_router, b_router, tm):
    D = x1s[0].shape[1]
    nc = D // (2 * LANES)
    bounds = [0]
    for x1 in x1s:
        bounds.append(bounds[-1] + x1.shape[0] // tm)
    T = bounds[-1] * tm

    def x_spec(k):
        lo, n = bounds[k], bounds[k + 1] - bounds[k]
        return pl.BlockSpec((tm, D), lambda i: (jnp.clip(i - lo, 0, n - 1), 0))

    return pl.pallas_call(
        functools.partial(_router_kernel, bounds=tuple(bounds)),
        grid=(bounds[-1],),
        in_specs=[x_spec(k) for k in range(len(x1s))]
        + [pl.BlockSpec((1, D), lambda i: (0, 0)),
           pl.BlockSpec((D, LANES), lambda i: (0, 0)),
           pl.BlockSpec((1, LANES), lambda i: (0, 0))],
        out_specs=[pl.BlockSpec((tm, LANES), lambda i: (i, 0)), pl.BlockSpec((tm, LANES), lambda i: (i, 0)),
                   pl.BlockSpec((tm, nc, LANES), lambda i: (i, 0, 0))],
        out_shape=[jax.ShapeDtypeStruct((T, LANES), jnp.int32), jax.ShapeDtypeStruct((T, LANES), F32),
                   jax.ShapeDtypeStruct((T, nc, LANES), U32)],
        compiler_params=_params(("arbitrary",)),
    )(*x1s, g.reshape(1, D).astype(F32), w_router, b_router)


def _expert_up_kernel(src_ref, te_ref, nused_ref, hp_hbm, wg_ref, wu_ref, a_ref, buf, xs, sem, *, tme, nc):
    i = pl.program_id(0)
    n_used = nused_ref[0]

    def row_copy(tile, slot, r):
        return pltpu.make_async_copy(hp_hbm.at[src_ref[tile * tme + r]], buf.at[slot, pl.ds(r * nc, nc), :],
                                     sem.at[slot])

    def issue(tile, slot):
        def body(r, c):
            row_copy(tile, slot, r).start(priority=1)
            return c
        lax.fori_loop(0, tme, body, 0, unroll=8)

    @pl.when(i == 0)
    def _():
        issue(0, 0)

    @pl.when(i + 1 < n_used)
    def _():
        issue(i + 1, (i + 1) % 2)

    @pl.when(i < n_used)
    def _():
        slot = i % 2

        def drain(r, c):
            row_copy(i, slot, r).wait()
            return c
        lax.fori_loop(0, tme, drain, 0, unroll=8)
        for c in range(nc):
            lo, hi = _unpack_pairs(buf[slot, pl.ds(c, tme, stride=nc), :])
            xs[:, c * LANES:(c + 1) * LANES] = lo.astype(xs.dtype)
            xs[:, (nc + c) * LANES:(nc + c + 1) * LANES] = hi.astype(xs.dtype)
        x = xs[...]
        gate = jnp.dot(x, wg_ref[0], preferred_element_type=F32)
        up = jnp.dot(x, wu_ref[0], preferred_element_type=F32)
        a_ref[...] = (jax.nn.silu(gate) * up).astype(a_ref.dtype)

    @pl.when(i >= n_used)
    def _():
        a_ref[...] = jnp.zeros_like(a_ref)


def _expert_down_kernel(dst_ref, te_ref, nused_ref, a_ref, wd_ref, yt_hbm, ybuf, zbuf, sem, zsem,
                        *, tme, nc, n_split, n_real):
    i = pl.program_id(0)
    n_used = nused_ref[0]
    last = pl.num_programs(0) - 1
    half_d = nc * LANES
    wc = half_d // n_split
    cpc = wc // LANES

    def row_copy(tile, slot, r):
        return pltpu.make_async_copy(ybuf.at[slot, pl.ds(r * nc, nc), :], yt_hbm.at[dst_ref[tile * tme + r]],
                                     sem.at[slot])

    def drain(tile):
        def body(r, c):
            row_copy(tile, tile % 2, r).wait()
            return c
        lax.fori_loop(0, tme, body, 0, unroll=8)

    @pl.when(i == 0)
    def _():
        zbuf[...] = jnp.zeros_like(zbuf)

        def zero_copy(r):
            return pltpu.make_async_copy(zbuf, yt_hbm.at[n_real + r], zsem)

        def start(r, c):
            zero_copy(r).start()
            return c

        def wait(r, c):
            zero_copy(r).wait()
            return c
        lax.fori_loop(0, tme, start, 0, unroll=8)
        lax.fori_loop(0, tme, wait, 0, unroll=8)

    @pl.when((i >= 2) & (i - 2 < n_used))
    def _():
        drain(i - 2)

    @pl.when(i < n_used)
    def _():
        slot = i % 2
        a = a_ref[...]
        for k in range(n_split):
            lo = jnp.dot(a, wd_ref[0, :, k * wc:(k + 1) * wc], preferred_element_type=F32)
            hi = jnp.dot(a, wd_ref[0, :, half_d + k * wc:half_d + (k + 1) * wc], preferred_element_type=F32)
            packed = _pack_pairs(lo, hi)
            for cc in range(cpc):
                ybuf[slot, pl.ds(k * cpc + cc, tme, stride=nc), :] = packed[:, cc * LANES:(cc + 1) * LANES]

        def body(r, c):
            row_copy(i, slot, r).start(priority=1)
            return c
        lax.fori_loop(0, tme, body, 0, unroll=8)

    @pl.when((i == last) & (i >= 1) & (i - 1 < n_used))
    def _():
        drain(i - 1)

    @pl.when((i == last) & (i < n_used))
    def _():
        drain(i)


def expert_ffn(hp, w_gate, w_up, w_down, src, dst, tile_expert, n_used, n_tiles, tme, n_rows_out):
    _, nc, _ = hp.shape
    D = nc * 2 * LANES
    F = w_gate.shape[2]
    P = n_tiles * tme
    act = pl.pallas_call(
        functools.partial(_expert_up_kernel, tme=tme, nc=nc),
        grid_spec=pltpu.PrefetchScalarGridSpec(
            num_scalar_prefetch=3,
            grid=(n_tiles,),
            in_specs=[pl.BlockSpec(memory_space=pl.ANY),
                      pl.BlockSpec((1, D, F), lambda i, s, te, n: (te[i], 0, 0)),
                      pl.BlockSpec((1, D, F), lambda i, s, te, n: (te[i], 0, 0))],
            out_specs=pl.BlockSpec((tme, F), lambda i, s, te, n: (i, 0)),
            scratch_shapes=[pltpu.VMEM((2, tme * nc, LANES), U32), pltpu.VMEM((tme, D), BF16),
                            pltpu.SemaphoreType.DMA((2,))],
        ),
        out_shape=jax.ShapeDtypeStruct((P, F), BF16),
        compiler_params=_params(("arbitrary",)),
    )(src, tile_expert, n_used, hp, w_gate, w_up)
    n_split = max(1, (nc * LANES) // 512)
    return pl.pallas_call(
        functools.partial(_expert_down_kernel, tme=tme, nc=nc, n_split=n_split, n_real=n_rows_out - tme),
        grid_spec=pltpu.PrefetchScalarGridSpec(
            num_scalar_prefetch=3,
            grid=(n_tiles,),
            in_specs=[pl.BlockSpec((tme, F), lambda i, d, te, n: (i, 0)),
                      pl.BlockSpec((1, F, D), lambda i, d, te, n: (te[i], 0, 0))],
            out_specs=pl.BlockSpec(memory_space=pl.ANY),
            scratch_shapes=[pltpu.VMEM((2, tme * nc, LANES), U32), pltpu.VMEM((nc, LANES), U32),
                            pltpu.SemaphoreType.DMA((2,)), pltpu.SemaphoreType.DMA],
        ),
        out_shape=jax.ShapeDtypeStruct((n_rows_out, nc, LANES), U32),
        compiler_params=_params(("arbitrary",)),
    )(dst, tile_expert, n_used, act, w_down)


def _final_kernel(x_ref, cw_ref, y_ref, g_ref, o_ref, *, tm, nc):
    cw0 = cw_ref[:, 0:1]
    cw1 = cw_ref[:, 1:2]
    ss = jnp.zeros((tm, 1), F32)
    for c in range(nc):
        lo0, hi0 = _unpack_pairs(y_ref[pl.ds(c, tm, stride=2 * nc), :])
        lo1, hi1 = _unpack_pairs(y_ref[pl.ds(nc + c, tm, stride=2 * nc), :])
        cl = slice(c * LANES, (c + 1) * LANES)
        ch = slice((nc + c) * LANES, (nc + c + 1) * LANES)
        xl = x_ref[:, cl] + (cw0 * lo0 + cw1 * lo1)
        xh = x_ref[:, ch] + (cw0 * hi0 + cw1 * hi1)
        o_ref[:, cl] = xl
        o_ref[:, ch] = xh
        ss = ss + jnp.sum(xl * xl, axis=-1, keepdims=True) + jnp.sum(xh * xh, axis=-1, keepdims=True)
    inv = lax.rsqrt(ss / (2 * nc * LANES) + RMS_EPS)
    o_ref[...] = o_ref[...] * inv * g_ref[...]


def final_combine(x1, cw, yt, g, tm, tile0):
    T, D = x1.shape
    nc = yt.shape[1]
    y2d = yt.reshape(yt.shape[0] * nc, LANES)
    return pl.pallas_call(
        functools.partial(_final_kernel, tm=tm, nc=nc),
        grid=(T // tm,),
        in_specs=[pl.BlockSpec((tm, D), lambda i: (i, 0)),
                  pl.BlockSpec((tm, LANES), lambda i: (tile0 + i, 0)),
                  pl.BlockSpec((2 * tm * nc, LANES), lambda i: (tile0 + i, 0)),
                  pl.BlockSpec((1, D), lambda i: (0, 0))],
        out_specs=pl.BlockSpec((tm, D), lambda i: (i, 0)),
        out_shape=jax.ShapeDtypeStruct((T, D), F32),
        compiler_params=_params(("parallel",)),
    )(x1, cw, y2d, g.reshape(1, D).astype(F32))


def _sorted_layout(eid, tme):
    T = eid.shape[0]
    n_tiles = (2 * T) // tme + N_EXPERTS
    P = n_tiles * tme
    flat_e = eid.reshape(-1)
    onehot = (flat_e[:, None] == jnp.arange(N_EXPERTS, dtype=jnp.int32)[None, :]).astype(jnp.int32)
    csum = jnp.cumsum(onehot, axis=0)
    rank = jnp.take_along_axis(csum, flat_e[:, None], axis=1)[:, 0] - 1
    counts = csum[-1]
    padded = ((counts + tme - 1) // tme) * tme
    ends = jnp.cumsum(padded)
    pos = (ends - padded)[flat_e] + rank
    spare = 2 * T + jnp.arange(P, dtype=jnp.int32) % tme
    dst = spare.at[pos].set(jnp.arange(2 * T, dtype=jnp.int32))
    src = jnp.where(dst < 2 * T, dst // 2, 0)
    n_used = (ends[-1] // tme).astype(jnp.int32)
    tile_start = jnp.minimum(jnp.arange(n_tiles, dtype=jnp.int32), n_used - 1) * tme
    tile_expert = jnp.minimum(jnp.searchsorted(ends, tile_start, side="right"), N_EXPERTS - 1).astype(jnp.int32)
    return src, dst, tile_expert, n_used.reshape(1), n_tiles


def _encoder_trunk(x, wts):
    B, S, D = x.shape
    T = B * S
    hpg, cg = wts["hpg"], wts["cg"]
    n_grp = len(ATTN_PATTERNS)
    W = hpg * HEAD_DIM
    f_start = 3 * n_grp * W
    f_width = N_FOURIER_GROUPS * cg
    x2d = x.reshape(T, D)

    dils = sorted({d for _, d in ATTN_PATTERNS if d > 1})
    h, h_cm = rmsnorm_cast(x2d, wts["attn_norm_g"], B, S, dils)
    h_cm[1] = h

    outs, lses = [], []
    for g, (_, dil) in enumerate(ATTN_PATTERNS):
        qkv = in_projection(h_cm[dil], wts["w_in"], wts["b_full"], W, 3, lambda j, g=g: j * n_grp + g)
        o_g, lse_g = band_attention(qkv, B, S, g, hpg, n_grp * hpg)
        outs.append(o_g)
        lses.append(lse_g)

    tn = _tile(math.gcd(math.gcd(f_start, f_width), 2 * D), 1024)
    c0 = f_start // tn
    fg = in_projection(h, wts["w_in"], wts["b_full"], tn, (f_width + 2 * D) // tn, lambda j: c0 + j,
                       gate_tile0=f_width // tn)

    pc, ps = fourier_channel_stage(fg, cg, wts["chan_table"])
    cos_s, sin_s = _dft_tables(S)
    fr = fourier_sequence_stage(pc, ps, cos_s.astype(BF16), (-sin_s).astype(BF16), B, S,
                                float(1.0 / math.sqrt(S * cg)))

    merged = merge_branches(outs, lses, fr, wts["w_branch_attn"], wts["w_branch_fourier"], fg, f_width, hpg)
    return out_projection(merged, wts["w_out"], x2d)


def _moe_and_final_norm(x1s, wts):
    t_all = sum(x1.shape[0] for x1 in x1s)
    tm = _tile(math.gcd(*[x1.shape[0] for x1 in x1s]), 256, 8)
    eid, cw, hp = route(x1s, wts["ffn_norm_g"], wts["w_router"], wts["b_router"], tm)
    tme = _tile(2 * t_all, EXPERT_ROW_TILE, 8)
    src, dst, tile_expert, n_used, n_tiles = _sorted_layout(eid[:, :2], tme)
    yt = expert_ffn(hp, wts["w_expert_gate"], wts["w_expert_up"], wts["w_expert_down"], src, dst, tile_expert,
                    n_used, n_tiles, tme, 2 * t_all + tme)
    outs, tile0 = [], 0
    for x1 in x1s:
        outs.append(final_combine(x1, cw, yt, wts["final_norm_g"], tm, tile0))
        tile0 += x1.shape[0] // tm
    return outs


def kernel(x_prompt, x_sample, attn_norm_g, w_in, w_branch_attn, w_branch_fourier, b_gate, w_out, ffn_norm_g, w_router_group, b_router_group, w_router_expert, b_router_expert, w_expert_gate, w_expert_up, w_expert_down, final_norm_g):
    assert w_in.shape[0] == 1, "the final norm is fused into the layer's last kernel: one layer only"
    l = 0
    D = x_prompt.shape[-1]
    hpg = w_branch_attn.shape[1] // HEAD_DIM
    cg = w_branch_fourier.shape[1] // N_FOURIER_GROUPS
    in_width = w_in.shape[2]
    n_gate = b_gate.shape[1]
    cos_c, sin_c = _dft_tables(cg)
    w_r = jnp.concatenate(
        [w_router_group[l], jnp.transpose(w_router_expert[l], (1, 0, 2)).reshape(D, N_EXPERTS)], axis=1)
    b_r = jnp.concatenate([b_router_group[l], b_router_expert[l].reshape(-1)])
    pad = LANES - w_r.shape[1]
    wts = dict(
        hpg=hpg, cg=cg, chan_table=jnp.concatenate([cos_c, sin_c], axis=1).astype(BF16),
        attn_norm_g=attn_norm_g[l], ffn_norm_g=ffn_norm_g[l], final_norm_g=final_norm_g,
        w_in=w_in[l].astype(BF16),
        b_full=jnp.concatenate([jnp.zeros((in_width - n_gate,), F32), b_gate[l].astype(F32)]).reshape(1, in_width),
        w_branch_attn=w_branch_attn[l].astype(BF16),
        w_branch_fourier=w_branch_fourier[l].astype(BF16),
        w_out=w_out[l].astype(BF16),
        w_router=jnp.pad(w_r, ((0, 0), (0, pad))).astype(BF16),
        b_router=jnp.pad(b_r, (0, pad)).reshape(1, LANES).astype(F32),
        w_expert_gate=w_expert_gate[l].astype(BF16),
        w_expert_up=w_expert_up[l].astype(BF16),
        w_expert_down=w_expert_down[l].astype(BF16),
    )
    xs = (x_prompt, x_sample)
    outs = _moe_and_final_norm([_encoder_trunk(x, wts) for x in xs], wts)
    return tuple(o.reshape(x.shape) for o, x in zip(outs, xs))
```

```python
import functools
import math

import numpy as np
import jax
import jax.numpy as jnp
from jax import lax
from jax.experimental import pallas as pl
from jax.experimental.pallas import tpu as pltpu

F32 = jnp.float32
BF16 = jnp.bfloat16
U32 = jnp.uint32

RMS_EPS = 1e-6
NEG_INF = -1e30
HEAD_DIM = 128
ATTN_PATTERNS = ((128, 1), (512, 4), (2048, 16))
N_FOURIER_GROUPS = 4
N_EXPERT_GROUPS = 4
EXPERTS_PER_GROUP = 4
N_EXPERTS = N_EXPERT_GROUPS * EXPERTS_PER_GROUP
LANES = 128
V7X_VMEM_LIMIT = 56 * 1024 * 1024
ATTN_Q_TILE = 128
ATTN_ROW_TILE = 1024
EXPERT_ROW_TILE = 512
HI_MASK = 0xFFFF0000


def _tile(n, pref, mult=LANES):
    if n <= pref:
        return n
    t = (pref // mult) * mult
    while t >= mult:
        if n % t == 0:
            return t
        t -= mult
    raise ValueError(f"no tile for {n} <= {pref}")


def _params(sem, vmem=V7X_VMEM_LIMIT):
    return pltpu.CompilerParams(dimension_semantics=sem, vmem_limit_bytes=vmem)


def _pack_pairs(lo, hi):
    a = pltpu.bitcast(lo.astype(BF16).astype(F32), U32) >> 16
    b = pltpu.bitcast(hi.astype(BF16).astype(F32), U32) & jnp.uint32(HI_MASK)
    return a | b


def _unpack_pairs(w):
    return pltpu.bitcast(w << 16, F32), pltpu.bitcast(w & jnp.uint32(HI_MASK), F32)


def _rms_kernel(x_ref, g_ref, *refs, dils):
    o_ref = refs[0]
    cm_refs = refs[1:1 + len(dils)]
    scr = refs[-1] if dils else None
    x = x_ref[...]
    ms = jnp.mean(x * x, axis=-1, keepdims=True)
    h = x * lax.rsqrt(ms + RMS_EPS) * g_ref[...]
    o_ref[...] = h.astype(o_ref.dtype)
    if not dils:
        return
    tm = x.shape[0]
    nch = x.shape[1] // LANES
    for c in range(nch):
        scr[c] = h[:, c * LANES:(c + 1) * LANES]
    for cm_ref, d in zip(cm_refs, dils):
        for r in range(d):
            for c in range(nch):
                cm_ref[0, r, :, c * LANES:(c + 1) * LANES] = scr[c, pl.ds(r, tm // d, stride=d), :].astype(cm_ref.dtype)


def rmsnorm_cast(x, g, B, S, dils):
    T, D = x.shape
    tm = _tile(S, 256, 16 * max(dils, default=1))
    spb = S // tm
    outs = pl.pallas_call(
        functools.partial(_rms_kernel, dils=tuple(dils)),
        grid=(T // tm,),
        in_specs=[pl.BlockSpec((tm, D), lambda i: (i, 0)), pl.BlockSpec((1, D), lambda i: (0, 0))],
        out_specs=[pl.BlockSpec((tm, D), lambda i: (i, 0))]
        + [pl.BlockSpec((1, d, tm // d, D), lambda i: (i // spb, 0, i % spb, 0)) for d in dils],
        out_shape=[jax.ShapeDtypeStruct((T, D), BF16)]
        + [jax.ShapeDtypeStruct((B, d, S // d, D), BF16) for d in dils],
        scratch_shapes=[pltpu.VMEM((D // LANES, tm, LANES), F32)] if dils else [],
        compiler_params=_params(("parallel",)),
    )(x, g.reshape(1, D).astype(F32))
    return outs[0], {d: o.reshape(T, D) for d, o in zip(dils, outs[1:])}


def _in_proj_kernel(h_ref, w_ref, b_ref, o_ref, *, gate_tile0):
    acc = jnp.dot(h_ref[...], w_ref[...], preferred_element_type=F32)
    if gate_tile0 is None:
        o_ref[...] = acc.astype(o_ref.dtype)
        return
    j = pl.program_id(1)

    @pl.when(j < gate_tile0)
    def _():
        o_ref[...] = acc.astype(o_ref.dtype)

    @pl.when(j >= gate_tile0)
    def _():
        o_ref[...] = jax.nn.sigmoid(acc + b_ref[...]).astype(o_ref.dtype)


def in_projection(h, w, b_full, tn, n_out, w_col, gate_tile0=None):
    T, K = h.shape
    tm = _tile(T, 1024)
    return pl.pallas_call(
        functools.partial(_in_proj_kernel, gate_tile0=gate_tile0),
        grid=(T // tm, n_out),
        in_specs=[pl.BlockSpec((tm, K), lambda i, j: (i, 0)),
                  pl.BlockSpec((K, tn), lambda i, j: (0, w_col(j))),
                  pl.BlockSpec((1, tn), lambda i, j: (0, w_col(j)))],
        out_specs=pl.BlockSpec((tm, tn), lambda i, j: (i, j)),
        out_shape=jax.ShapeDtypeStruct((T, n_out * tn), BF16),
        compiler_params=_params(("parallel", "arbitrary")),
    )(h, w, b_full)


def _attn_kernel(q_ref, kp_ref, kc_ref, kn_ref, vp_ref, vc_ref, vn_ref, o_ref, lse_ref, so, sl,
                 *, tq, sub, half, hps, dil, cls_len, slopes):
    i = pl.program_id(1)
    hh = pl.program_id(2)
    key_pad = -(sub + 2 * half) % LANES
    span = sub + 2 * half + key_pad
    scale = HEAD_DIM ** -0.5
    qidx = lax.broadcasted_iota(jnp.int32, (sub, span), 0)
    kidx = lax.broadcasted_iota(jnp.int32, (sub, span), 1)
    absrel = jnp.abs(kidx - half - qidx)
    absrel_f = absrel.astype(F32)
    lane = lax.broadcasted_iota(jnp.int32, (sub, LANES), 1)
    n_sub = tq // sub
    valids = []
    for u in range(n_sub):
        kpos = i * tq + (u * sub - half) + kidx
        valids.append((absrel <= half) & (kpos >= 0) & (kpos < cls_len))
    neg_slopes = []
    for h in range(hps):
        s_h = jnp.float32(0.0)
        for b in range(len(slopes) // hps):
            s_h = jnp.where(hh == b, jnp.float32(-slopes[b * hps + h] * dil), s_h)
        neg_slopes.append(s_h)

    @pl.when(hh == 0)
    def _():
        sl[...] = jnp.zeros_like(sl)

    def keys(prev_ref, cur_ref, next_ref, r, u, cols):
        lo = u * sub - half if u > 0 else 0
        hi = (u + 1) * sub + half if u < n_sub - 1 else tq
        parts = [cur_ref[0, r, lo:hi, cols]]
        if u == 0:
            parts.insert(0, prev_ref[0, r, :, cols])
        if u == n_sub - 1:
            parts.append(next_ref[0, r, :, cols])
        if key_pad:
            parts.append(jnp.zeros((key_pad, HEAD_DIM), cur_ref.dtype))
        return parts[0] if len(parts) == 1 else jnp.concatenate(parts, axis=0)

    for r in range(dil):
        for u in range(n_sub):
            rows = pl.ds(u * sub * dil + r, sub, stride=dil) if dil > 1 else pl.ds(u * sub, sub)
            lse_tile = sl[rows, :]
            for h in range(hps):
                cols = slice(h * HEAD_DIM, (h + 1) * HEAD_DIM)
                q = q_ref[0, r, u * sub:(u + 1) * sub, cols]
                k = keys(kp_ref, kc_ref, kn_ref, r, u, cols)
                v = keys(vp_ref, vc_ref, vn_ref, r, u, cols)
                s = lax.dot_general(q, k, (((1,), (1,)), ((), ())), preferred_element_type=F32)
                s = s * scale + absrel_f * neg_slopes[h]
                s = jnp.where(valids[u], s, NEG_INF)
                m = jnp.max(s, axis=-1, keepdims=True)
                p = jnp.exp(s - m)
                den = jnp.sum(p, axis=-1, keepdims=True)
                so[h, rows, :] = jnp.dot(p.astype(v.dtype), v, preferred_element_type=F32) / den
                lse_tile = jnp.where(lane == hh * hps + h, m + jnp.log(den), lse_tile)
            sl[rows, :] = lse_tile
    for h in range(hps):
        o_ref[0, :, h * HEAD_DIM:(h + 1) * HEAD_DIM] = so[h].astype(o_ref.dtype)
    lse_ref[0] = sl[...]


def band_attention(qkv, B, S, group, hpg, n_heads_total):
    window, dil = ATTN_PATTERNS[group]
    half = window // (2 * dil)
    W = hpg * HEAD_DIM
    L = S // dil
    tq = min(L, max(ATTN_Q_TILE, ATTN_ROW_TILE // dil))
    rows = tq * dil
    sub = min(ATTN_Q_TILE, tq)
    hps = max(1, hpg * ATTN_ROW_TILE // max(rows, ATTN_ROW_TILE))
    assert S % dil == 0 and L % tq == 0 and tq % sub == 0 and tq % half == 0 and half % 16 == 0 and hpg % hps == 0
    n_hh = hpg // hps
    Wb = hps * HEAD_DIM
    hb = tq // half
    n_hb = L // half
    s_all = 2.0 ** (-8.0 * np.arange(1, n_heads_total + 1) / n_heads_total)
    slopes = tuple(float(np.float32(v)) for v in s_all[group * hpg:(group + 1) * hpg])
    qv = qkv.reshape(B, dil, L, 3 * W)

    cur = lambda part: pl.BlockSpec((1, dil, tq, Wb), lambda b, i, hh: (b, 0, i, part * n_hh + hh))
    prv = lambda part: pl.BlockSpec(
        (1, dil, half, Wb), lambda b, i, hh: (b, 0, jnp.maximum(i * hb - 1, 0), part * n_hh + hh))
    nxt = lambda part: pl.BlockSpec(
        (1, dil, half, Wb), lambda b, i, hh: (b, 0, jnp.minimum((i + 1) * hb, n_hb - 1), part * n_hh + hh))
    o, lse = pl.pallas_call(
        functools.partial(_attn_kernel, tq=tq, sub=sub, half=half, hps=hps, dil=dil, cls_len=L, slopes=slopes),
        grid=(B, L // tq, n_hh),
        in_specs=[cur(0), prv(1), cur(1), nxt(1), prv(2), cur(2), nxt(2)],
        out_specs=[pl.BlockSpec((1, rows, Wb), lambda b, i, hh: (b, i, hh)),
                   pl.BlockSpec((1, rows, LANES), lambda b, i, hh: (b, i, 0))],
        out_shape=[jax.ShapeDtypeStruct((B, S, W), BF16), jax.ShapeDtypeStruct((B, S, LANES), F32)],
        scratch_shapes=[pltpu.VMEM((hps, rows, LANES), F32), pltpu.VMEM((rows, LANES), F32)],
        compiler_params=_params(("parallel", "arbitrary", "arbitrary")),
    )(qv, qv, qv, qv, qv, qv, qv)
    return o.reshape(B * S, W), lse.reshape(B * S, LANES)


def _dft_tables(n):
    k = jnp.arange(n, dtype=jnp.int32)

    def thin(cols):
        ang = ((k[:, None] * cols[None, :]) % n).astype(F32) * np.float32(2.0 * np.pi / n)
        return jnp.cos(ang), jnp.sin(ang)

    m = 64
    if n <= m or n % m:
        return thin(k)
    c_hi, s_hi = thin(jnp.arange(n // m, dtype=jnp.int32) * m)
    c_lo, s_lo = thin(jnp.arange(m, dtype=jnp.int32))
    cos = c_hi[:, :, None] * c_lo[:, None, :] - s_hi[:, :, None] * s_lo[:, None, :]
    sin = s_hi[:, :, None] * c_lo[:, None, :] + c_hi[:, :, None] * s_lo[:, None, :]
    return cos.reshape(n, n), sin.reshape(n, n)


def _fourier_chan_kernel(x_ref, t_ref, pc_ref, ps_ref):
    r = jnp.dot(x_ref[...], t_ref[...], preferred_element_type=F32)
    c = pc_ref.shape[-1]
    pc_ref[...] = r[:, :c].astype(pc_ref.dtype)
    ps_ref[...] = r[:, c:].astype(ps_ref.dtype)


def fourier_channel_stage(fg, cg, table):
    T = fg.shape[0]
    tm = _tile(T, 1024)
    return pl.pallas_call(
        _fourier_chan_kernel,
        grid=(T // tm, N_FOURIER_GROUPS),
        in_specs=[pl.BlockSpec((tm, cg), lambda i, g: (i, g)),
                  pl.BlockSpec((cg, 2 * cg), lambda i, g: (0, 0))],
        out_specs=[pl.BlockSpec((tm, cg), lambda i, g: (i, g)), pl.BlockSpec((tm, cg), lambda i, g: (i, g))],
        out_shape=[jax.ShapeDtypeStruct((T, N_FOURIER_GROUPS * cg), BF16)] * 2,
        compiler_params=_params(("parallel", "arbitrary")),
    )(fg, table)


def _fourier_seq_kernel(cs_ref, ss_ref, pc_ref, ps_ref, o_ref, acc_ref, *, scale):
    k = pl.program_id(3)

    @pl.when(k == 0)
    def _():
        acc_ref[...] = jnp.zeros_like(acc_ref)

    acc_ref[...] += (jnp.dot(cs_ref[...], pc_ref[0], preferred_element_type=F32)
                     + jnp.dot(ss_ref[...], ps_ref[0], preferred_element_type=F32))

    @pl.when(k == pl.num_programs(3) - 1)
    def _():
        o_ref[0] = (acc_ref[...] * scale).astype(o_ref.dtype)


def fourier_sequence_stage(pc, ps, cos_s, neg_sin_s, B, S, scale):
    N = pc.shape[1]
    tm, tn, tk = _tile(S, 1024), _tile(N, 1024), _tile(S, 2048)
    pc3, ps3 = pc.reshape(B, S, N), ps.reshape(B, S, N)
    out = pl.pallas_call(
        functools.partial(_fourier_seq_kernel, scale=scale),
        grid=(B, S // tm, N // tn, S // tk),
        in_specs=[pl.BlockSpec((tm, tk), lambda b, i, j, k: (i, k)),
                  pl.BlockSpec((tm, tk), lambda b, i, j, k: (i, k)),
                  pl.BlockSpec((1, tk, tn), lambda b, i, j, k: (b, k, j)),
                  pl.BlockSpec((1, tk, tn), lambda b, i, j, k: (b, k, j))],
        out_specs=pl.BlockSpec((1, tm, tn), lambda b, i, j, k: (b, i, j)),
        out_shape=jax.ShapeDtypeStruct((B, S, N), BF16),
        scratch_shapes=[pltpu.VMEM((tm, tn), F32)],
        compiler_params=_params(("parallel", "parallel", "parallel", "arbitrary")),
    )(cos_s, neg_sin_s, pc3, ps3)
    return out.reshape(B * S, N)


def _merge_kernel(o0_ref, o1_ref, o2_ref, l0_ref, l1_ref, l2_ref, fr_ref, wa_ref, wf_ref, ga_ref, gf_ref,
                  m_ref, oc_ref, *, hpg):
    @pl.when(pl.program_id(1) == 0)
    def _():
        l0, l1, l2 = l0_ref[...], l1_ref[...], l2_ref[...]
        m = jnp.maximum(jnp.maximum(l0, l1), l2)
        e0, e1, e2 = jnp.exp(l0 - m), jnp.exp(l1 - m), jnp.exp(l2 - m)
        den = e0 + e1 + e2
        w0, w1, w2 = e0 / den, e1 / den, e2 / den
        for h in range(hpg):
            cols = slice(h * HEAD_DIM, (h + 1) * HEAD_DIM)
            oc = (w0[:, h:h + 1] * o0_ref[:, cols].astype(F32)
                  + w1[:, h:h + 1] * o1_ref[:, cols].astype(F32)
                  + w2[:, h:h + 1] * o2_ref[:, cols].astype(F32))
            oc_ref[:, cols] = oc.astype(oc_ref.dtype)

    attn = jnp.dot(oc_ref[...], wa_ref[...], preferred_element_type=F32)
    four = jnp.dot(fr_ref[...], wf_ref[...], preferred_element_type=F32)
    m_ref[...] = (ga_ref[...].astype(F32) * attn + gf_ref[...].astype(F32) * four).astype(m_ref.dtype)


def merge_branches(os_, lses, fr, wa, wf, fg, gate_start, hpg):
    T, W = os_[0].shape
    D = wa.shape[1]
    FW = fr.shape[1]
    tm, tn = _tile(T, 1024), _tile(math.gcd(D, gate_start), 512)
    g0 = gate_start // tn
    row = lambda w: pl.BlockSpec((tm, w), lambda i, j: (i, 0))
    return pl.pallas_call(
        functools.partial(_merge_kernel, hpg=hpg),
        grid=(T // tm, D // tn),
        in_specs=[row(W), row(W), row(W), row(LANES), row(LANES), row(LANES), row(FW),
                  pl.BlockSpec((W, tn), lambda i, j: (0, j)),
                  pl.BlockSpec((FW, tn), lambda i, j: (0, j)),
                  pl.BlockSpec((tm, tn), lambda i, j: (i, g0 + j)),
                  pl.BlockSpec((tm, tn), lambda i, j: (i, g0 + D // tn + j))],
        out_specs=pl.BlockSpec((tm, tn), lambda i, j: (i, j)),
        out_shape=jax.ShapeDtypeStruct((T, D), BF16),
        scratch_shapes=[pltpu.VMEM((tm, W), BF16)],
        compiler_params=_params(("parallel", "arbitrary")),
    )(*os_, *lses, fr, wa, wf, fg, fg)


def _out_proj_kernel(m_ref, w_ref, x_ref, o_ref):
    o_ref[...] = x_ref[...] + jnp.dot(m_ref[...], w_ref[...], preferred_element_type=F32)


def out_projection(merged, w, x):
    T, K = merged.shape
    N = w.shape[1]
    tm, tn = _tile(T, 1024), _tile(N, 512)
    return pl.pallas_call(
        _out_proj_kernel,
        grid=(T // tm, N // tn),
        in_specs=[pl.BlockSpec((tm, K), lambda i, j: (i, 0)),
                  pl.BlockSpec((K, tn), lambda i, j: (0, j)),
                  pl.BlockSpec((tm, tn), lambda i, j: (i, j))],
        out_specs=pl.BlockSpec((tm, tn), lambda i, j: (i, j)),
        out_shape=jax.ShapeDtypeStruct((T, N), F32),
        compiler_params=_params(("parallel", "arbitrary")),
    )(merged, w, x)


def _router_kernel(*refs, bounds):
    n_in = len(bounds) - 1
    i = pl.program_id(0)
    for k in range(n_in):
        @pl.when((i >= bounds[k]) & (i < bounds[k + 1]))
        def _(k=k):
            _route_tile(refs[k], *refs[n_in:])


def _route_tile(x_ref, g_ref, w_ref, b_ref, eid_ref, cw_ref, hp_ref):
    x = x_ref[...]
    ms = jnp.mean(x * x, axis=-1, keepdims=True)
    hf = x * lax.rsqrt(ms + RMS_EPS) * g_ref[...]
    h = hf.astype(BF16)
    half_d = hf.shape[1] // 2
    hp_ref[...] = _pack_pairs(hf[:, :half_d], hf[:, half_d:])
    lg = jnp.dot(h, w_ref[...], preferred_element_type=F32) + b_ref[...]
    lane = lax.broadcasted_iota(jnp.int32, lg.shape, 1)
    big = jnp.int32(LANES)
    in_grp = lane < N_EXPERT_GROUPS
    gl = jnp.where(in_grp, lg, -jnp.inf)
    gmax = jnp.max(gl, axis=-1, keepdims=True)
    gsel = jnp.min(jnp.where(gl == gmax, lane, big), axis=-1, keepdims=True)
    p_group = 1.0 / jnp.sum(jnp.where(in_grp, jnp.exp(gl - gmax), 0.0), axis=-1, keepdims=True)
    lo = N_EXPERT_GROUPS + gsel * EXPERTS_PER_GROUP
    in_sel = (lane >= lo) & (lane < lo + EXPERTS_PER_GROUP)
    el = jnp.where(in_sel, lg, -jnp.inf)
    t1 = jnp.max(el, axis=-1, keepdims=True)
    i1 = jnp.min(jnp.where(el == t1, lane, big), axis=-1, keepdims=True)
    el2 = jnp.where(lane == i1, -jnp.inf, el)
    t2 = jnp.max(el2, axis=-1, keepdims=True)
    i2 = jnp.min(jnp.where(el2 == t2, lane, big), axis=-1, keepdims=True)
    e21 = jnp.exp(t2 - t1)
    p1 = 1.0 / (1.0 + e21)
    p2 = e21 / (1.0 + e21)
    eid_ref[...] = jnp.where(lane == 0, i1 - N_EXPERT_GROUPS, jnp.where(lane == 1, i2 - N_EXPERT_GROUPS, 0))
    cw_ref[...] = jnp.where(lane == 0, p_group * p1, jnp.where(lane == 1, p_group * p2, 0.0))


def route(x1s, g, w_router, b_router, tm):
    D = x1s[0].shape[1]
    bounds = [0]
    for x1 in x1s:
        bounds.append(bounds[-1] + x1.shape[0] // tm)
    T = bounds[-1] * tm

    def x_spec(k):
        lo, n = bounds[k], bounds[k + 1] - bounds[k]
        return pl.BlockSpec((tm, D), lambda i: (jnp.clip(i - lo, 0, n - 1), 0))

    return pl.pallas_call(
        functools.partial(_router_kernel, bounds=tuple(bounds)),
        grid=(bounds[-1],),
        in_specs=[x_spec(k) for k in range(len(x1s))]
        + [pl.BlockSpec((1, D), lambda i: (0, 0)),
           pl.BlockSpec((D, LANES), lambda i: (0, 0)),
           pl.BlockSpec((1, LANES), lambda i: (0, 0))],
        out_specs=[pl.BlockSpec((tm, LANES), lambda i: (i, 0)), pl.BlockSpec((tm, LANES), lambda i: (i, 0)),
                   pl.BlockSpec((tm, D // 2), lambda i: (i, 0))],
        out_shape=[jax.ShapeDtypeStruct((T, LANES), jnp.int32), jax.ShapeDtypeStruct((T, LANES), F32),
                   jax.ShapeDtypeStruct((T, D // 2), U32)],
        compiler_params=_params(("arbitrary",)),
    )(*x1s, g.reshape(1, D).astype(F32), w_router, b_router)


def _expert_up_kernel(src_ref, te_ref, nused_ref, hp_hbm, wg_ref, wu_ref, a_ref, buf, xs, sem, *, tme):
    i = pl.program_id(0)
    n_used = nused_ref[0]
    half_d = xs.shape[1] // 2

    def row_copy(tile, slot, r):
        return pltpu.make_async_copy(hp_hbm.at[pl.ds(src_ref[tile * tme + r], 1), :],
                                     buf.at[slot, pl.ds(r, 1), :], sem.at[slot])

    def issue(tile, slot):
        def body(r, c):
            row_copy(tile, slot, r).start(priority=1)
            return c
        lax.fori_loop(0, tme, body, 0, unroll=8)

    @pl.when(i == 0)
    def _():
        issue(0, 0)

    @pl.when(i + 1 < n_used)
    def _():
        issue(i + 1, (i + 1) % 2)

    @pl.when(i < n_used)
    def _():
        slot = i % 2

        def drain(r, c):
            row_copy(i, slot, r).wait()
            return c
        lax.fori_loop(0, tme, drain, 0, unroll=8)
        lo, hi = _unpack_pairs(buf[slot])
        xs[:, :half_d] = lo.astype(xs.dtype)
        xs[:, half_d:] = hi.astype(xs.dtype)
        x = xs[...]
        gate = jnp.dot(x, wg_ref[0], preferred_element_type=F32)
        up = jnp.dot(x, wu_ref[0], preferred_element_type=F32)
        a_ref[...] = (jax.nn.silu(gate) * up).astype(a_ref.dtype)

    @pl.when(i >= n_used)
    def _():
        a_ref[...] = jnp.zeros_like(a_ref)


def _expert_down_kernel(dst_ref, te_ref, nused_ref, a_ref, wd_ref, yt_hbm, ybuf, zbuf, sem, zsem,
                        *, tme, n_split, n_real):
    i = pl.program_id(0)
    n_used = nused_ref[0]
    last = pl.num_programs(0) - 1
    half_d = ybuf.shape[2]
    wc = half_d // n_split

    def row_copy(tile, slot, r):
        return pltpu.make_async_copy(ybuf.at[slot, pl.ds(r, 1), :],
                                     yt_hbm.at[pl.ds(dst_ref[tile * tme + r], 1), :], sem.at[slot])

    def drain(tile):
        def body(r, c):
            row_copy(tile, tile % 2, r).wait()
            return c
        lax.fori_loop(0, tme, body, 0, unroll=8)

    @pl.when(i == 0)
    def _():
        zbuf[...] = jnp.zeros_like(zbuf)

        def zero_copy(r):
            return pltpu.make_async_copy(zbuf.at[pl.ds(0, 1), :], yt_hbm.at[pl.ds(n_real + r, 1), :], zsem)

        def start(r, c):
            zero_copy(r).start()
            return c

        def wait(r, c):
            zero_copy(r).wait()
            return c
        lax.fori_loop(0, tme, start, 0, unroll=8)
        lax.fori_loop(0, tme, wait, 0, unroll=8)

    @pl.when((i >= 2) & (i - 2 < n_used))
    def _():
        drain(i - 2)

    @pl.when(i < n_used)
    def _():
        slot = i % 2
        a = a_ref[...]
        for k in range(n_split):
            lo = jnp.dot(a, wd_ref[0, :, k * wc:(k + 1) * wc], preferred_element_type=F32)
            hi = jnp.dot(a, wd_ref[0, :, half_d + k * wc:half_d + (k + 1) * wc], preferred_element_type=F32)
            ybuf[slot, :, k * wc:(k + 1) * wc] = _pack_pairs(lo, hi)

        def body(r, c):
            row_copy(i, slot, r).start(priority=1)
            return c
        lax.fori_loop(0, tme, body, 0, unroll=8)

    @pl.when((i == last) & (i >= 1) & (i - 1 < n_used))
    def _():
        drain(i - 1)

    @pl.when((i == last) & (i < n_used))
    def _():
        drain(i)


def expert_ffn(hp, w_gate, w_up, w_down, src, dst, tile_expert, n_used, n_tiles, tme, n_rows_out):
    D = hp.shape[1] * 2
    F = w_gate.shape[2]
    P = n_tiles * tme
    act = pl.pallas_call(
        functools.partial(_expert_up_kernel, tme=tme),
        grid_spec=pltpu.PrefetchScalarGridSpec(
            num_scalar_prefetch=3,
            grid=(n_tiles,),
            in_specs=[pl.BlockSpec(memory_space=pl.ANY),
                      pl.BlockSpec((1, D, F), lambda i, s, te, n: (te[i], 0, 0)),
                      pl.BlockSpec((1, D, F), lambda i, s, te, n: (te[i], 0, 0))],
            out_specs=pl.BlockSpec((tme, F), lambda i, s, te, n: (i, 0)),
            scratch_shapes=[pltpu.VMEM((2, tme, D // 2), U32), pltpu.VMEM((tme, D), BF16),
                            pltpu.SemaphoreType.DMA((2,))],
        ),
        out_shape=jax.ShapeDtypeStruct((P, F), BF16),
        compiler_params=_params(("arbitrary",)),
    )(src, tile_expert, n_used, hp, w_gate, w_up)
    n_split = max(1, D // 1024)
    return pl.pallas_call(
        functools.partial(_expert_down_kernel, tme=tme, n_split=n_split, n_real=n_rows_out - tme),
        grid_spec=pltpu.PrefetchScalarGridSpec(
            num_scalar_prefetch=3,
            grid=(n_tiles,),
            in_specs=[pl.BlockSpec((tme, F), lambda i, d, te, n: (i, 0)),
                      pl.BlockSpec((1, F, D), lambda i, d, te, n: (te[i], 0, 0))],
            out_specs=pl.BlockSpec(memory_space=pl.ANY),
            scratch_shapes=[pltpu.VMEM((2, tme, D // 2), U32), pltpu.VMEM((8, D // 2), U32),
                            pltpu.SemaphoreType.DMA((2,)), pltpu.SemaphoreType.DMA],
        ),
        out_shape=jax.ShapeDtypeStruct((n_rows_out, D // 2), U32),
        compiler_params=_params(("arbitrary",)),
    )(dst, tile_expert, n_used, act, w_down)


def _final_kernel(x_ref, cw_ref, y0_ref, y1_ref, g_ref, o_ref):
    half_d = y0_ref.shape[1]
    cw0 = cw_ref[:, 0:1]
    cw1 = cw_ref[:, 1:2]
    lo0, hi0 = _unpack_pairs(y0_ref[...])
    lo1, hi1 = _unpack_pairs(y1_ref[...])
    xl = x_ref[:, :half_d] + (cw0 * lo0 + cw1 * lo1)
    xh = x_ref[:, half_d:] + (cw0 * hi0 + cw1 * hi1)
    ss = jnp.sum(xl * xl, axis=-1, keepdims=True) + jnp.sum(xh * xh, axis=-1, keepdims=True)
    inv = lax.rsqrt(ss / (2 * half_d) + RMS_EPS)
    o_ref[:, :half_d] = xl * inv * g_ref[:, :half_d]
    o_ref[:, half_d:] = xh * inv * g_ref[:, half_d:]


def final_combine(x1, cw, yt, g, tm, tile0, n_tok_tiles):
    T, D = x1.shape
    return pl.pallas_call(
        _final_kernel,
        grid=(T // tm,),
        in_specs=[pl.BlockSpec((tm, D), lambda i: (i, 0)),
                  pl.BlockSpec((tm, LANES), lambda i: (tile0 + i, 0)),
                  pl.BlockSpec((tm, D // 2), lambda i: (tile0 + i, 0)),
                  pl.BlockSpec((tm, D // 2), lambda i: (n_tok_tiles + tile0 + i, 0)),
                  pl.BlockSpec((1, D), lambda i: (0, 0))],
        out_specs=pl.BlockSpec((tm, D), lambda i: (i, 0)),
        out_shape=jax.ShapeDtypeStruct((T, D), F32),
        compiler_params=_params(("parallel",)),
    )(x1, cw, yt, yt, g.reshape(1, D).astype(F32))


def _sorted_layout(eid, tme):
    T = eid.shape[0]
    n_tiles = (2 * T) // tme + N_EXPERTS
    P = n_tiles * tme
    flat_e = eid.reshape(-1)
    onehot = (flat_e[:, None] == jnp.arange(N_EXPERTS, dtype=jnp.int32)[None, :]).astype(jnp.int32)
    csum = jnp.cumsum(onehot, axis=0)
    rank = jnp.take_along_axis(csum, flat_e[:, None], axis=1)[:, 0] - 1
    counts = csum[-1]
    padded = ((counts + tme - 1) // tme) * tme
    ends = jnp.cumsum(padded)
    pos = (ends - padded)[flat_e] + rank
    spare = 2 * T + jnp.arange(P, dtype=jnp.int32) % tme
    a = jnp.arange(2 * T, dtype=jnp.int32)
    dst = spare.at[pos].set((a % 2) * T + a // 2)
    src = jnp.where(dst < 2 * T, dst % T, 0)
    n_used = (ends[-1] // tme).astype(jnp.int32)
    tile_start = jnp.minimum(jnp.arange(n_tiles, dtype=jnp.int32), n_used - 1) * tme
    tile_expert = jnp.minimum(jnp.searchsorted(ends, tile_start, side="right"), N_EXPERTS - 1).astype(jnp.int32)
    return src, dst, tile_expert, n_used.reshape(1), n_tiles


def _encoder_trunk(x, wts):
    B, S, D = x.shape
    T = B * S
    hpg, cg = wts["hpg"], wts["cg"]
    n_grp = len(ATTN_PATTERNS)
    W = hpg * HEAD_DIM
    f_start = 3 * n_grp * W
    f_width = N_FOURIER_GROUPS * cg
    x2d = x.reshape(T, D)

    dils = sorted({d for _, d in ATTN_PATTERNS if d > 1})
    h, h_cm = rmsnorm_cast(x2d, wts["attn_norm_g"], B, S, dils)
    h_cm[1] = h

    outs, lses = [], []
    for g, (_, dil) in enumerate(ATTN_PATTERNS):
        qkv = in_projection(h_cm[dil], wts["w_in"], wts["b_full"], W, 3, lambda j, g=g: j * n_grp + g)
        o_g, lse_g = band_attention(qkv, B, S, g, hpg, n_grp * hpg)
        outs.append(o_g)
        lses.append(lse_g)

    tn = _tile(math.gcd(math.gcd(f_start, f_width), 2 * D), 1024)
    c0 = f_start // tn
    fg = in_projection(h, wts["w_in"], wts["b_full"], tn, (f_width + 2 * D) // tn, lambda j: c0 + j,
                       gate_tile0=f_width // tn)

    pc, ps = fourier_channel_stage(fg, cg, wts["chan_table"])
    cos_s, sin_s = _dft_tables(S)
    fr = fourier_sequence_stage(pc, ps, cos_s.astype(BF16), (-sin_s).astype(BF16), B, S,
                                float(1.0 / math.sqrt(S * cg)))

    merged = merge_branches(outs, lses, fr, wts["w_branch_attn"], wts["w_branch_fourier"], fg, f_width, hpg)
    return out_projection(merged, wts["w_out"], x2d)


def _moe_and_final_norm(x1s, wts):
    t_all = sum(x1.shape[0] for x1 in x1s)
    tm = _tile(math.gcd(*[x1.shape[0] for x1 in x1s]), 256, 8)
    eid, cw, hp = route(x1s, wts["ffn_norm_g"], wts["w_router"], wts["b_router"], tm)
    tme = _tile(2 * t_all, EXPERT_ROW_TILE, 8)
    src, dst, tile_expert, n_used, n_tiles = _sorted_layout(eid[:, :2], tme)
    yt = expert_ffn(hp, wts["w_expert_gate"], wts["w_expert_up"], wts["w_expert_down"], src, dst, tile_expert,
                    n_used, n_tiles, tme, 2 * t_all + tme)
    outs, tile0 = [], 0
    for x1 in x1s:
        outs.append(final_combine(x1, cw, yt, wts["final_norm_g"], tm, tile0, t_all // tm))
        tile0 += x1.shape[0] // tm
    return outs


def kernel(x_prompt, x_sample, attn_norm_g, w_in, w_branch_attn, w_branch_fourier, b_gate, w_out, ffn_norm_g, w_router_group, b_router_group, w_router_expert, b_router_expert, w_expert_gate, w_expert_up, w_expert_down, final_norm_g):
    assert w_in.shape[0] == 1, "the final norm is fused into the layer's last kernel: one layer only"
    l = 0
    D = x_prompt.shape[-1]
    hpg = w_branch_attn.shape[1] // HEAD_DIM
    cg = w_branch_fourier.shape[1] // N_FOURIER_GROUPS
    in_width = w_in.shape[2]
    n_gate = b_gate.shape[1]
    cos_c, sin_c = _dft_tables(cg)
    w_r = jnp.concatenate(
        [w_router_group[l], jnp.transpose(w_router_expert[l], (1, 0, 2)).reshape(D, N_EXPERTS)], axis=1)
    b_r = jnp.concatenate([b_router_group[l], b_router_expert[l].reshape(-1)])
    pad = LANES - w_r.shape[1]
    wts = dict(
        hpg=hpg, cg=cg, chan_table=jnp.concatenate([cos_c, sin_c], axis=1).astype(BF16),
        attn_norm_g=attn_norm_g[l], ffn_norm_g=ffn_norm_g[l], final_norm_g=final_norm_g,
        w_in=w_in[l].astype(BF16),
        b_full=jnp.concatenate([jnp.zeros((in_width - n_gate,), F32), b_gate[l].astype(F32)]).reshape(1, in_width),
        w_branch_attn=w_branch_attn[l].astype(BF16),
        w_branch_fourier=w_branch_fourier[l].astype(BF16),
        w_out=w_out[l].astype(BF16),
        w_router=jnp.pad(w_r, ((0, 0), (0, pad))).astype(BF16),
        b_router=jnp.pad(b_r, (0, pad)).reshape(1, LANES).astype(F32),
        w_expert_gate=w_expert_gate[l].astype(BF16),
        w_expert_up=w_expert_up[l].astype(BF16),
        w_expert_down=w_expert_down[l].astype(BF16),
    )
    xs = (x_prompt, x_sample)
    outs = _moe_and_final_norm([_encoder_trunk(x, wts) for x in xs], wts)
    return tuple(o.reshape(x.shape) for o, x in zip(outs, xs))
```

```python
import functools
import math

import numpy as np
import jax
import jax.numpy as jnp
from jax import lax
from jax.experimental import pallas as pl
from jax.experimental.pallas import tpu as pltpu

F32 = jnp.float32
BF16 = jnp.bfloat16
U32 = jnp.uint32

RMS_EPS = 1e-6
NEG_INF = -1e30
HEAD_DIM = 128
ATTN_PATTERNS = ((128, 1), (512, 4), (2048, 16))
N_FOURIER_GROUPS = 4
N_EXPERT_GROUPS = 4
EXPERTS_PER_GROUP = 4
N_EXPERTS = N_EXPERT_GROUPS * EXPERTS_PER_GROUP
LANES = 128
V7X_VMEM_LIMIT = 56 * 1024 * 1024
ATTN_Q_TILE = 128
ATTN_ROW_TILE = 1024
EXPERT_ROW_TILE = 512
HI_MASK = 0xFFFF0000


def _tile(n, pref, mult=LANES):
    if n <= pref:
        return n
    t = (pref // mult) * mult
    while t >= mult:
        if n % t == 0:
            return t
        t -= mult
    raise ValueError(f"no tile for {n} <= {pref}")


def _params(sem, vmem=V7X_VMEM_LIMIT):
    return pltpu.CompilerParams(dimension_semantics=sem, vmem_limit_bytes=vmem)


def _pack_pairs(lo, hi):
    a = pltpu.bitcast(lo.astype(BF16).astype(F32), U32) >> 16
    b = pltpu.bitcast(hi.astype(BF16).astype(F32), U32) & jnp.uint32(HI_MASK)
    return a | b


def _unpack_pairs(w):
    return pltpu.bitcast(w << 16, F32), pltpu.bitcast(w & jnp.uint32(HI_MASK), F32)


def _rms_kernel(x_ref, g_ref, *refs, dils):
    o_ref = refs[0]
    cm_refs = refs[1:1 + len(dils)]
    scr = refs[-1] if dils else None
    x = x_ref[...]
    ms = jnp.mean(x * x, axis=-1, keepdims=True)
    h = x * lax.rsqrt(ms + RMS_EPS) * g_ref[...]
    o_ref[...] = h.astype(o_ref.dtype)
    if not dils:
        return
    tm = x.shape[0]
    nch = x.shape[1] // LANES
    for c in range(nch):
        scr[c] = h[:, c * LANES:(c + 1) * LANES]
    for cm_ref, d in zip(cm_refs, dils):
        for r in range(d):
            for c in range(nch):
                cm_ref[0, r, :, c * LANES:(c + 1) * LANES] = scr[c, pl.ds(r, tm // d, stride=d), :].astype(cm_ref.dtype)


def rmsnorm_cast(x, g, B, S, dils):
    T, D = x.shape
    tm = _tile(S, 256, 16 * max(dils, default=1))
    spb = S // tm
    outs = pl.pallas_call(
        functools.partial(_rms_kernel, dils=tuple(dils)),
        grid=(T // tm,),
        in_specs=[pl.BlockSpec((tm, D), lambda i: (i, 0)), pl.BlockSpec((1, D), lambda i: (0, 0))],
        out_specs=[pl.BlockSpec((tm, D), lambda i: (i, 0))]
        + [pl.BlockSpec((1, d, tm // d, D), lambda i: (i // spb, 0, i % spb, 0)) for d in dils],
        out_shape=[jax.ShapeDtypeStruct((T, D), BF16)]
        + [jax.ShapeDtypeStruct((B, d, S // d, D), BF16) for d in dils],
        scratch_shapes=[pltpu.VMEM((D // LANES, tm, LANES), F32)] if dils else [],
        compiler_params=_params(("parallel",)),
    )(x, g.reshape(1, D).astype(F32))
    return outs[0], {d: o.reshape(T, D) for d, o in zip(dils, outs[1:])}


def _in_proj_kernel(h_ref, w_ref, b_ref, o_ref, *, gate_tile0):
    acc = jnp.dot(h_ref[...], w_ref[...], preferred_element_type=F32)
    if gate_tile0 is None:
        o_ref[...] = acc.astype(o_ref.dtype)
        return
    j = pl.program_id(1)

    @pl.when(j < gate_tile0)
    def _():
        o_ref[...] = acc.astype(o_ref.dtype)

    @pl.when(j >= gate_tile0)
    def _():
        o_ref[...] = jax.nn.sigmoid(acc + b_ref[...]).astype(o_ref.dtype)


def in_projection(h, w, b_full, tn, n_out, w_col, gate_tile0=None):
    T, K = h.shape
    tm = _tile(T, 1024)
    return pl.pallas_call(
        functools.partial(_in_proj_kernel, gate_tile0=gate_tile0),
        grid=(T // tm, n_out),
        in_specs=[pl.BlockSpec((tm, K), lambda i, j: (i, 0)),
                  pl.BlockSpec((K, tn), lambda i, j: (0, w_col(j))),
                  pl.BlockSpec((1, tn), lambda i, j: (0, w_col(j)))],
        out_specs=pl.BlockSpec((tm, tn), lambda i, j: (i, j)),
        out_shape=jax.ShapeDtypeStruct((T, n_out * tn), BF16),
        compiler_params=_params(("parallel", "arbitrary")),
    )(h, w, b_full)


def _attn_kernel(q_ref, kp_ref, kc_ref, kn_ref, vp_ref, vc_ref, vn_ref, o_ref, lse_ref, so, sl,
                 *, tq, sub, half, hps, dil, cls_len, slopes):
    i = pl.program_id(1)
    hh = pl.program_id(2)
    key_pad = -(sub + 2 * half) % LANES
    span = sub + 2 * half + key_pad
    scale = HEAD_DIM ** -0.5
    qidx = lax.broadcasted_iota(jnp.int32, (sub, span), 0)
    kidx = lax.broadcasted_iota(jnp.int32, (sub, span), 1)
    absrel = jnp.abs(kidx - half - qidx)
    absrel_f = absrel.astype(F32)
    lane = lax.broadcasted_iota(jnp.int32, (sub, LANES), 1)
    n_sub = tq // sub
    valids = []
    for u in range(n_sub):
        kpos = i * tq + (u * sub - half) + kidx
        valids.append((absrel <= half) & (kpos >= 0) & (kpos < cls_len))
    neg_slopes = []
    for h in range(hps):
        s_h = jnp.float32(0.0)
        for b in range(len(slopes) // hps):
            s_h = jnp.where(hh == b, jnp.float32(-slopes[b * hps + h] * dil), s_h)
        neg_slopes.append(s_h)

    @pl.when(hh == 0)
    def _():
        sl[...] = jnp.zeros_like(sl)

    def keys(prev_ref, cur_ref, next_ref, r, u, cols):
        lo = u * sub - half if u > 0 else 0
        hi = (u + 1) * sub + half if u < n_sub - 1 else tq
        parts = [cur_ref[0, r, lo:hi, cols]]
        if u == 0:
            parts.insert(0, prev_ref[0, r, :, cols])
        if u == n_sub - 1:
            parts.append(next_ref[0, r, :, cols])
        if key_pad:
            parts.append(jnp.zeros((key_pad, HEAD_DIM), cur_ref.dtype))
        return parts[0] if len(parts) == 1 else jnp.concatenate(parts, axis=0)

    for r in range(dil):
        for u in range(n_sub):
            rows = pl.ds(u * sub * dil + r, sub, stride=dil) if dil > 1 else pl.ds(u * sub, sub)
            lse_tile = sl[rows, :]
            for h in range(hps):
                cols = slice(h * HEAD_DIM, (h + 1) * HEAD_DIM)
                q = q_ref[0, r, u * sub:(u + 1) * sub, cols]
                k = keys(kp_ref, kc_ref, kn_ref, r, u, cols)
                v = keys(vp_ref, vc_ref, vn_ref, r, u, cols)
                s = lax.dot_general(q, k, (((1,), (1,)), ((), ())), preferred_element_type=F32)
                s = s * scale + absrel_f * neg_slopes[h]
                s = jnp.where(valids[u], s, NEG_INF)
                m = jnp.max(s, axis=-1, keepdims=True)
                p = jnp.exp(s - m)
                den = jnp.sum(p, axis=-1, keepdims=True)
                so[h, rows, :] = jnp.dot(p.astype(v.dtype), v, preferred_element_type=F32) / den
                lse_tile = jnp.where(lane == hh * hps + h, m + jnp.log(den), lse_tile)
            sl[rows, :] = lse_tile
    for h in range(hps):
        o_ref[0, :, h * HEAD_DIM:(h + 1) * HEAD_DIM] = so[h].astype(o_ref.dtype)
    lse_ref[0] = sl[...]


def band_attention(qkv, B, S, group, hpg, n_heads_total):
    window, dil = ATTN_PATTERNS[group]
    half = window // (2 * dil)
    W = hpg * HEAD_DIM
    L = S // dil
    tq = min(L, max(ATTN_Q_TILE, ATTN_ROW_TILE // dil))
    rows = tq * dil
    sub = min(ATTN_Q_TILE, tq)
    hps = max(1, hpg * ATTN_ROW_TILE // max(rows, ATTN_ROW_TILE))
    assert S % dil == 0 and L % tq == 0 and tq % sub == 0 and tq % half == 0 and half % 16 == 0 and hpg % hps == 0
    n_hh = hpg // hps
    Wb = hps * HEAD_DIM
    hb = tq // half
    n_hb = L // half
    s_all = 2.0 ** (-8.0 * np.arange(1, n_heads_total + 1) / n_heads_total)
    slopes = tuple(float(np.float32(v)) for v in s_all[group * hpg:(group + 1) * hpg])
    qv = qkv.reshape(B, dil, L, 3 * W)

    cur = lambda part: pl.BlockSpec((1, dil, tq, Wb), lambda b, i, hh: (b, 0, i, part * n_hh + hh))
    prv = lambda part: pl.BlockSpec(
        (1, dil, half, Wb), lambda b, i, hh: (b, 0, jnp.maximum(i * hb - 1, 0), part * n_hh + hh))
    nxt = lambda part: pl.BlockSpec(
        (1, dil, half, Wb), lambda b, i, hh: (b, 0, jnp.minimum((i + 1) * hb, n_hb - 1), part * n_hh + hh))
    o, lse = pl.pallas_call(
        functools.partial(_attn_kernel, tq=tq, sub=sub, half=half, hps=hps, dil=dil, cls_len=L, slopes=slopes),
        grid=(B, L // tq, n_hh),
        in_specs=[cur(0), prv(1), cur(1), nxt(1), prv(2), cur(2), nxt(2)],
        out_specs=[pl.BlockSpec((1, rows, Wb), lambda b, i, hh: (b, i, hh)),
                   pl.BlockSpec((1, rows, LANES), lambda b, i, hh: (b, i, 0))],
        out_shape=[jax.ShapeDtypeStruct((B, S, W), BF16), jax.ShapeDtypeStruct((B, S, LANES), F32)],
        scratch_shapes=[pltpu.VMEM((hps, rows, LANES), F32), pltpu.VMEM((rows, LANES), F32)],
        compiler_params=_params(("parallel", "arbitrary", "arbitrary")),
    )(qv, qv, qv, qv, qv, qv, qv)
    return o.reshape(B * S, W), lse.reshape(B * S, LANES)


def _dft_tables(n):
    k = jnp.arange(n, dtype=jnp.int32)

    def thin(cols):
        ang = ((k[:, None] * cols[None, :]) % n).astype(F32) * np.float32(2.0 * np.pi / n)
        return jnp.cos(ang), jnp.sin(ang)

    m = 64
    if n <= m or n % m:
        return thin(k)
    c_hi, s_hi = thin(jnp.arange(n // m, dtype=jnp.int32) * m)
    c_lo, s_lo = thin(jnp.arange(m, dtype=jnp.int32))
    cos = c_hi[:, :, None] * c_lo[:, None, :] - s_hi[:, :, None] * s_lo[:, None, :]
    sin = s_hi[:, :, None] * c_lo[:, None, :] + c_hi[:, :, None] * s_lo[:, None, :]
    return cos.reshape(n, n), sin.reshape(n, n)


def _fourier_chan_kernel(x_ref, t_ref, pc_ref, ps_ref):
    r = jnp.dot(x_ref[...], t_ref[...], preferred_element_type=F32)
    c = pc_ref.shape[-1]
    pc_ref[...] = r[:, :c].astype(pc_ref.dtype)
    ps_ref[...] = r[:, c:].astype(ps_ref.dtype)


def fourier_channel_stage(fg, cg, table):
    T = fg.shape[0]
    tm = _tile(T, 1024)
    return pl.pallas_call(
        _fourier_chan_kernel,
        grid=(T // tm, N_FOURIER_GROUPS),
        in_specs=[pl.BlockSpec((tm, cg), lambda i, g: (i, g)),
                  pl.BlockSpec((cg, 2 * cg), lambda i, g: (0, 0))],
        out_specs=[pl.BlockSpec((tm, cg), lambda i, g: (i, g)), pl.BlockSpec((tm, cg), lambda i, g: (i, g))],
        out_shape=[jax.ShapeDtypeStruct((T, N_FOURIER_GROUPS * cg), BF16)] * 2,
        compiler_params=_params(("parallel", "arbitrary")),
    )(fg, table)


def _fourier_seq_kernel(cs_ref, ss_ref, pc_ref, ps_ref, o_ref, acc_ref, *, scale):
    k = pl.program_id(3)

    @pl.when(k == 0)
    def _():
        acc_ref[...] = jnp.zeros_like(acc_ref)

    acc_ref[...] += (jnp.dot(cs_ref[...], pc_ref[0], preferred_element_type=F32)
                     + jnp.dot(ss_ref[...], ps_ref[0], preferred_element_type=F32))

    @pl.when(k == pl.num_programs(3) - 1)
    def _():
        o_ref[0] = (acc_ref[...] * scale).astype(o_ref.dtype)


def fourier_sequence_stage(pc, ps, cos_s, neg_sin_s, B, S, scale):
    N = pc.shape[1]
    tm, tn, tk = _tile(S, 1024), _tile(N, 1024), _tile(S, 2048)
    pc3, ps3 = pc.reshape(B, S, N), ps.reshape(B, S, N)
    out = pl.pallas_call(
        functools.partial(_fourier_seq_kernel, scale=scale),
        grid=(B, S // tm, N // tn, S // tk),
        in_specs=[pl.BlockSpec((tm, tk), lambda b, i, j, k: (i, k)),
                  pl.BlockSpec((tm, tk), lambda b, i, j, k: (i, k)),
                  pl.BlockSpec((1, tk, tn), lambda b, i, j, k: (b, k, j)),
                  pl.BlockSpec((1, tk, tn), lambda b, i, j, k: (b, k, j))],
        out_specs=pl.BlockSpec((1, tm, tn), lambda b, i, j, k: (b, i, j)),
        out_shape=jax.ShapeDtypeStruct((B, S, N), BF16),
        scratch_shapes=[pltpu.VMEM((tm, tn), F32)],
        compiler_params=_params(("parallel", "parallel", "parallel", "arbitrary")),
    )(cos_s, neg_sin_s, pc3, ps3)
    return out.reshape(B * S, N)


def _merge_kernel(o0_ref, o1_ref, o2_ref, l0_ref, l1_ref, l2_ref, fr_ref, wa_ref, wf_ref, *refs, hpg, n_b):
    ga_refs, gf_refs = refs[:n_b], refs[n_b:2 * n_b]
    m_ref, oc_ref = refs[2 * n_b], refs[2 * n_b + 1]
    l0, l1, l2 = l0_ref[...], l1_ref[...], l2_ref[...]
    m = jnp.maximum(jnp.maximum(l0, l1), l2)
    e0, e1, e2 = jnp.exp(l0 - m), jnp.exp(l1 - m), jnp.exp(l2 - m)
    den = e0 + e1 + e2
    w0, w1, w2 = e0 / den, e1 / den, e2 / den
    for h in range(hpg):
        cols = slice(h * HEAD_DIM, (h + 1) * HEAD_DIM)
        oc = (w0[:, h:h + 1] * o0_ref[:, cols].astype(F32)
              + w1[:, h:h + 1] * o1_ref[:, cols].astype(F32)
              + w2[:, h:h + 1] * o2_ref[:, cols].astype(F32))
        oc_ref[:, cols] = oc.astype(oc_ref.dtype)
    oc_all = oc_ref[...]
    fr = fr_ref[...]
    bw = ga_refs[0].shape[1]
    for c in range(n_b):
        cols = slice(c * bw, (c + 1) * bw)
        attn = jnp.dot(oc_all, wa_ref[:, cols], preferred_element_type=F32)
        four = jnp.dot(fr, wf_ref[:, cols], preferred_element_type=F32)
        m_ref[:, cols] = (ga_refs[c][...].astype(F32) * attn + gf_refs[c][...].astype(F32) * four).astype(m_ref.dtype)


def merge_branches(os_, lses, fr, wa, wf, fg, gate_start, hpg):
    T, W = os_[0].shape
    D = wa.shape[1]
    FW = fr.shape[1]
    tm = _tile(T, 256)
    bw = math.gcd(D, gate_start)
    n_b, g0 = D // bw, gate_start // bw
    row = lambda w: pl.BlockSpec((tm, w), lambda i: (i, 0))
    resident = lambda r: pl.BlockSpec((r, D), lambda i: (0, 0), pipeline_mode=pl.Buffered(1))
    gate = lambda blk: pl.BlockSpec((tm, bw), lambda i: (i, blk))
    return pl.pallas_call(
        functools.partial(_merge_kernel, hpg=hpg, n_b=n_b),
        grid=(T // tm,),
        in_specs=[row(W), row(W), row(W), row(LANES), row(LANES), row(LANES), row(FW), resident(W), resident(FW)]
        + [gate(g0 + c) for c in range(2 * n_b)],
        out_specs=pl.BlockSpec((tm, D), lambda i: (i, 0)),
        out_shape=jax.ShapeDtypeStruct((T, D), BF16),
        scratch_shapes=[pltpu.VMEM((tm, W), BF16)],
        compiler_params=_params(("parallel",)),
    )(*os_, *lses, fr, wa, wf, *([fg] * (2 * n_b)))


def _out_proj_kernel(m_ref, w_ref, x_ref, o_ref):
    o_ref[...] = x_ref[...] + jnp.dot(m_ref[...], w_ref[...], preferred_element_type=F32)


def out_projection(merged, w, x):
    T, K = merged.shape
    N = w.shape[1]
    tm, tn = _tile(T, 1024), _tile(N, 512)
    return pl.pallas_call(
        _out_proj_kernel,
        grid=(T // tm, N // tn),
        in_specs=[pl.BlockSpec((tm, K), lambda i, j: (i, 0)),
                  pl.BlockSpec((K, tn), lambda i, j: (0, j)),
                  pl.BlockSpec((tm, tn), lambda i, j: (i, j))],
        out_specs=pl.BlockSpec((tm, tn), lambda i, j: (i, j)),
        out_shape=jax.ShapeDtypeStruct((T, N), F32),
        compiler_params=_params(("parallel", "arbitrary")),
    )(merged, w, x)


def _router_kernel(*refs, bounds):
    n_in = len(bounds) - 1
    i = pl.program_id(0)
    for k in range(n_in):
        @pl.when((i >= bounds[k]) & (i < bounds[k + 1]))
        def _(k=k):
            _route_tile(refs[k], *refs[n_in:])


def _route_tile(x_ref, g_ref, w_ref, b_ref, eid_ref, cw_ref, hp_ref):
    x = x_ref[...]
    ms = jnp.mean(x * x, axis=-1, keepdims=True)
    hf = x * lax.rsqrt(ms + RMS_EPS) * g_ref[...]
    h = hf.astype(BF16)
    half_d = hf.shape[1] // 2
    hp_ref[...] = _pack_pairs(hf[:, :half_d], hf[:, half_d:])
    lg = jnp.dot(h, w_ref[...], preferred_element_type=F32) + b_ref[...]
    lane = lax.broadcasted_iota(jnp.int32, lg.shape, 1)
    big = jnp.int32(LANES)
    in_grp = lane < N_EXPERT_GROUPS
    gl = jnp.where(in_grp, lg, -jnp.inf)
    gmax = jnp.max(gl, axis=-1, keepdims=True)
    gsel = jnp.min(jnp.where(gl == gmax, lane, big), axis=-1, keepdims=True)
    p_group = 1.0 / jnp.sum(jnp.where(in_grp, jnp.exp(gl - gmax), 0.0), axis=-1, keepdims=True)
    lo = N_EXPERT_GROUPS + gsel * EXPERTS_PER_GROUP
    in_sel = (lane >= lo) & (lane < lo + EXPERTS_PER_GROUP)
    el = jnp.where(in_sel, lg, -jnp.inf)
    t1 = jnp.max(el, axis=-1, keepdims=True)
    i1 = jnp.min(jnp.where(el == t1, lane, big), axis=-1, keepdims=True)
    el2 = jnp.where(lane == i1, -jnp.inf, el)
    t2 = jnp.max(el2, axis=-1, keepdims=True)
    i2 = jnp.min(jnp.where(el2 == t2, lane, big), axis=-1, keepdims=True)
    e21 = jnp.exp(t2 - t1)
    p1 = 1.0 / (1.0 + e21)
    p2 = e21 / (1.0 + e21)
    eid_ref[...] = jnp.where(lane == 0, i1 - N_EXPERT_GROUPS, jnp.where(lane == 1, i2 - N_EXPERT_GROUPS, 0))
    cw_ref[...] = jnp.where(lane == 0, p_group * p1, jnp.where(lane == 1, p_group * p2, 0.0))


def route(x1s, g, w_router, b_router, tm):
    D = x1s[0].shape[1]
    bounds = [0]
    for x1 in x1s:
        bounds.append(bounds[-1] + x1.shape[0] // tm)
    T = bounds[-1] * tm

    def x_spec(k):
        lo, n = bounds[k], bounds[k + 1] - bounds[k]
        return pl.BlockSpec((tm, D), lambda i: (jnp.clip(i - lo, 0, n - 1), 0))

    return pl.pallas_call(
        functools.partial(_router_kernel, bounds=tuple(bounds)),
        grid=(bounds[-1],),
        in_specs=[x_spec(k) for k in range(len(x1s))]
        + [pl.BlockSpec((1, D), lambda i: (0, 0)),
           pl.BlockSpec((D, LANES), lambda i: (0, 0)),
           pl.BlockSpec((1, LANES), lambda i: (0, 0))],
        out_specs=[pl.BlockSpec((tm, LANES), lambda i: (i, 0)), pl.BlockSpec((tm, LANES), lambda i: (i, 0)),
                   pl.BlockSpec((tm, D // 2), lambda i: (i, 0))],
        out_shape=[jax.ShapeDtypeStruct((T, LANES), jnp.int32), jax.ShapeDtypeStruct((T, LANES), F32),
                   jax.ShapeDtypeStruct((T, D // 2), U32)],
        compiler_params=_params(("arbitrary",)),
    )(*x1s, g.reshape(1, D).astype(F32), w_router, b_router)


def _expert_up_kernel(src_ref, te_ref, nused_ref, hp_hbm, wg_ref, wu_ref, a_ref, buf, xs, sem, *, tme):
    i = pl.program_id(0)
    n_used = nused_ref[0]
    half_d = xs.shape[1] // 2

    def row_copy(tile, slot, r):
        return pltpu.make_async_copy(hp_hbm.at[pl.ds(src_ref[tile * tme + r], 1), :],
                                     buf.at[slot, pl.ds(r, 1), :], sem.at[slot])

    def issue(tile, slot):
        def body(r2, c):
            row_copy(tile, slot, 2 * r2).start(priority=0)
            row_copy(tile, slot, 2 * r2 + 1).start(priority=1)
            return c
        lax.fori_loop(0, tme // 2, body, 0, unroll=4)

    @pl.when(i == 0)
    def _():
        issue(0, 0)

    @pl.when(i + 1 < n_used)
    def _():
        issue(i + 1, (i + 1) % 2)

    @pl.when(i < n_used)
    def _():
        slot = i % 2

        def drain(r, c):
            row_copy(i, slot, r).wait()
            return c
        lax.fori_loop(0, tme, drain, 0, unroll=8)
        lo, hi = _unpack_pairs(buf[slot])
        xs[:, :half_d] = lo.astype(xs.dtype)
        xs[:, half_d:] = hi.astype(xs.dtype)
        x = xs[...]
        gate = jnp.dot(x, wg_ref[0], preferred_element_type=F32)
        up = jnp.dot(x, wu_ref[0], preferred_element_type=F32)
        a_ref[...] = (jax.nn.silu(gate) * up).astype(a_ref.dtype)

    @pl.when(i >= n_used)
    def _():
        a_ref[...] = jnp.zeros_like(a_ref)


def _expert_down_kernel(dst_ref, te_ref, nused_ref, a_ref, wd_ref, yt_hbm, ybuf, zbuf, sem, zsem,
                        *, tme, n_split, n_real):
    i = pl.program_id(0)
    n_used = nused_ref[0]
    last = pl.num_programs(0) - 1
    half_d = ybuf.shape[2]
    wc = half_d // n_split

    def row_copy(tile, slot, r):
        return pltpu.make_async_copy(ybuf.at[slot, pl.ds(r, 1), :],
                                     yt_hbm.at[pl.ds(dst_ref[tile * tme + r], 1), :], sem.at[slot])

    def drain(tile):
        def body(r, c):
            row_copy(tile, tile % 2, r).wait()
            return c
        lax.fori_loop(0, tme, body, 0, unroll=8)

    @pl.when(i == 0)
    def _():
        zbuf[...] = jnp.zeros_like(zbuf)

        def zero_copy(r):
            return pltpu.make_async_copy(zbuf.at[pl.ds(0, 1), :], yt_hbm.at[pl.ds(n_real + r, 1), :], zsem)

        def start(r, c):
            zero_copy(r).start()
            return c

        def wait(r, c):
            zero_copy(r).wait()
            return c
        lax.fori_loop(0, tme, start, 0, unroll=8)
        lax.fori_loop(0, tme, wait, 0, unroll=8)

    @pl.when((i >= 2) & (i - 2 < n_used))
    def _():
        drain(i - 2)

    @pl.when(i < n_used)
    def _():
        slot = i % 2
        a = a_ref[...]
        for k in range(n_split):
            lo = jnp.dot(a, wd_ref[0, :, k * wc:(k + 1) * wc], preferred_element_type=F32)
            hi = jnp.dot(a, wd_ref[0, :, half_d + k * wc:half_d + (k + 1) * wc], preferred_element_type=F32)
            ybuf[slot, :, k * wc:(k + 1) * wc] = _pack_pairs(lo, hi)

        def body(r2, c):
            row_copy(i, slot, 2 * r2).start(priority=0)
            row_copy(i, slot, 2 * r2 + 1).start(priority=1)
            return c
        lax.fori_loop(0, tme // 2, body, 0, unroll=4)

    @pl.when((i == last) & (i >= 1) & (i - 1 < n_used))
    def _():
        drain(i - 1)

    @pl.when((i == last) & (i < n_used))
    def _():
        drain(i)


def expert_ffn(hp, w_gate, w_up, w_down, src, dst, tile_expert, n_used, n_tiles, tme, n_rows_out):
    D = hp.shape[1] * 2
    F = w_gate.shape[2]
    P = n_tiles * tme
    act = pl.pallas_call(
        functools.partial(_expert_up_kernel, tme=tme),
        grid_spec=pltpu.PrefetchScalarGridSpec(
            num_scalar_prefetch=3,
            grid=(n_tiles,),
            in_specs=[pl.BlockSpec(memory_space=pl.ANY),
                      pl.BlockSpec((1, D, F), lambda i, s, te, n: (te[i], 0, 0)),
                      pl.BlockSpec((1, D, F), lambda i, s, te, n: (te[i], 0, 0))],
            out_specs=pl.BlockSpec((tme, F), lambda i, s, te, n: (i, 0)),
            scratch_shapes=[pltpu.VMEM((2, tme, D // 2), U32), pltpu.VMEM((tme, D), BF16),
                            pltpu.SemaphoreType.DMA((2,))],
        ),
        out_shape=jax.ShapeDtypeStruct((P, F), BF16),
        compiler_params=_params(("arbitrary",)),
    )(src, tile_expert, n_used, hp, w_gate, w_up)
    n_split = max(1, D // 1024)
    return pl.pallas_call(
        functools.partial(_expert_down_kernel, tme=tme, n_split=n_split, n_real=n_rows_out - tme),
        grid_spec=pltpu.PrefetchScalarGridSpec(
            num_scalar_prefetch=3,
            grid=(n_tiles,),
            in_specs=[pl.BlockSpec((tme, F), lambda i, d, te, n: (i, 0)),
                      pl.BlockSpec((1, F, D), lambda i, d, te, n: (te[i], 0, 0))],
            out_specs=pl.BlockSpec(memory_space=pl.ANY),
            scratch_shapes=[pltpu.VMEM((2, tme, D // 2), U32), pltpu.VMEM((8, D // 2), U32),
                            pltpu.SemaphoreType.DMA((2,)), pltpu.SemaphoreType.DMA],
        ),
        out_shape=jax.ShapeDtypeStruct((n_rows_out, D // 2), U32),
        compiler_params=_params(("arbitrary",)),
    )(dst, tile_expert, n_used, act, w_down)


def _final_kernel(x_ref, cw_ref, y0_ref, y1_ref, g_ref, o_ref):
    half_d = y0_ref.shape[1]
    cw0 = cw_ref[:, 0:1]
    cw1 = cw_ref[:, 1:2]
    lo0, hi0 = _unpack_pairs(y0_ref[...])
    lo1, hi1 = _unpack_pairs(y1_ref[...])
    xl = x_ref[:, :half_d] + (cw0 * lo0 + cw1 * lo1)
    xh = x_ref[:, half_d:] + (cw0 * hi0 + cw1 * hi1)
    ss = jnp.sum(xl * xl, axis=-1, keepdims=True) + jnp.sum(xh * xh, axis=-1, keepdims=True)
    inv = lax.rsqrt(ss / (2 * half_d) + RMS_EPS)
    o_ref[:, :half_d] = xl * inv * g_ref[:, :half_d]
    o_ref[:, half_d:] = xh * inv * g_ref[:, half_d:]


def final_combine(x1, cw, yt, g, tm, tile0, n_tok_tiles):
    T, D = x1.shape
    return pl.pallas_call(
        _final_kernel,
        grid=(T // tm,),
        in_specs=[pl.BlockSpec((tm, D), lambda i: (i, 0)),
                  pl.BlockSpec((tm, LANES), lambda i: (tile0 + i, 0)),
                  pl.BlockSpec((tm, D // 2), lambda i: (tile0 + i, 0)),
                  pl.BlockSpec((tm, D // 2), lambda i: (n_tok_tiles + tile0 + i, 0)),
                  pl.BlockSpec((1, D), lambda i: (0, 0))],
        out_specs=pl.BlockSpec((tm, D), lambda i: (i, 0)),
        out_shape=jax.ShapeDtypeStruct((T, D), F32),
        compiler_params=_params(("parallel",)),
    )(x1, cw, yt, yt, g.reshape(1, D).astype(F32))


def _sorted_layout(eid, tme):
    T = eid.shape[0]
    n_tiles = (2 * T) // tme + N_EXPERTS
    P = n_tiles * tme
    flat_e = eid.reshape(-1)
    onehot = (flat_e[:, None] == jnp.arange(N_EXPERTS, dtype=jnp.int32)[None, :]).astype(jnp.int32)
    csum = jnp.cumsum(onehot, axis=0)
    rank = jnp.take_along_axis(csum, flat_e[:, None], axis=1)[:, 0] - 1
    counts = csum[-1]
    padded = ((counts + tme - 1) // tme) * tme
    ends = jnp.cumsum(padded)
    pos = (ends - padded)[flat_e] + rank
    spare = 2 * T + jnp.arange(P, dtype=jnp.int32) % tme
    a = jnp.arange(2 * T, dtype=jnp.int32)
    dst = spare.at[pos].set((a % 2) * T + a // 2)
    src = jnp.where(dst < 2 * T, dst % T, 0)
    n_used = (ends[-1] // tme).astype(jnp.int32)
    tile_start = jnp.minimum(jnp.arange(n_tiles, dtype=jnp.int32), n_used - 1) * tme
    tile_expert = jnp.minimum(jnp.searchsorted(ends, tile_start, side="right"), N_EXPERTS - 1).astype(jnp.int32)
    return src, dst, tile_expert, n_used.reshape(1), n_tiles


def _encoder_trunk(x, wts):
    B, S, D = x.shape
    T = B * S
    hpg, cg = wts["hpg"], wts["cg"]
    n_grp = len(ATTN_PATTERNS)
    W = hpg * HEAD_DIM
    f_start = 3 * n_grp * W
    f_width = N_FOURIER_GROUPS * cg
    x2d = x.reshape(T, D)

    dils = sorted({d for _, d in ATTN_PATTERNS if d > 1})
    h, h_cm = rmsnorm_cast(x2d, wts["attn_norm_g"], B, S, dils)
    h_cm[1] = h

    outs, lses = [], []
    for g, (_, dil) in enumerate(ATTN_PATTERNS):
        qkv = in_projection(h_cm[dil], wts["w_in"], wts["b_full"], W, 3, lambda j, g=g: j * n_grp + g)
        o_g, lse_g = band_attention(qkv, B, S, g, hpg, n_grp * hpg)
        outs.append(o_g)
        lses.append(lse_g)

    tn = _tile(math.gcd(math.gcd(f_start, f_width), 2 * D), 1024)
    c0 = f_start // tn
    fg = in_projection(h, wts["w_in"], wts["b_full"], tn, (f_width + 2 * D) // tn, lambda j: c0 + j,
                       gate_tile0=f_width // tn)

    pc, ps = fourier_channel_stage(fg, cg, wts["chan_table"])
    cos_s, sin_s = _dft_tables(S)
    fr = fourier_sequence_stage(pc, ps, cos_s.astype(BF16), (-sin_s).astype(BF16), B, S,
                                float(1.0 / math.sqrt(S * cg)))

    merged = merge_branches(outs, lses, fr, wts["w_branch_attn"], wts["w_branch_fourier"], fg, f_width, hpg)
    return out_projection(merged, wts["w_out"], x2d)


def _moe_and_final_norm(x1s, wts):
    t_all = sum(x1.shape[0] for x1 in x1s)
    tm = _tile(math.gcd(*[x1.shape[0] for x1 in x1s]), 256, 8)
    eid, cw, hp = route(x1s, wts["ffn_norm_g"], wts["w_router"], wts["b_router"], tm)
    tme = _tile(2 * t_all, EXPERT_ROW_TILE, 8)
    src, dst, tile_expert, n_used, n_tiles = _sorted_layout(eid[:, :2], tme)
    yt = expert_ffn(hp, wts["w_expert_gate"], wts["w_expert_up"], wts["w_expert_down"], src, dst, tile_expert,
                    n_used, n_tiles, tme, 2 * t_all + tme)
    outs, tile0 = [], 0
    for x1 in x1s:
        outs.append(final_combine(x1, cw, yt, wts["final_norm_g"], tm, tile0, t_all // tm))
        tile0 += x1.shape[0] // tm
    return outs


def kernel(x_prompt, x_sample, attn_norm_g, w_in, w_branch_attn, w_branch_fourier, b_gate, w_out, ffn_norm_g, w_router_group, b_router_group, w_router_expert, b_router_expert, w_expert_gate, w_expert_up, w_expert_down, final_norm_g):
    assert w_in.shape[0] == 1, "the final norm is fused into the layer's last kernel: one layer only"
    l = 0
    D = x_prompt.shape[-1]
    hpg = w_branch_attn.shape[1] // HEAD_DIM
    cg = w_branch_fourier.shape[1] // N_FOURIER_GROUPS
    in_width = w_in.shape[2]
    n_gate = b_gate.shape[1]
    cos_c, sin_c = _dft_tables(cg)
    w_r = jnp.concatenate(
        [w_router_group[l], jnp.transpose(w_router_expert[l], (1, 0, 2)).reshape(D, N_EXPERTS)], axis=1)
    b_r = jnp.concatenate([b_router_group[l], b_router_expert[l].reshape(-1)])
    pad = LANES - w_r.shape[1]
    wts = dict(
        hpg=hpg, cg=cg, chan_table=jnp.concatenate([cos_c, sin_c], axis=1).astype(BF16),
        attn_norm_g=attn_norm_g[l], ffn_norm_g=ffn_norm_g[l], final_norm_g=final_norm_g,
        w_in=w_in[l].astype(BF16),
        b_full=jnp.concatenate([jnp.zeros((in_width - n_gate,), F32), b_gate[l].astype(F32)]).reshape(1, in_width),
        w_branch_attn=w_branch_attn[l].astype(BF16),
        w_branch_fourier=w_branch_fourier[l].astype(BF16),
        w_out=w_out[l].astype(BF16),
        w_router=jnp.pad(w_r, ((0, 0), (0, pad))).astype(BF16),
        b_router=jnp.pad(b_r, (0, pad)).reshape(1, LANES).astype(F32),
        w_expert_gate=w_expert_gate[l].astype(BF16),
        w_expert_up=w_expert_up[l].astype(BF16),
        w_expert_down=w_expert_down[l].astype(BF16),
    )
    xs = (x_prompt, x_sample)
    outs = _moe_and_final_norm([_encoder_trunk(x, wts) for x in xs], wts)
    return tuple(o.reshape(x.shape) for o, x in zip(outs, xs))
```

```python
import functools
import math

import numpy as np
import jax
import jax.numpy as jnp
from jax import lax
from jax.experimental import pallas as pl
from jax.experimental.pallas import tpu as pltpu

F32 = jnp.float32
BF16 = jnp.bfloat16
U32 = jnp.uint32

RMS_EPS = 1e-6
NEG_INF = -1e30
HEAD_DIM = 128
ATTN_PATTERNS = ((128, 1), (512, 4), (2048, 16))
N_FOURIER_GROUPS = 4
N_EXPERT_GROUPS = 4
EXPERTS_PER_GROUP = 4
N_EXPERTS = N_EXPERT_GROUPS * EXPERTS_PER_GROUP
LANES = 128
V7X_VMEM_LIMIT = 56 * 1024 * 1024
ATTN_Q_TILE = 128
ATTN_ROW_TILE = 1024
EXPERT_ROW_TILE = 512
HI_MASK = 0xFFFF0000


def _tile(n, pref, mult=LANES):
    if n <= pref:
        return n
    t = (pref // mult) * mult
    while t >= mult:
        if n % t == 0:
            return t
        t -= mult
    raise ValueError(f"no tile for {n} <= {pref}")


def _params(sem, vmem=V7X_VMEM_LIMIT):
    return pltpu.CompilerParams(dimension_semantics=sem, vmem_limit_bytes=vmem)


def _pack_pairs(lo, hi):
    a = pltpu.bitcast(lo.astype(BF16).astype(F32), U32) >> 16
    b = pltpu.bitcast(hi.astype(BF16).astype(F32), U32) & jnp.uint32(HI_MASK)
    return a | b


def _unpack_pairs(w):
    return pltpu.bitcast(w << 16, F32), pltpu.bitcast(w & jnp.uint32(HI_MASK), F32)


def _rms_kernel(x_ref, g_ref, *refs, dils):
    o_ref = refs[0]
    cm_refs = refs[1:1 + len(dils)]
    scr = refs[-1] if dils else None
    x = x_ref[...]
    ms = jnp.mean(x * x, axis=-1, keepdims=True)
    h = x * lax.rsqrt(ms + RMS_EPS) * g_ref[...]
    o_ref[...] = h.astype(o_ref.dtype)
    if not dils:
        return
    tm = x.shape[0]
    nch = x.shape[1] // LANES
    for c in range(nch):
        scr[c] = h[:, c * LANES:(c + 1) * LANES]
    for cm_ref, d in zip(cm_refs, dils):
        for r in range(d):
            for c in range(nch):
                cm_ref[0, r, :, c * LANES:(c + 1) * LANES] = scr[c, pl.ds(r, tm // d, stride=d), :].astype(cm_ref.dtype)


def rmsnorm_cast(x, g, B, S, dils):
    T, D = x.shape
    tm = _tile(S, 256, 16 * max(dils, default=1))
    spb = S // tm
    outs = pl.pallas_call(
        functools.partial(_rms_kernel, dils=tuple(dils)),
        grid=(T // tm,),
        in_specs=[pl.BlockSpec((tm, D), lambda i: (i, 0)), pl.BlockSpec((1, D), lambda i: (0, 0))],
        out_specs=[pl.BlockSpec((tm, D), lambda i: (i, 0))]
        + [pl.BlockSpec((1, d, tm // d, D), lambda i: (i // spb, 0, i % spb, 0)) for d in dils],
        out_shape=[jax.ShapeDtypeStruct((T, D), BF16)]
        + [jax.ShapeDtypeStruct((B, d, S // d, D), BF16) for d in dils],
        scratch_shapes=[pltpu.VMEM((D // LANES, tm, LANES), F32)] if dils else [],
        compiler_params=_params(("parallel",)),
    )(x, g.reshape(1, D).astype(F32))
    return outs[0], {d: o.reshape(T, D) for d, o in zip(dils, outs[1:])}


def _in_proj_kernel(h_ref, w_ref, b_ref, o_ref, *, gate_tile0):
    acc = jnp.dot(h_ref[...], w_ref[...], preferred_element_type=F32)
    if gate_tile0 is None:
        o_ref[...] = acc.astype(o_ref.dtype)
        return
    j = pl.program_id(1)

    @pl.when(j < gate_tile0)
    def _():
        o_ref[...] = acc.astype(o_ref.dtype)

    @pl.when(j >= gate_tile0)
    def _():
        o_ref[...] = jax.nn.sigmoid(acc + b_ref[...]).astype(o_ref.dtype)


def in_projection(h, w, b_full, tn, n_out, w_col, gate_tile0=None):
    T, K = h.shape
    tm = _tile(T, 1024)
    return pl.pallas_call(
        functools.partial(_in_proj_kernel, gate_tile0=gate_tile0),
        grid=(T // tm, n_out),
        in_specs=[pl.BlockSpec((tm, K), lambda i, j: (i, 0)),
                  pl.BlockSpec((K, tn), lambda i, j: (0, w_col(j))),
                  pl.BlockSpec((1, tn), lambda i, j: (0, w_col(j)))],
        out_specs=pl.BlockSpec((tm, tn), lambda i, j: (i, j)),
        out_shape=jax.ShapeDtypeStruct((T, n_out * tn), BF16),
        compiler_params=_params(("parallel", "arbitrary")),
    )(h, w, b_full)


def _attn_kernel(q_ref, kp_ref, kc_ref, kn_ref, vp_ref, vc_ref, vn_ref, o_ref, lse_ref, so, sl,
                 *, tq, sub, half, hps, dil, cls_len, slopes):
    i = pl.program_id(1)
    hh = pl.program_id(2)
    key_pad = -(sub + 2 * half) % LANES
    span = sub + 2 * half + key_pad
    scale = HEAD_DIM ** -0.5
    qidx = lax.broadcasted_iota(jnp.int32, (sub, span), 0)
    kidx = lax.broadcasted_iota(jnp.int32, (sub, span), 1)
    absrel = jnp.abs(kidx - half - qidx)
    absrel_f = absrel.astype(F32)
    lane = lax.broadcasted_iota(jnp.int32, (sub, LANES), 1)
    n_sub = tq // sub
    valids = []
    for u in range(n_sub):
        kpos = i * tq + (u * sub - half) + kidx
        valids.append((absrel <= half) & (kpos >= 0) & (kpos < cls_len))
    neg_slopes = []
    for h in range(hps):
        s_h = jnp.float32(0.0)
        for b in range(len(slopes) // hps):
            s_h = jnp.where(hh == b, jnp.float32(-slopes[b * hps + h] * dil), s_h)
        neg_slopes.append(s_h)

    @pl.when(hh == 0)
    def _():
        sl[...] = jnp.zeros_like(sl)

    def keys(prev_ref, cur_ref, next_ref, r, u, cols):
        lo = u * sub - half if u > 0 else 0
        hi = (u + 1) * sub + half if u < n_sub - 1 else tq
        parts = [cur_ref[0, r, lo:hi, cols]]
        if u == 0:
            parts.insert(0, prev_ref[0, r, :, cols])
        if u == n_sub - 1:
            parts.append(next_ref[0, r, :, cols])
        if key_pad:
            parts.append(jnp.zeros((key_pad, HEAD_DIM), cur_ref.dtype))
        return parts[0] if len(parts) == 1 else jnp.concatenate(parts, axis=0)

    for r in range(dil):
        for u in range(n_sub):
            rows = pl.ds(u * sub * dil + r, sub, stride=dil) if dil > 1 else pl.ds(u * sub, sub)
            lse_tile = sl[rows, :]
            for h in range(hps):
                cols = slice(h * HEAD_DIM, (h + 1) * HEAD_DIM)
                q = q_ref[0, r, u * sub:(u + 1) * sub, cols]
                k = keys(kp_ref, kc_ref, kn_ref, r, u, cols)
                v = keys(vp_ref, vc_ref, vn_ref, r, u, cols)
                s = lax.dot_general(q, k, (((1,), (1,)), ((), ())), preferred_element_type=F32)
                s = s * scale + absrel_f * neg_slopes[h]
                s = jnp.where(valids[u], s, NEG_INF)
                m = jnp.max(s, axis=-1, keepdims=True)
                p = jnp.exp(s - m)
                den = jnp.sum(p, axis=-1, keepdims=True)
                so[h, rows, :] = jnp.dot(p.astype(v.dtype), v, preferred_element_type=F32) / den
                lse_tile = jnp.where(lane == hh * hps + h, m + jnp.log(den), lse_tile)
            sl[rows, :] = lse_tile
    for h in range(hps):
        o_ref[0, :, h * HEAD_DIM:(h + 1) * HEAD_DIM] = so[h].astype(o_ref.dtype)
    lse_ref[0] = sl[...]


def band_attention(qkv, B, S, group, hpg, n_heads_total):
    window, dil = ATTN_PATTERNS[group]
    half = window // (2 * dil)
    W = hpg * HEAD_DIM
    L = S // dil
    tq = min(L, max(ATTN_Q_TILE, ATTN_ROW_TILE // dil))
    rows = tq * dil
    sub = min(ATTN_Q_TILE, tq)
    hps = max(1, hpg * ATTN_ROW_TILE // max(rows, ATTN_ROW_TILE))
    assert S % dil == 0 and L % tq == 0 and tq % sub == 0 and tq % half == 0 and half % 16 == 0 and hpg % hps == 0
    n_hh = hpg // hps
    Wb = hps * HEAD_DIM
    hb = tq // half
    n_hb = L // half
    s_all = 2.0 ** (-8.0 * np.arange(1, n_heads_total + 1) / n_heads_total)
    slopes = tuple(float(np.float32(v)) for v in s_all[group * hpg:(group + 1) * hpg])
    qv = qkv.reshape(B, dil, L, 3 * W)

    cur = lambda part: pl.BlockSpec((1, dil, tq, Wb), lambda b, i, hh: (b, 0, i, part * n_hh + hh))
    prv = lambda part: pl.BlockSpec(
        (1, dil, half, Wb), lambda b, i, hh: (b, 0, jnp.maximum(i * hb - 1, 0), part * n_hh + hh))
    nxt = lambda part: pl.BlockSpec(
        (1, dil, half, Wb), lambda b, i, hh: (b, 0, jnp.minimum((i + 1) * hb, n_hb - 1), part * n_hh + hh))
    o, lse = pl.pallas_call(
        functools.partial(_attn_kernel, tq=tq, sub=sub, half=half, hps=hps, dil=dil, cls_len=L, slopes=slopes),
        grid=(B, L // tq, n_hh),
        in_specs=[cur(0), prv(1), cur(1), nxt(1), prv(2), cur(2), nxt(2)],
        out_specs=[pl.BlockSpec((1, rows, Wb), lambda b, i, hh: (b, i, hh)),
                   pl.BlockSpec((1, rows, LANES), lambda b, i, hh: (b, i, 0))],
        out_shape=[jax.ShapeDtypeStruct((B, S, W), BF16), jax.ShapeDtypeStruct((B, S, LANES), F32)],
        scratch_shapes=[pltpu.VMEM((hps, rows, LANES), F32), pltpu.VMEM((rows, LANES), F32)],
        compiler_params=_params(("parallel", "arbitrary", "arbitrary")),
    )(qv, qv, qv, qv, qv, qv, qv)
    return o.reshape(B * S, W), lse.reshape(B * S, LANES)


def _dft_tables(n):
    k = jnp.arange(n, dtype=jnp.int32)

    def thin(cols):
        ang = ((k[:, None] * cols[None, :]) % n).astype(F32) * np.float32(2.0 * np.pi / n)
        return jnp.cos(ang), jnp.sin(ang)

    m = 64
    if n <= m or n % m:
        return thin(k)
    c_hi, s_hi = thin(jnp.arange(n // m, dtype=jnp.int32) * m)
    c_lo, s_lo = thin(jnp.arange(m, dtype=jnp.int32))
    cos = c_hi[:, :, None] * c_lo[:, None, :] - s_hi[:, :, None] * s_lo[:, None, :]
    sin = s_hi[:, :, None] * c_lo[:, None, :] + c_hi[:, :, None] * s_lo[:, None, :]
    return cos.reshape(n, n), sin.reshape(n, n)


def _fourier_chan_kernel(x_ref, t_ref, pc_ref, ps_ref):
    r = jnp.dot(x_ref[...], t_ref[...], preferred_element_type=F32)
    c = pc_ref.shape[-1]
    pc_ref[...] = r[:, :c].astype(pc_ref.dtype)
    ps_ref[...] = r[:, c:].astype(ps_ref.dtype)


def fourier_channel_stage(fg, cg, table):
    T = fg.shape[0]
    tm = _tile(T, 1024)
    return pl.pallas_call(
        _fourier_chan_kernel,
        grid=(T // tm, N_FOURIER_GROUPS),
        in_specs=[pl.BlockSpec((tm, cg), lambda i, g: (i, g)),
                  pl.BlockSpec((cg, 2 * cg), lambda i, g: (0, 0))],
        out_specs=[pl.BlockSpec((tm, cg), lambda i, g: (i, g)), pl.BlockSpec((tm, cg), lambda i, g: (i, g))],
        out_shape=[jax.ShapeDtypeStruct((T, N_FOURIER_GROUPS * cg), BF16)] * 2,
        compiler_params=_params(("parallel", "arbitrary")),
    )(fg, table)


def _fourier_seq_kernel(cs_ref, ss_ref, pc_ref, ps_ref, o_ref, acc_ref, *, scale):
    k = pl.program_id(3)

    @pl.when(k == 0)
    def _():
        acc_ref[...] = jnp.zeros_like(acc_ref)

    acc_ref[...] += (jnp.dot(cs_ref[...], pc_ref[0], preferred_element_type=F32)
                     + jnp.dot(ss_ref[...], ps_ref[0], preferred_element_type=F32))

    @pl.when(k == pl.num_programs(3) - 1)
    def _():
        o_ref[0] = (acc_ref[...] * scale).astype(o_ref.dtype)


def fourier_sequence_stage(pc, ps, cos_s, neg_sin_s, B, S, scale):
    N = pc.shape[1]
    tm, tn, tk = _tile(S, 1024), _tile(N, 1024), _tile(S, 2048)
    pc3, ps3 = pc.reshape(B, S, N), ps.reshape(B, S, N)
    out = pl.pallas_call(
        functools.partial(_fourier_seq_kernel, scale=scale),
        grid=(B, S // tm, N // tn, S // tk),
        in_specs=[pl.BlockSpec((tm, tk), lambda b, i, j, k: (i, k)),
                  pl.BlockSpec((tm, tk), lambda b, i, j, k: (i, k)),
                  pl.BlockSpec((1, tk, tn), lambda b, i, j, k: (b, k, j)),
                  pl.BlockSpec((1, tk, tn), lambda b, i, j, k: (b, k, j))],
        out_specs=pl.BlockSpec((1, tm, tn), lambda b, i, j, k: (b, i, j)),
        out_shape=jax.ShapeDtypeStruct((B, S, N), BF16),
        scratch_shapes=[pltpu.VMEM((tm, tn), F32)],
        compiler_params=_params(("parallel", "parallel", "parallel", "arbitrary")),
    )(cos_s, neg_sin_s, pc3, ps3)
    return out.reshape(B * S, N)


def _merge_kernel(o0_ref, o1_ref, o2_ref, l0_ref, l1_ref, l2_ref, fr_ref, wa_ref, wf_ref, *refs, hpg, n_b):
    ga_refs, gf_refs = refs[:n_b], refs[n_b:2 * n_b]
    m_ref, oc_ref = refs[2 * n_b], refs[2 * n_b + 1]
    l0, l1, l2 = l0_ref[...], l1_ref[...], l2_ref[...]
    m = jnp.maximum(jnp.maximum(l0, l1), l2)
    e0, e1, e2 = jnp.exp(l0 - m), jnp.exp(l1 - m), jnp.exp(l2 - m)
    den = e0 + e1 + e2
    w0, w1, w2 = e0 / den, e1 / den, e2 / den
    for h in range(hpg):
        cols = slice(h * HEAD_DIM, (h + 1) * HEAD_DIM)
        oc = (w0[:, h:h + 1] * o0_ref[:, cols].astype(F32)
              + w1[:, h:h + 1] * o1_ref[:, cols].astype(F32)
              + w2[:, h:h + 1] * o2_ref[:, cols].astype(F32))
        oc_ref[:, cols] = oc.astype(oc_ref.dtype)
    oc_all = oc_ref[...]
    fr = fr_ref[...]
    bw = ga_refs[0].shape[1]
    for c in range(n_b):
        cols = slice(c * bw, (c + 1) * bw)
        attn = jnp.dot(oc_all, wa_ref[:, cols], preferred_element_type=F32)
        four = jnp.dot(fr, wf_ref[:, cols], preferred_element_type=F32)
        m_ref[:, cols] = (ga_refs[c][...].astype(F32) * attn + gf_refs[c][...].astype(F32) * four).astype(m_ref.dtype)


def merge_branches(os_, lses, fr, wa, wf, fg, gate_start, hpg):
    T, W = os_[0].shape
    D = wa.shape[1]
    FW = fr.shape[1]
    tm = _tile(T, 256)
    bw = math.gcd(D, gate_start)
    n_b, g0 = D // bw, gate_start // bw
    row = lambda w: pl.BlockSpec((tm, w), lambda i: (i, 0))
    resident = lambda r: pl.BlockSpec((r, D), lambda i: (0, 0), pipeline_mode=pl.Buffered(1))
    gate = lambda blk: pl.BlockSpec((tm, bw), lambda i: (i, blk))
    return pl.pallas_call(
        functools.partial(_merge_kernel, hpg=hpg, n_b=n_b),
        grid=(T // tm,),
        in_specs=[row(W), row(W), row(W), row(LANES), row(LANES), row(LANES), row(FW), resident(W), resident(FW)]
        + [gate(g0 + c) for c in range(2 * n_b)],
        out_specs=pl.BlockSpec((tm, D), lambda i: (i, 0)),
        out_shape=jax.ShapeDtypeStruct((T, D), BF16),
        scratch_shapes=[pltpu.VMEM((tm, W), BF16)],
        compiler_params=_params(("parallel",)),
    )(*os_, *lses, fr, wa, wf, *([fg] * (2 * n_b)))


def _out_proj_kernel(m_ref, w_ref, x_ref, o_ref):
    o_ref[...] = x_ref[...] + jnp.dot(m_ref[...], w_ref[...], preferred_element_type=F32)


def out_projection(merged, w, x):
    T, K = merged.shape
    N = w.shape[1]
    tm, tn = _tile(T, 1024), _tile(N, 512)
    return pl.pallas_call(
        _out_proj_kernel,
        grid=(T // tm, N // tn),
        in_specs=[pl.BlockSpec((tm, K), lambda i, j: (i, 0)),
                  pl.BlockSpec((K, tn), lambda i, j: (0, j)),
                  pl.BlockSpec((tm, tn), lambda i, j: (i, j))],
        out_specs=pl.BlockSpec((tm, tn), lambda i, j: (i, j)),
        out_shape=jax.ShapeDtypeStruct((T, N), F32),
        compiler_params=_params(("parallel", "arbitrary")),
    )(merged, w, x)


def _router_kernel(*refs, bounds):
    n_in = len(bounds) - 1
    i = pl.program_id(0)
    for k in range(n_in):
        @pl.when((i >= bounds[k]) & (i < bounds[k + 1]))
        def _(k=k):
            _route_tile(refs[k], *refs[n_in:])


def _route_tile(x_ref, g_ref, w_ref, b_ref, eid_ref, cw_ref, hp_ref):
    x = x_ref[...]
    ms = jnp.mean(x * x, axis=-1, keepdims=True)
    hf = x * lax.rsqrt(ms + RMS_EPS) * g_ref[...]
    h = hf.astype(BF16)
    half_d = hf.shape[1] // 2
    hp_ref[...] = _pack_pairs(hf[:, :half_d], hf[:, half_d:])
    lg = jnp.dot(h, w_ref[...], preferred_element_type=F32) + b_ref[...]
    lane = lax.broadcasted_iota(jnp.int32, lg.shape, 1)
    big = jnp.int32(LANES)
    in_grp = lane < N_EXPERT_GROUPS
    gl = jnp.where(in_grp, lg, -jnp.inf)
    gmax = jnp.max(gl, axis=-1, keepdims=True)
    gsel = jnp.min(jnp.where(gl == gmax, lane, big), axis=-1, keepdims=True)
    p_group = 1.0 / jnp.sum(jnp.where(in_grp, jnp.exp(gl - gmax), 0.0), axis=-1, keepdims=True)
    lo = N_EXPERT_GROUPS + gsel * EXPERTS_PER_GROUP
    in_sel = (lane >= lo) & (lane < lo + EXPERTS_PER_GROUP)
    el = jnp.where(in_sel, lg, -jnp.inf)
    t1 = jnp.max(el, axis=-1, keepdims=True)
    i1 = jnp.min(jnp.where(el == t1, lane, big), axis=-1, keepdims=True)
    el2 = jnp.where(lane == i1, -jnp.inf, el)
    t2 = jnp.max(el2, axis=-1, keepdims=True)
    i2 = jnp.min(jnp.where(el2 == t2, lane, big), axis=-1, keepdims=True)
    e21 = jnp.exp(t2 - t1)
    p1 = 1.0 / (1.0 + e21)
    p2 = e21 / (1.0 + e21)
    eid_ref[...] = jnp.where(lane == 0, i1 - N_EXPERT_GROUPS, jnp.where(lane == 1, i2 - N_EXPERT_GROUPS, 0))
    cw_ref[...] = jnp.where(lane == 0, p_group * p1, jnp.where(lane == 1, p_group * p2, 0.0))


def route(x1s, g, w_router, b_router, tm):
    D = x1s[0].shape[1]
    bounds = [0]
    for x1 in x1s:
        bounds.append(bounds[-1] + x1.shape[0] // tm)
    T = bounds[-1] * tm

    def x_spec(k):
        lo, n = bounds[k], bounds[k + 1] - bounds[k]
        return pl.BlockSpec((tm, D), lambda i: (jnp.clip(i - lo, 0, n - 1), 0))

    return pl.pallas_call(
        functools.partial(_router_kernel, bounds=tuple(bounds)),
        grid=(bounds[-1],),
        in_specs=[x_spec(k) for k in range(len(x1s))]
        + [pl.BlockSpec((1, D), lambda i: (0, 0)),
           pl.BlockSpec((D, LANES), lambda i: (0, 0)),
           pl.BlockSpec((1, LANES), lambda i: (0, 0))],
        out_specs=[pl.BlockSpec((tm, LANES), lambda i: (i, 0)), pl.BlockSpec((tm, LANES), lambda i: (i, 0)),
                   pl.BlockSpec((tm, D // 2), lambda i: (i, 0))],
        out_shape=[jax.ShapeDtypeStruct((T, LANES), jnp.int32), jax.ShapeDtypeStruct((T, LANES), F32),
                   jax.ShapeDtypeStruct((T, D // 2), U32)],
        compiler_params=_params(("arbitrary",)),
    )(*x1s, g.reshape(1, D).astype(F32), w_router, b_router)


def _expert_up_kernel(src_ref, te_ref, nused_ref, hp_hbm, wg_ref, wu_ref, a_ref, buf, xs, sem, *, tme):
    i = pl.program_id(0)
    n_used = nused_ref[0]
    half_d = xs.shape[1] // 2

    def row_copy(tile, slot, r):
        return pltpu.make_async_copy(hp_hbm.at[pl.ds(src_ref[tile * tme + r], 1), :],
                                     buf.at[slot, pl.ds(r, 1), :], sem.at[slot])

    def drain(tile, slot):
        def body(r, c):
            row_copy(tile, slot, r).wait()
            return c
        lax.fori_loop(0, tme, body, 0, unroll=8)

    @pl.when(i == 0)
    def _():
        def body(r, c):
            row_copy(0, 0, r).start()
            return c
        lax.fori_loop(0, tme, body, 0, unroll=8)

    @pl.when(i < n_used)
    def _():
        slot = i % 2
        drain(i, slot)
        lo, hi = _unpack_pairs(buf[slot])
        xs[:, :half_d] = lo.astype(xs.dtype)
        xs[:, half_d:] = hi.astype(xs.dtype)
        nxt = jnp.minimum(i + 1, n_used - 1)
        for r in range(tme):
            row_copy(nxt, 1 - slot, r).start(priority=r % 2)
        x = xs[...]
        gate = jnp.dot(x, wg_ref[0], preferred_element_type=F32)
        up = jnp.dot(x, wu_ref[0], preferred_element_type=F32)
        a_ref[...] = (jax.nn.silu(gate) * up).astype(a_ref.dtype)

    @pl.when(i == n_used - 1)
    def _():
        drain(i, 1 - i % 2)

    @pl.when(i >= n_used)
    def _():
        a_ref[...] = jnp.zeros_like(a_ref)


def _expert_down_kernel(dst_ref, te_ref, nused_ref, a_ref, wd_ref, yt_hbm, ybuf, sem, *, tme, n_split, n_tiles):
    i = pl.program_id(0)
    n_used = nused_ref[0]
    last = n_tiles - 1
    half_d = ybuf.shape[2]
    wc = half_d // n_split

    def row_copy(tile, r):
        slot = (tile + 3) % 3
        return pltpu.make_async_copy(ybuf.at[slot, pl.ds(r, 1), :],
                                     yt_hbm.at[pl.ds(dst_ref[(tile + 1) * tme + r], 1), :], sem.at[slot])

    def drain(tile):
        def body(r, c):
            row_copy(tile, r).wait()
            return c
        lax.fori_loop(0, tme, body, 0, unroll=8)

    @pl.when(i == 0)
    def _():
        ybuf[2] = jnp.zeros(ybuf.shape[1:], ybuf.dtype)

    @pl.when((i >= 2) & (i - 3 < n_used))
    def _():
        drain(i - 3)

    @pl.when(i < n_used)
    def _():
        a = a_ref[...]
        for r in range(tme):
            row_copy(i - 1, r).start(priority=r % 2)
        for k in range(n_split):
            lo = jnp.dot(a, wd_ref[0, :, k * wc:(k + 1) * wc], preferred_element_type=F32)
            hi = jnp.dot(a, wd_ref[0, :, half_d + k * wc:half_d + (k + 1) * wc], preferred_element_type=F32)
            ybuf[i % 3, :, k * wc:(k + 1) * wc] = _pack_pairs(lo, hi)

    @pl.when(i == n_used)
    def _():
        def body(r, c):
            row_copy(i - 1, r).start()
            return c
        lax.fori_loop(0, tme, body, 0, unroll=8)

    @pl.when((i == last) & (last - 2 < n_used))
    def _():
        drain(last - 2)

    @pl.when((i == last) & (last - 1 < n_used))
    def _():
        drain(last - 1)


def expert_ffn(hp, w_gate, w_up, w_down, src, dst, tile_expert, n_used, n_tiles, tme, n_rows_out):
    D = hp.shape[1] * 2
    F = w_gate.shape[2]
    P = n_tiles * tme
    act = pl.pallas_call(
        functools.partial(_expert_up_kernel, tme=tme),
        grid_spec=pltpu.PrefetchScalarGridSpec(
            num_scalar_prefetch=3,
            grid=(n_tiles,),
            in_specs=[pl.BlockSpec(memory_space=pl.ANY),
                      pl.BlockSpec((1, D, F), lambda i, s, te, n: (te[i], 0, 0)),
                      pl.BlockSpec((1, D, F), lambda i, s, te, n: (te[i], 0, 0))],
            out_specs=pl.BlockSpec((tme, F), lambda i, s, te, n: (i, 0)),
            scratch_shapes=[pltpu.VMEM((2, tme, D // 2), U32), pltpu.VMEM((tme, D), BF16),
                            pltpu.SemaphoreType.DMA((2,))],
        ),
        out_shape=jax.ShapeDtypeStruct((P, F), BF16),
        compiler_params=_params(("arbitrary",)),
    )(src, tile_expert, n_used, hp, w_gate, w_up)
    n_split = max(1, D // 1024)
    spare = n_rows_out - tme + jnp.arange(tme, dtype=jnp.int32)
    return pl.pallas_call(
        functools.partial(_expert_down_kernel, tme=tme, n_split=n_split, n_tiles=n_tiles),
        grid_spec=pltpu.PrefetchScalarGridSpec(
            num_scalar_prefetch=3,
            grid=(n_tiles,),
            in_specs=[pl.BlockSpec((tme, F), lambda i, d, te, n: (i, 0)),
                      pl.BlockSpec((1, F, D), lambda i, d, te, n: (te[i], 0, 0))],
            out_specs=pl.BlockSpec(memory_space=pl.ANY),
            scratch_shapes=[pltpu.VMEM((3, tme, D // 2), U32), pltpu.SemaphoreType.DMA((3,))],
        ),
        out_shape=jax.ShapeDtypeStruct((n_rows_out, D // 2), U32),
        compiler_params=_params(("arbitrary",)),
    )(jnp.concatenate([spare, dst]), tile_expert, n_used, act, w_down)


def _final_kernel(x_ref, cw_ref, y0_ref, y1_ref, g_ref, o_ref):
    half_d = y0_ref.shape[1]
    cw0 = cw_ref[:, 0:1]
    cw1 = cw_ref[:, 1:2]
    lo0, hi0 = _unpack_pairs(y0_ref[...])
    lo1, hi1 = _unpack_pairs(y1_ref[...])
    xl = x_ref[:, :half_d] + (cw0 * lo0 + cw1 * lo1)
    xh = x_ref[:, half_d:] + (cw0 * hi0 + cw1 * hi1)
    ss = jnp.sum(xl * xl, axis=-1, keepdims=True) + jnp.sum(xh * xh, axis=-1, keepdims=True)
    inv = lax.rsqrt(ss / (2 * half_d) + RMS_EPS)
    o_ref[:, :half_d] = xl * inv * g_ref[:, :half_d]
    o_ref[:, half_d:] = xh * inv * g_ref[:, half_d:]


def final_combine(x1, cw, yt, g, tm, tile0, n_tok_tiles):
    T, D = x1.shape
    return pl.pallas_call(
        _final_kernel,
        grid=(T // tm,),
        in_specs=[pl.BlockSpec((tm, D), lambda i: (i, 0)),
                  pl.BlockSpec((tm, LANES), lambda i: (tile0 + i, 0)),
                  pl.BlockSpec((tm, D // 2), lambda i: (tile0 + i, 0)),
                  pl.BlockSpec((tm, D // 2), lambda i: (n_tok_tiles + tile0 + i, 0)),
                  pl.BlockSpec((1, D), lambda i: (0, 0))],
        out_specs=pl.BlockSpec((tm, D), lambda i: (i, 0)),
        out_shape=jax.ShapeDtypeStruct((T, D), F32),
        compiler_params=_params(("parallel",)),
    )(x1, cw, yt, yt, g.reshape(1, D).astype(F32))


def _sorted_layout(eid, tme):
    T = eid.shape[0]
    n_tiles = (2 * T) // tme + N_EXPERTS
    P = n_tiles * tme
    flat_e = eid.reshape(-1)
    onehot = (flat_e[:, None] == jnp.arange(N_EXPERTS, dtype=jnp.int32)[None, :]).astype(jnp.int32)
    csum = jnp.cumsum(onehot, axis=0)
    rank = jnp.take_along_axis(csum, flat_e[:, None], axis=1)[:, 0] - 1
    counts = csum[-1]
    padded = ((counts + tme - 1) // tme) * tme
    ends = jnp.cumsum(padded)
    pos = (ends - padded)[flat_e] + rank
    spare = 2 * T + jnp.arange(P, dtype=jnp.int32) % tme
    a = jnp.arange(2 * T, dtype=jnp.int32)
    dst = spare.at[pos].set((a % 2) * T + a // 2)
    src = jnp.where(dst < 2 * T, dst % T, 0)
    n_used = (ends[-1] // tme).astype(jnp.int32)
    tile_start = jnp.minimum(jnp.arange(n_tiles, dtype=jnp.int32), n_used - 1) * tme
    tile_expert = jnp.minimum(jnp.searchsorted(ends, tile_start, side="right"), N_EXPERTS - 1).astype(jnp.int32)
    return src, dst, tile_expert, n_used.reshape(1), n_tiles


def _encoder_trunk(x, wts):
    B, S, D = x.shape
    T = B * S
    hpg, cg = wts["hpg"], wts["cg"]
    n_grp = len(ATTN_PATTERNS)
    W = hpg * HEAD_DIM
    f_start = 3 * n_grp * W
    f_width = N_FOURIER_GROUPS * cg
    x2d = x.reshape(T, D)

    dils = sorted({d for _, d in ATTN_PATTERNS if d > 1})
    h, h_cm = rmsnorm_cast(x2d, wts["attn_norm_g"], B, S, dils)
    h_cm[1] = h

    outs, lses = [], []
    for g, (_, dil) in enumerate(ATTN_PATTERNS):
        qkv = in_projection(h_cm[dil], wts["w_in"], wts["b_full"], W, 3, lambda j, g=g: j * n_grp + g)
        o_g, lse_g = band_attention(qkv, B, S, g, hpg, n_grp * hpg)
        outs.append(o_g)
        lses.append(lse_g)

    tn = _tile(math.gcd(math.gcd(f_start, f_width), 2 * D), 1024)
    c0 = f_start // tn
    fg = in_projection(h, wts["w_in"], wts["b_full"], tn, (f_width + 2 * D) // tn, lambda j: c0 + j,
                       gate_tile0=f_width // tn)

    pc, ps = fourier_channel_stage(fg, cg, wts["chan_table"])
    cos_s, sin_s = _dft_tables(S)
    fr = fourier_sequence_stage(pc, ps, cos_s.astype(BF16), (-sin_s).astype(BF16), B, S,
                                float(1.0 / math.sqrt(S * cg)))

    merged = merge_branches(outs, lses, fr, wts["w_branch_attn"], wts["w_branch_fourier"], fg, f_width, hpg)
    return out_projection(merged, wts["w_out"], x2d)


def _moe_and_final_norm(x1s, wts):
    t_all = sum(x1.shape[0] for x1 in x1s)
    tm = _tile(math.gcd(*[x1.shape[0] for x1 in x1s]), 256, 8)
    eid, cw, hp = route(x1s, wts["ffn_norm_g"], wts["w_router"], wts["b_router"], tm)
    tme = _tile(2 * t_all, EXPERT_ROW_TILE, 8)
    src, dst, tile_expert, n_used, n_tiles = _sorted_layout(eid[:, :2], tme)
    yt = expert_ffn(hp, wts["w_expert_gate"], wts["w_expert_up"], wts["w_expert_down"], src, dst, tile_expert,
                    n_used, n_tiles, tme, 2 * t_all + tme)
    outs, tile0 = [], 0
    for x1 in x1s:
        outs.append(final_combine(x1, cw, yt, wts["final_norm_g"], tm, tile0, t_all // tm))
        tile0 += x1.shape[0] // tm
    return outs


def kernel(x_prompt, x_sample, attn_norm_g, w_in, w_branch_attn, w_branch_fourier, b_gate, w_out, ffn_norm_g, w_router_group, b_router_group, w_router_expert, b_router_expert, w_expert_gate, w_expert_up, w_expert_down, final_norm_g):
    assert w_in.shape[0] == 1, "the final norm is fused into the layer's last kernel: one layer only"
    l = 0
    D = x_prompt.shape[-1]
    hpg = w_branch_attn.shape[1] // HEAD_DIM
    cg = w_branch_fourier.shape[1] // N_FOURIER_GROUPS
    in_width = w_in.shape[2]
    n_gate = b_gate.shape[1]
    cos_c, sin_c = _dft_tables(cg)
    w_r = jnp.concatenate(
        [w_router_group[l], jnp.transpose(w_router_expert[l], (1, 0, 2)).reshape(D, N_EXPERTS)], axis=1)
    b_r = jnp.concatenate([b_router_group[l], b_router_expert[l].reshape(-1)])
    pad = LANES - w_r.shape[1]
    wts = dict(
        hpg=hpg, cg=cg, chan_table=jnp.concatenate([cos_c, sin_c], axis=1).astype(BF16),
        attn_norm_g=attn_norm_g[l], ffn_norm_g=ffn_norm_g[l], final_norm_g=final_norm_g,
        w_in=w_in[l].astype(BF16),
        b_full=jnp.concatenate([jnp.zeros((in_width - n_gate,), F32), b_gate[l].astype(F32)]).reshape(1, in_width),
        w_branch_attn=w_branch_attn[l].astype(BF16),
        w_branch_fourier=w_branch_fourier[l].astype(BF16),
        w_out=w_out[l].astype(BF16),
        w_router=jnp.pad(w_r, ((0, 0), (0, pad))).astype(BF16),
        b_router=jnp.pad(b_r, (0, pad)).reshape(1, LANES).astype(F32),
        w_expert_gate=w_expert_gate[l].astype(BF16),
        w_expert_up=w_expert_up[l].astype(BF16),
        w_expert_down=w_expert_down[l].astype(BF16),
    )
    xs = (x_prompt, x_sample)
    outs = _moe_and_final_norm([_encoder_trunk(x, wts) for x in xs], wts)
    return tuple(o.reshape(x.shape) for o, x in zip(outs, xs))
```

```python
import functools
import math

import numpy as np
import jax
import jax.numpy as jnp
from jax import lax
from jax.experimental import pallas as pl
from jax.experimental.pallas import tpu as pltpu

F32 = jnp.float32
BF16 = jnp.bfloat16
U32 = jnp.uint32

RMS_EPS = 1e-6
NEG_INF = -1e30
HEAD_DIM = 128
ATTN_PATTERNS = ((128, 1), (512, 4), (2048, 16))
N_FOURIER_GROUPS = 4
N_EXPERT_GROUPS = 4
EXPERTS_PER_GROUP = 4
N_EXPERTS = N_EXPERT_GROUPS * EXPERTS_PER_GROUP
LANES = 128
V7X_VMEM_LIMIT = 56 * 1024 * 1024
ATTN_Q_TILE = 128
ATTN_ROW_TILE = 1024
EXPERT_ROW_TILE = 512
HI_MASK = 0xFFFF0000


def _tile(n, pref, mult=LANES):
    if n <= pref:
        return n
    t = (pref // mult) * mult
    while t >= mult:
        if n % t == 0:
            return t
        t -= mult
    raise ValueError(f"no tile for {n} <= {pref}")


def _params(sem, vmem=V7X_VMEM_LIMIT):
    return pltpu.CompilerParams(dimension_semantics=sem, vmem_limit_bytes=vmem)


def _pack_pairs(lo, hi):
    a = pltpu.bitcast(lo.astype(BF16).astype(F32), U32) >> 16
    b = pltpu.bitcast(hi.astype(BF16).astype(F32), U32) & jnp.uint32(HI_MASK)
    return a | b


def _unpack_pairs(w):
    return pltpu.bitcast(w << 16, F32), pltpu.bitcast(w & jnp.uint32(HI_MASK), F32)


def _rms_kernel(x_ref, g_ref, o_ref, *cm_refs, dils):
    x = x_ref[...]
    ms = jnp.mean(x * x, axis=-1, keepdims=True)
    h = (x * lax.rsqrt(ms + RMS_EPS) * g_ref[...]).astype(o_ref.dtype)
    o_ref[...] = h
    tm = x.shape[0]
    row = lax.broadcasted_iota(jnp.int32, (tm, tm), 0)
    col = lax.broadcasted_iota(jnp.int32, (tm, tm), 1)
    for cm_ref, d in zip(cm_refs, dils):
        n = tm // d
        perm = (col == (row % n) * d + row // n).astype(h.dtype)
        hp = jnp.dot(perm, h, preferred_element_type=F32).astype(cm_ref.dtype)
        for r in range(d):
            cm_ref[0, r] = hp[r * n:(r + 1) * n]


def rmsnorm_cast(x, g, B, S, dils):
    T, D = x.shape
    tm = _tile(S, 256, 16 * max(dils, default=1))
    spb = S // tm
    outs = pl.pallas_call(
        functools.partial(_rms_kernel, dils=tuple(dils)),
        grid=(T // tm,),
        in_specs=[pl.BlockSpec((tm, D), lambda i: (i, 0)), pl.BlockSpec((1, D), lambda i: (0, 0))],
        out_specs=[pl.BlockSpec((tm, D), lambda i: (i, 0))]
        + [pl.BlockSpec((1, d, tm // d, D), lambda i: (i // spb, 0, i % spb, 0)) for d in dils],
        out_shape=[jax.ShapeDtypeStruct((T, D), BF16)]
        + [jax.ShapeDtypeStruct((B, d, S // d, D), BF16) for d in dils],
        compiler_params=_params(("parallel",)),
    )(x, g.reshape(1, D).astype(F32))
    return outs[0], {d: o.reshape(T, D) for d, o in zip(dils, outs[1:])}


def _in_proj_kernel(h_ref, w_ref, b_ref, o_ref, *, gate_tile0):
    acc = jnp.dot(h_ref[...], w_ref[...], preferred_element_type=F32)
    if gate_tile0 is None:
        o_ref[...] = acc.astype(o_ref.dtype)
        return
    j = pl.program_id(1)

    @pl.when(j < gate_tile0)
    def _():
        o_ref[...] = acc.astype(o_ref.dtype)

    @pl.when(j >= gate_tile0)
    def _():
        o_ref[...] = jax.nn.sigmoid(acc + b_ref[...]).astype(o_ref.dtype)


def in_projection(h, w, b_full, tn, n_out, w_col, gate_tile0=None):
    T, K = h.shape
    tm = _tile(T, 1024)
    return pl.pallas_call(
        functools.partial(_in_proj_kernel, gate_tile0=gate_tile0),
        grid=(T // tm, n_out),
        in_specs=[pl.BlockSpec((tm, K), lambda i, j: (i, 0)),
                  pl.BlockSpec((K, tn), lambda i, j: (0, w_col(j))),
                  pl.BlockSpec((1, tn), lambda i, j: (0, w_col(j)))],
        out_specs=pl.BlockSpec((tm, tn), lambda i, j: (i, j)),
        out_shape=jax.ShapeDtypeStruct((T, n_out * tn), BF16),
        compiler_params=_params(("parallel", "arbitrary")),
    )(h, w, b_full)


def _attn_kernel(q_ref, kp_ref, kc_ref, kn_ref, vp_ref, vc_ref, vn_ref, o_ref, lse_ref, so, sl,
                 *, tq, sub, half, hps, dil, cls_len, slopes):
    i = pl.program_id(1)
    hh = pl.program_id(2)
    key_pad = -(sub + 2 * half) % LANES
    span = sub + 2 * half + key_pad
    scale = HEAD_DIM ** -0.5
    qidx = lax.broadcasted_iota(jnp.int32, (sub, span), 0)
    kidx = lax.broadcasted_iota(jnp.int32, (sub, span), 1)
    absrel = jnp.abs(kidx - half - qidx)
    absrel_f = absrel.astype(F32)
    lane = lax.broadcasted_iota(jnp.int32, (sub, LANES), 1)
    n_sub = tq // sub
    valids = []
    for u in range(n_sub):
        kpos = i * tq + (u * sub - half) + kidx
        valids.append((absrel <= half) & (kpos >= 0) & (kpos < cls_len))
    neg_slopes = []
    for h in range(hps):
        s_h = jnp.float32(0.0)
        for b in range(len(slopes) // hps):
            s_h = jnp.where(hh == b, jnp.float32(-slopes[b * hps + h] * dil), s_h)
        neg_slopes.append(s_h)

    @pl.when(hh == 0)
    def _():
        sl[...] = jnp.zeros_like(sl)

    def keys(prev_ref, cur_ref, next_ref, r, u, cols):
        lo = u * sub - half if u > 0 else 0
        hi = (u + 1) * sub + half if u < n_sub - 1 else tq
        parts = [cur_ref[0, r, lo:hi, cols]]
        if u == 0:
            parts.insert(0, prev_ref[0, r, :, cols])
        if u == n_sub - 1:
            parts.append(next_ref[0, r, :, cols])
        if key_pad:
            parts.append(jnp.zeros((key_pad, HEAD_DIM), cur_ref.dtype))
        return parts[0] if len(parts) == 1 else jnp.concatenate(parts, axis=0)

    for r in range(dil):
        for u in range(n_sub):
            rows = pl.ds(u * sub * dil + r, sub, stride=dil) if dil > 1 else pl.ds(u * sub, sub)
            lse_tile = sl[rows, :]
            for h in range(hps):
                cols = slice(h * HEAD_DIM, (h + 1) * HEAD_DIM)
                q = q_ref[0, r, u * sub:(u + 1) * sub, cols]
                k = keys(kp_ref, kc_ref, kn_ref, r, u, cols)
                v = keys(vp_ref, vc_ref, vn_ref, r, u, cols)
                s = lax.dot_general(q, k, (((1,), (1,)), ((), ())), preferred_element_type=F32)
                s = s * scale + absrel_f * neg_slopes[h]
                s = jnp.where(valids[u], s, NEG_INF)
                m = jnp.max(s, axis=-1, keepdims=True)
                p = jnp.exp(s - m)
                den = jnp.sum(p, axis=-1, keepdims=True)
                so[h, rows, :] = jnp.dot(p.astype(v.dtype), v, preferred_element_type=F32) / den
                lse_tile = jnp.where(lane == hh * hps + h, m + jnp.log(den), lse_tile)
            sl[rows, :] = lse_tile
    for h in range(hps):
        o_ref[0, :, h * HEAD_DIM:(h + 1) * HEAD_DIM] = so[h].astype(o_ref.dtype)
    lse_ref[0] = sl[...]


def band_attention(qkv, B, S, group, hpg, n_heads_total):
    window, dil = ATTN_PATTERNS[group]
    half = window // (2 * dil)
    W = hpg * HEAD_DIM
    L = S // dil
    tq = min(L, max(ATTN_Q_TILE, ATTN_ROW_TILE // dil))
    rows = tq * dil
    sub = min(ATTN_Q_TILE, tq)
    hps = max(1, hpg * ATTN_ROW_TILE // max(rows, ATTN_ROW_TILE))
    assert S % dil == 0 and L % tq == 0 and tq % sub == 0 and tq % half == 0 and half % 16 == 0 and hpg % hps == 0
    n_hh = hpg // hps
    Wb = hps * HEAD_DIM
    hb = tq // half
    n_hb = L // half
    s_all = 2.0 ** (-8.0 * np.arange(1, n_heads_total + 1) / n_heads_total)
    slopes = tuple(float(np.float32(v)) for v in s_all[group * hpg:(group + 1) * hpg])
    qv = qkv.reshape(B, dil, L, 3 * W)

    cur = lambda part: pl.BlockSpec((1, dil, tq, Wb), lambda b, i, hh: (b, 0, i, part * n_hh + hh))
    prv = lambda part: pl.BlockSpec(
        (1, dil, half, Wb), lambda b, i, hh: (b, 0, jnp.maximum(i * hb - 1, 0), part * n_hh + hh))
    nxt = lambda part: pl.BlockSpec(
        (1, dil, half, Wb), lambda b, i, hh: (b, 0, jnp.minimum((i + 1) * hb, n_hb - 1), part * n_hh + hh))
    o, lse = pl.pallas_call(
        functools.partial(_attn_kernel, tq=tq, sub=sub, half=half, hps=hps, dil=dil, cls_len=L, slopes=slopes),
        grid=(B, L // tq, n_hh),
        in_specs=[cur(0), prv(1), cur(1), nxt(1), prv(2), cur(2), nxt(2)],
        out_specs=[pl.BlockSpec((1, rows, Wb), lambda b, i, hh: (b, i, hh)),
                   pl.BlockSpec((1, rows, LANES), lambda b, i, hh: (b, i, 0))],
        out_shape=[jax.ShapeDtypeStruct((B, S, W), BF16), jax.ShapeDtypeStruct((B, S, LANES), F32)],
        scratch_shapes=[pltpu.VMEM((hps, rows, LANES), F32), pltpu.VMEM((rows, LANES), F32)],
        compiler_params=_params(("parallel", "arbitrary", "arbitrary")),
    )(qv, qv, qv, qv, qv, qv, qv)
    return o.reshape(B * S, W), lse.reshape(B * S, LANES)


def _dft_tables(n):
    k = jnp.arange(n, dtype=jnp.int32)

    def thin(cols):
        ang = ((k[:, None] * cols[None, :]) % n).astype(F32) * np.float32(2.0 * np.pi / n)
        return jnp.cos(ang), jnp.sin(ang)

    m = 64
    if n <= m or n % m:
        return thin(k)
    c_hi, s_hi = thin(jnp.arange(n // m, dtype=jnp.int32) * m)
    c_lo, s_lo = thin(jnp.arange(m, dtype=jnp.int32))
    cos = c_hi[:, :, None] * c_lo[:, None, :] - s_hi[:, :, None] * s_lo[:, None, :]
    sin = s_hi[:, :, None] * c_lo[:, None, :] + c_hi[:, :, None] * s_lo[:, None, :]
    return cos.reshape(n, n), sin.reshape(n, n)


def _fourier_chan_kernel(x_ref, t_ref, pc_ref, ps_ref):
    r = jnp.dot(x_ref[...], t_ref[...], preferred_element_type=F32)
    c = pc_ref.shape[-1]
    pc_ref[...] = r[:, :c].astype(pc_ref.dtype)
    ps_ref[...] = r[:, c:].astype(ps_ref.dtype)


def fourier_channel_stage(fg, cg, table):
    T = fg.shape[0]
    tm = _tile(T, 1024)
    return pl.pallas_call(
        _fourier_chan_kernel,
        grid=(T // tm, N_FOURIER_GROUPS),
        in_specs=[pl.BlockSpec((tm, cg), lambda i, g: (i, g)),
                  pl.BlockSpec((cg, 2 * cg), lambda i, g: (0, 0))],
        out_specs=[pl.BlockSpec((tm, cg), lambda i, g: (i, g)), pl.BlockSpec((tm, cg), lambda i, g: (i, g))],
        out_shape=[jax.ShapeDtypeStruct((T, N_FOURIER_GROUPS * cg), BF16)] * 2,
        compiler_params=_params(("parallel", "arbitrary")),
    )(fg, table)


def _fourier_seq_kernel(cs_ref, ss_ref, pc_ref, ps_ref, o_ref, acc_ref, *, scale):
    k = pl.program_id(3)

    @pl.when(k == 0)
    def _():
        acc_ref[...] = jnp.zeros_like(acc_ref)

    acc_ref[...] += (jnp.dot(cs_ref[...], pc_ref[0], preferred_element_type=F32)
                     + jnp.dot(ss_ref[...], ps_ref[0], preferred_element_type=F32))

    @pl.when(k == pl.num_programs(3) - 1)
    def _():
        o_ref[0] = (acc_ref[...] * scale).astype(o_ref.dtype)


def fourier_sequence_stage(pc, ps, cos_s, neg_sin_s, B, S, scale):
    N = pc.shape[1]
    tm, tn, tk = _tile(S, 1024), _tile(N, 1024), _tile(S, 2048)
    pc3, ps3 = pc.reshape(B, S, N), ps.reshape(B, S, N)
    out = pl.pallas_call(
        functools.partial(_fourier_seq_kernel, scale=scale),
        grid=(B, S // tm, N // tn, S // tk),
        in_specs=[pl.BlockSpec((tm, tk), lambda b, i, j, k: (i, k)),
                  pl.BlockSpec((tm, tk), lambda b, i, j, k: (i, k)),
                  pl.BlockSpec((1, tk, tn), lambda b, i, j, k: (b, k, j)),
                  pl.BlockSpec((1, tk, tn), lambda b, i, j, k: (b, k, j))],
        out_specs=pl.BlockSpec((1, tm, tn), lambda b, i, j, k: (b, i, j)),
        out_shape=jax.ShapeDtypeStruct((B, S, N), BF16),
        scratch_shapes=[pltpu.VMEM((tm, tn), F32)],
        compiler_params=_params(("parallel", "parallel", "parallel", "arbitrary")),
    )(cos_s, neg_sin_s, pc3, ps3)
    return out.reshape(B * S, N)


def _merge_kernel(o0_ref, o1_ref, o2_ref, l0_ref, l1_ref, l2_ref, fr_ref, wa_ref, wf_ref, *refs, hpg, n_b):
    ga_refs, gf_refs = refs[:n_b], refs[n_b:2 * n_b]
    m_ref, oc_ref = refs[2 * n_b], refs[2 * n_b + 1]
    l0, l1, l2 = l0_ref[...], l1_ref[...], l2_ref[...]
    m = jnp.maximum(jnp.maximum(l0, l1), l2)
    e0, e1, e2 = jnp.exp(l0 - m), jnp.exp(l1 - m), jnp.exp(l2 - m)
    den = e0 + e1 + e2
    w0, w1, w2 = e0 / den, e1 / den, e2 / den
    for h in range(hpg):
        cols = slice(h * HEAD_DIM, (h + 1) * HEAD_DIM)
        oc = (w0[:, h:h + 1] * o0_ref[:, cols].astype(F32)
              + w1[:, h:h + 1] * o1_ref[:, cols].astype(F32)
              + w2[:, h:h + 1] * o2_ref[:, cols].astype(F32))
        oc_ref[:, cols] = oc.astype(oc_ref.dtype)
    oc_all = oc_ref[...]
    fr = fr_ref[...]
    bw = ga_refs[0].shape[1]
    for c in range(n_b):
        cols = slice(c * bw, (c + 1) * bw)
        attn = jnp.dot(oc_all, wa_ref[:, cols], preferred_element_type=F32)
        four = jnp.dot(fr, wf_ref[:, cols], preferred_element_type=F32)
        m_ref[:, cols] = (ga_refs[c][...].astype(F32) * attn + gf_refs[c][...].astype(F32) * four).astype(m_ref.dtype)


def merge_branches(os_, lses, fr, wa, wf, fg, gate_start, hpg):
    T, W = os_[0].shape
    D = wa.shape[1]
    FW = fr.shape[1]
    tm = _tile(T, 256)
    bw = math.gcd(D, gate_start)
    n_b, g0 = D // bw, gate_start // bw
    row = lambda w: pl.BlockSpec((tm, w), lambda i: (i, 0))
    resident = lambda r: pl.BlockSpec((r, D), lambda i: (0, 0), pipeline_mode=pl.Buffered(1))
    gate = lambda blk: pl.BlockSpec((tm, bw), lambda i: (i, blk))
    return pl.pallas_call(
        functools.partial(_merge_kernel, hpg=hpg, n_b=n_b),
        grid=(T // tm,),
        in_specs=[row(W), row(W), row(W), row(LANES), row(LANES), row(LANES), row(FW), resident(W), resident(FW)]
        + [gate(g0 + c) for c in range(2 * n_b)],
        out_specs=pl.BlockSpec((tm, D), lambda i: (i, 0)),
        out_shape=jax.ShapeDtypeStruct((T, D), BF16),
        scratch_shapes=[pltpu.VMEM((tm, W), BF16)],
        compiler_params=_params(("parallel",)),
    )(*os_, *lses, fr, wa, wf, *([fg] * (2 * n_b)))


def _out_proj_kernel(m_ref, w_ref, x_ref, o_ref):
    o_ref[...] = x_ref[...] + jnp.dot(m_ref[...], w_ref[...], preferred_element_type=F32)


def out_projection(merged, w, x):
    T, K = merged.shape
    N = w.shape[1]
    tm, tn = _tile(T, 1024), _tile(N, 512)
    return pl.pallas_call(
        _out_proj_kernel,
        grid=(T // tm, N // tn),
        in_specs=[pl.BlockSpec((tm, K), lambda i, j: (i, 0)),
                  pl.BlockSpec((K, tn), lambda i, j: (0, j)),
                  pl.BlockSpec((tm, tn), lambda i, j: (i, j))],
        out_specs=pl.BlockSpec((tm, tn), lambda i, j: (i, j)),
        out_shape=jax.ShapeDtypeStruct((T, N), F32),
        compiler_params=_params(("parallel", "arbitrary")),
    )(merged, w, x)


def _router_kernel(*refs, bounds):
    n_in = len(bounds) - 1
    i = pl.program_id(0)
    for k in range(n_in):
        @pl.when((i >= bounds[k]) & (i < bounds[k + 1]))
        def _(k=k):
            _route_tile(refs[k], *refs[n_in:])


def _route_tile(x_ref, g_ref, w_ref, b_ref, eid_ref, cw_ref, hp_ref):
    x = x_ref[...]
    ms = jnp.mean(x * x, axis=-1, keepdims=True)
    hf = x * lax.rsqrt(ms + RMS_EPS) * g_ref[...]
    h = hf.astype(BF16)
    half_d = hf.shape[1] // 2
    hp_ref[...] = _pack_pairs(hf[:, :half_d], hf[:, half_d:])
    lg = jnp.dot(h, w_ref[...], preferred_element_type=F32) + b_ref[...]
    lane = lax.broadcasted_iota(jnp.int32, lg.shape, 1)
    big = jnp.int32(LANES)
    in_grp = lane < N_EXPERT_GROUPS
    gl = jnp.where(in_grp, lg, -jnp.inf)
    gmax = jnp.max(gl, axis=-1, keepdims=True)
    gsel = jnp.min(jnp.where(gl == gmax, lane, big), axis=-1, keepdims=True)
    p_group = 1.0 / jnp.sum(jnp.where(in_grp, jnp.exp(gl - gmax), 0.0), axis=-1, keepdims=True)
    lo = N_EXPERT_GROUPS + gsel * EXPERTS_PER_GROUP
    in_sel = (lane >= lo) & (lane < lo + EXPERTS_PER_GROUP)
    el = jnp.where(in_sel, lg, -jnp.inf)
    t1 = jnp.max(el, axis=-1, keepdims=True)
    i1 = jnp.min(jnp.where(el == t1, lane, big), axis=-1, keepdims=True)
    el2 = jnp.where(lane == i1, -jnp.inf, el)
    t2 = jnp.max(el2, axis=-1, keepdims=True)
    i2 = jnp.min(jnp.where(el2 == t2, lane, big), axis=-1, keepdims=True)
    e21 = jnp.exp(t2 - t1)
    p1 = 1.0 / (1.0 + e21)
    p2 = e21 / (1.0 + e21)
    eid_ref[...] = jnp.where(lane == 0, i1 - N_EXPERT_GROUPS, jnp.where(lane == 1, i2 - N_EXPERT_GROUPS, 0))
    cw_ref[...] = jnp.where(lane == 0, p_group * p1, jnp.where(lane == 1, p_group * p2, 0.0))


def route(x1s, g, w_router, b_router, tm):
    D = x1s[0].shape[1]
    bounds = [0]
    for x1 in x1s:
        bounds.append(bounds[-1] + x1.shape[0] // tm)
    T = bounds[-1] * tm

    def x_spec(k):
        lo, n = bounds[k], bounds[k + 1] - bounds[k]
        return pl.BlockSpec((tm, D), lambda i: (jnp.clip(i - lo, 0, n - 1), 0))

    return pl.pallas_call(
        functools.partial(_router_kernel, bounds=tuple(bounds)),
        grid=(bounds[-1],),
        in_specs=[x_spec(k) for k in range(len(x1s))]
        + [pl.BlockSpec((1, D), lambda i: (0, 0)),
           pl.BlockSpec((D, LANES), lambda i: (0, 0)),
           pl.BlockSpec((1, LANES), lambda i: (0, 0))],
        out_specs=[pl.BlockSpec((tm, LANES), lambda i: (i, 0)), pl.BlockSpec((tm, LANES), lambda i: (i, 0)),
                   pl.BlockSpec((tm, D // 2), lambda i: (i, 0))],
        out_shape=[jax.ShapeDtypeStruct((T, LANES), jnp.int32), jax.ShapeDtypeStruct((T, LANES), F32),
                   jax.ShapeDtypeStruct((T, D // 2), U32)],
        compiler_params=_params(("arbitrary",)),
    )(*x1s, g.reshape(1, D).astype(F32), w_router, b_router)


def _expert_up_kernel(src_ref, te_ref, nused_ref, hp_hbm, wg_ref, wu_ref, a_ref, buf, xs, sem, *, tme):
    i = pl.program_id(0)
    n_used = nused_ref[0]
    half_d = xs.shape[1] // 2

    def row_copy(tile, slot, r):
        return pltpu.make_async_copy(hp_hbm.at[pl.ds(src_ref[tile * tme + r], 1), :],
                                     buf.at[slot, pl.ds(r, 1), :], sem.at[slot])

    def drain(tile, slot):
        def body(r, c):
            row_copy(tile, slot, r).wait()
            return c
        lax.fori_loop(0, tme, body, 0, unroll=8)

    @pl.when(i == 0)
    def _():
        def body(r, c):
            row_copy(0, 0, r).start()
            return c
        lax.fori_loop(0, tme, body, 0, unroll=8)

    @pl.when(i < n_used)
    def _():
        slot = i % 2
        drain(i, slot)
        lo, hi = _unpack_pairs(buf[slot])
        xs[:, :half_d] = lo.astype(xs.dtype)
        xs[:, half_d:] = hi.astype(xs.dtype)
        nxt = jnp.minimum(i + 1, n_used - 1)
        for r in range(tme):
            row_copy(nxt, 1 - slot, r).start(priority=r % 2)
        x = xs[...]
        gate = jnp.dot(x, wg_ref[0], preferred_element_type=F32)
        up = jnp.dot(x, wu_ref[0], preferred_element_type=F32)
        a_ref[...] = (jax.nn.silu(gate) * up).astype(a_ref.dtype)

    @pl.when(i == n_used - 1)
    def _():
        drain(i, 1 - i % 2)

    @pl.when(i >= n_used)
    def _():
        a_ref[...] = jnp.zeros_like(a_ref)


def _expert_down_kernel(dst_ref, te_ref, nused_ref, a_ref, wd_ref, yt_hbm, ybuf, sem, *, tme, n_split, n_tiles):
    i = pl.program_id(0)
    n_used = nused_ref[0]
    last = n_tiles - 1
    half_d = ybuf.shape[2]
    wc = half_d // n_split

    def row_copy(tile, r):
        slot = (tile + 3) % 3
        return pltpu.make_async_copy(ybuf.at[slot, pl.ds(r, 1), :],
                                     yt_hbm.at[pl.ds(dst_ref[(tile + 1) * tme + r], 1), :], sem.at[slot])

    def drain(tile):
        def body(r, c):
            row_copy(tile, r).wait()
            return c
        lax.fori_loop(0, tme, body, 0, unroll=8)

    @pl.when(i == 0)
    def _():
        ybuf[2] = jnp.zeros(ybuf.shape[1:], ybuf.dtype)

    @pl.when((i >= 2) & (i - 3 < n_used))
    def _():
        drain(i - 3)

    @pl.when(i < n_used)
    def _():
        a = a_ref[...]
        for r in range(tme):
            row_copy(i - 1, r).start(priority=r % 2)
        for k in range(n_split):
            lo = jnp.dot(a, wd_ref[0, :, k * wc:(k + 1) * wc], preferred_element_type=F32)
            hi = jnp.dot(a, wd_ref[0, :, half_d + k * wc:half_d + (k + 1) * wc], preferred_element_type=F32)
            ybuf[i % 3, :, k * wc:(k + 1) * wc] = _pack_pairs(lo, hi)

    @pl.when(i == n_used)
    def _():
        def body(r, c):
            row_copy(i - 1, r).start()
            return c
        lax.fori_loop(0, tme, body, 0, unroll=8)

    @pl.when((i == last) & (last - 2 < n_used))
    def _():
        drain(last - 2)

    @pl.when((i == last) & (last - 1 < n_used))
    def _():
        drain(last - 1)


def expert_ffn(hp, w_gate, w_up, w_down, src, dst, tile_expert, n_used, n_tiles, tme, n_rows_out):
    D = hp.shape[1] * 2
    F = w_gate.shape[2]
    P = n_tiles * tme
    act = pl.pallas_call(
        functools.partial(_expert_up_kernel, tme=tme),
        grid_spec=pltpu.PrefetchScalarGridSpec(
            num_scalar_prefetch=3,
            grid=(n_tiles,),
            in_specs=[pl.BlockSpec(memory_space=pl.ANY),
                      pl.BlockSpec((1, D, F), lambda i, s, te, n: (te[i], 0, 0)),
                      pl.BlockSpec((1, D, F), lambda i, s, te, n: (te[i], 0, 0))],
            out_specs=pl.BlockSpec((tme, F), lambda i, s, te, n: (i, 0)),
            scratch_shapes=[pltpu.VMEM((2, tme, D // 2), U32), pltpu.VMEM((tme, D), BF16),
                            pltpu.SemaphoreType.DMA((2,))],
        ),
        out_shape=jax.ShapeDtypeStruct((P, F), BF16),
        compiler_params=_params(("arbitrary",)),
    )(src, tile_expert, n_used, hp, w_gate, w_up)
    n_split = max(1, D // 1024)
    spare = n_rows_out - tme + jnp.arange(tme, dtype=jnp.int32)
    return pl.pallas_call(
        functools.partial(_expert_down_kernel, tme=tme, n_split=n_split, n_tiles=n_tiles),
        grid_spec=pltpu.PrefetchScalarGridSpec(
            num_scalar_prefetch=3,
            grid=(n_tiles,),
            in_specs=[pl.BlockSpec((tme, F), lambda i, d, te, n: (i, 0)),
                      pl.BlockSpec((1, F, D), lambda i, d, te, n: (te[i], 0, 0))],
            out_specs=pl.BlockSpec(memory_space=pl.ANY),
            scratch_shapes=[pltpu.VMEM((3, tme, D // 2), U32), pltpu.SemaphoreType.DMA((3,))],
        ),
        out_shape=jax.ShapeDtypeStruct((n_rows_out, D // 2), U32),
        compiler_params=_params(("arbitrary",)),
    )(jnp.concatenate([spare, dst]), tile_expert, n_used, act, w_down)


def _final_kernel(x_ref, cw_ref, y0_ref, y1_ref, g_ref, o_ref):
    half_d = y0_ref.shape[1]
    cw0 = cw_ref[:, 0:1]
    cw1 = cw_ref[:, 1:2]
    lo0, hi0 = _unpack_pairs(y0_ref[...])
    lo1, hi1 = _unpack_pairs(y1_ref[...])
    xl = x_ref[:, :half_d] + (cw0 * lo0 + cw1 * lo1)
    xh = x_ref[:, half_d:] + (cw0 * hi0 + cw1 * hi1)
    ss = jnp.sum(xl * xl, axis=-1, keepdims=True) + jnp.sum(xh * xh, axis=-1, keepdims=True)
    inv = lax.rsqrt(ss / (2 * half_d) + RMS_EPS)
    o_ref[:, :half_d] = xl * inv * g_ref[:, :half_d]
    o_ref[:, half_d:] = xh * inv * g_ref[:, half_d:]


def final_combine(x1, cw, yt, g, tm, tile0, n_tok_tiles):
    T, D = x1.shape
    return pl.pallas_call(
        _final_kernel,
        grid=(T // tm,),
        in_specs=[pl.BlockSpec((tm, D), lambda i: (i, 0)),
                  pl.BlockSpec((tm, LANES), lambda i: (tile0 + i, 0)),
                  pl.BlockSpec((tm, D // 2), lambda i: (tile0 + i, 0)),
                  pl.BlockSpec((tm, D // 2), lambda i: (n_tok_tiles + tile0 + i, 0)),
                  pl.BlockSpec((1, D), lambda i: (0, 0))],
        out_specs=pl.BlockSpec((tm, D), lambda i: (i, 0)),
        out_shape=jax.ShapeDtypeStruct((T, D), F32),
        compiler_params=_params(("parallel",)),
    )(x1, cw, yt, yt, g.reshape(1, D).astype(F32))


def _sorted_layout(eid, tme):
    T = eid.shape[0]
    n_tiles = (2 * T) // tme + N_EXPERTS
    P = n_tiles * tme
    flat_e = eid.reshape(-1)
    onehot = (flat_e[:, None] == jnp.arange(N_EXPERTS, dtype=jnp.int32)[None, :]).astype(jnp.int32)
    csum = jnp.cumsum(onehot, axis=0)
    rank = jnp.take_along_axis(csum, flat_e[:, None], axis=1)[:, 0] - 1
    counts = csum[-1]
    padded = ((counts + tme - 1) // tme) * tme
    ends = jnp.cumsum(padded)
    pos = (ends - padded)[flat_e] + rank
    spare = 2 * T + jnp.arange(P, dtype=jnp.int32) % tme
    a = jnp.arange(2 * T, dtype=jnp.int32)
    dst = spare.at[pos].set((a % 2) * T + a // 2)
    src = jnp.where(dst < 2 * T, dst % T, 0)
    n_used = (ends[-1] // tme).astype(jnp.int32)
    tile_start = jnp.minimum(jnp.arange(n_tiles, dtype=jnp.int32), n_used - 1) * tme
    tile_expert = jnp.minimum(jnp.searchsorted(ends, tile_start, side="right"), N_EXPERTS - 1).astype(jnp.int32)
    return src, dst, tile_expert, n_used.reshape(1), n_tiles


def _encoder_trunk(x, wts):
    B, S, D = x.shape
    T = B * S
    hpg, cg = wts["hpg"], wts["cg"]
    n_grp = len(ATTN_PATTERNS)
    W = hpg * HEAD_DIM
    f_start = 3 * n_grp * W
    f_width = N_FOURIER_GROUPS * cg
    x2d = x.reshape(T, D)

    dils = sorted({d for _, d in ATTN_PATTERNS if d > 1})
    h, h_cm = rmsnorm_cast(x2d, wts["attn_norm_g"], B, S, dils)
    h_cm[1] = h

    outs, lses = [], []
    for g, (_, dil) in enumerate(ATTN_PATTERNS):
        qkv = in_projection(h_cm[dil], wts["w_in"], wts["b_full"], W, 3, lambda j, g=g: j * n_grp + g)
        o_g, lse_g = band_attention(qkv, B, S, g, hpg, n_grp * hpg)
        outs.append(o_g)
        lses.append(lse_g)

    tn = _tile(math.gcd(math.gcd(f_start, f_width), 2 * D), 1024)
    c0 = f_start // tn
    fg = in_projection(h, wts["w_in"], wts["b_full"], tn, (f_width + 2 * D) // tn, lambda j: c0 + j,
                       gate_tile0=f_width // tn)

    pc, ps = fourier_channel_stage(fg, cg, wts["chan_table"])
    cos_s, sin_s = _dft_tables(S)
    fr = fourier_sequence_stage(pc, ps, cos_s.astype(BF16), (-sin_s).astype(BF16), B, S,
                                float(1.0 / math.sqrt(S * cg)))

    merged = merge_branches(outs, lses, fr, wts["w_branch_attn"], wts["w_branch_fourier"], fg, f_width, hpg)
    return out_projection(merged, wts["w_out"], x2d)


def _moe_and_final_norm(x1s, wts):
    t_all = sum(x1.shape[0] for x1 in x1s)
    tm = _tile(math.gcd(*[x1.shape[0] for x1 in x1s]), 256, 8)
    eid, cw, hp = route(x1s, wts["ffn_norm_g"], wts["w_router"], wts["b_router"], tm)
    tme = _tile(2 * t_all, EXPERT_ROW_TILE, 8)
    src, dst, tile_expert, n_used, n_tiles = _sorted_layout(eid[:, :2], tme)
    yt = expert_ffn(hp, wts["w_expert_gate"], wts["w_expert_up"], wts["w_expert_down"], src, dst, tile_expert,
                    n_used, n_tiles, tme, 2 * t_all + tme)
    outs, tile0 = [], 0
    for x1 in x1s:
        outs.append(final_combine(x1, cw, yt, wts["final_norm_g"], tm, tile0, t_all // tm))
        tile0 += x1.shape[0] // tm
    return outs


def kernel(x_prompt, x_sample, attn_norm_g, w_in, w_branch_attn, w_branch_fourier, b_gate, w_out, ffn_norm_g, w_router_group, b_router_group, w_router_expert, b_router_expert, w_expert_gate, w_expert_up, w_expert_down, final_norm_g):
    assert w_in.shape[0] == 1, "the final norm is fused into the layer's last kernel: one layer only"
    l = 0
    D = x_prompt.shape[-1]
    hpg = w_branch_attn.shape[1] // HEAD_DIM
    cg = w_branch_fourier.shape[1] // N_FOURIER_GROUPS
    in_width = w_in.shape[2]
    n_gate = b_gate.shape[1]
    cos_c, sin_c = _dft_tables(cg)
    w_r = jnp.concatenate(
        [w_router_group[l], jnp.transpose(w_router_expert[l], (1, 0, 2)).reshape(D, N_EXPERTS)], axis=1)
    b_r = jnp.concatenate([b_router_group[l], b_router_expert[l].reshape(-1)])
    pad = LANES - w_r.shape[1]
    wts = dict(
        hpg=hpg, cg=cg, chan_table=jnp.concatenate([cos_c, sin_c], axis=1).astype(BF16),
        attn_norm_g=attn_norm_g[l], ffn_norm_g=ffn_norm_g[l], final_norm_g=final_norm_g,
        w_in=w_in[l].astype(BF16),
        b_full=jnp.concatenate([jnp.zeros((in_width - n_gate,), F32), b_gate[l].astype(F32)]).reshape(1, in_width),
        w_branch_attn=w_branch_attn[l].astype(BF16),
        w_branch_fourier=w_branch_fourier[l].astype(BF16),
        w_out=w_out[l].astype(BF16),
        w_router=jnp.pad(w_r, ((0, 0), (0, pad))).astype(BF16),
        b_router=jnp.pad(b_r, (0, pad)).reshape(1, LANES).astype(F32),
        w_expert_gate=w_expert_gate[l].astype(BF16),
        w_expert_up=w_expert_up[l].astype(BF16),
        w_expert_down=w_expert_down[l].astype(BF16),
    )
    xs = (x_prompt, x_sample)
    outs = _moe_and_final_norm([_encoder_trunk(x, wts) for x in xs], wts)
    return tuple(o.reshape(x.shape) for o, x in zip(outs, xs))
```

```python
import functools
import math

import numpy as np
import jax
import jax.numpy as jnp
from jax import lax
from jax.experimental import pallas as pl
from jax.experimental.pallas import tpu as pltpu

F32 = jnp.float32
BF16 = jnp.bfloat16
U32 = jnp.uint32

RMS_EPS = 1e-6
NEG_INF = -1e30
HEAD_DIM = 128
ATTN_PATTERNS = ((128, 1), (512, 4), (2048, 16))
N_FOURIER_GROUPS = 4
N_EXPERT_GROUPS = 4
EXPERTS_PER_GROUP = 4
N_EXPERTS = N_EXPERT_GROUPS * EXPERTS_PER_GROUP
LANES = 128
V7X_VMEM_LIMIT = 56 * 1024 * 1024
ATTN_Q_TILE = 128
ATTN_ROW_TILE = 1024
EXPERT_ROW_TILE = 512
HI_MASK = 0xFFFF0000


def _tile(n, pref, mult=LANES):
    if n <= pref:
        return n
    t = (pref // mult) * mult
    while t >= mult:
        if n % t == 0:
            return t
        t -= mult
    raise ValueError(f"no tile for {n} <= {pref}")


def _params(sem, vmem=V7X_VMEM_LIMIT):
    return pltpu.CompilerParams(dimension_semantics=sem, vmem_limit_bytes=vmem)


def _pack_pairs(lo, hi):
    a = pltpu.bitcast(lo.astype(BF16).astype(F32), U32) >> 16
    b = pltpu.bitcast(hi.astype(BF16).astype(F32), U32) & jnp.uint32(HI_MASK)
    return a | b


def _unpack_pairs(w):
    return pltpu.bitcast(w << 16, F32), pltpu.bitcast(w & jnp.uint32(HI_MASK), F32)


def _rms_kernel(x_ref, g_ref, o_ref, *cm_refs, dils):
    x = x_ref[...]
    ms = jnp.mean(x * x, axis=-1, keepdims=True)
    h = (x * lax.rsqrt(ms + RMS_EPS) * g_ref[...]).astype(o_ref.dtype)
    o_ref[...] = h
    tm = x.shape[0]
    row = lax.broadcasted_iota(jnp.int32, (tm, tm), 0)
    col = lax.broadcasted_iota(jnp.int32, (tm, tm), 1)
    for cm_ref, d in zip(cm_refs, dils):
        n = tm // d
        perm = (col == (row % n) * d + row // n).astype(h.dtype)
        hp = jnp.dot(perm, h, preferred_element_type=F32).astype(cm_ref.dtype)
        for r in range(d):
            cm_ref[0, r] = hp[r * n:(r + 1) * n]


def rmsnorm_cast(x, g, B, S, dils):
    T, D = x.shape
    tm = _tile(S, 256, 16 * max(dils, default=1))
    spb = S // tm
    outs = pl.pallas_call(
        functools.partial(_rms_kernel, dils=tuple(dils)),
        grid=(T // tm,),
        in_specs=[pl.BlockSpec((tm, D), lambda i: (i, 0)), pl.BlockSpec((1, D), lambda i: (0, 0))],
        out_specs=[pl.BlockSpec((tm, D), lambda i: (i, 0))]
        + [pl.BlockSpec((1, d, tm // d, D), lambda i: (i // spb, 0, i % spb, 0)) for d in dils],
        out_shape=[jax.ShapeDtypeStruct((T, D), BF16)]
        + [jax.ShapeDtypeStruct((B, d, S // d, D), BF16) for d in dils],
        compiler_params=_params(("parallel",)),
    )(x, g.reshape(1, D).astype(F32))
    return outs[0], {d: o.reshape(T, D) for d, o in zip(dils, outs[1:])}


def _in_proj_kernel(h_ref, w_ref, b_ref, o_ref, *, gate_tile0):
    acc = jnp.dot(h_ref[...], w_ref[...], preferred_element_type=F32)
    if gate_tile0 is None:
        o_ref[...] = acc.astype(o_ref.dtype)
        return
    j = pl.program_id(1)

    @pl.when(j < gate_tile0)
    def _():
        o_ref[...] = acc.astype(o_ref.dtype)

    @pl.when(j >= gate_tile0)
    def _():
        o_ref[...] = jax.nn.sigmoid(acc + b_ref[...]).astype(o_ref.dtype)


def in_projection(h, w, b_full, tn, n_out, w_col, gate_tile0=None):
    T, K = h.shape
    tm = _tile(T, 1024)
    return pl.pallas_call(
        functools.partial(_in_proj_kernel, gate_tile0=gate_tile0),
        grid=(T // tm, n_out),
        in_specs=[pl.BlockSpec((tm, K), lambda i, j: (i, 0)),
                  pl.BlockSpec((K, tn), lambda i, j: (0, w_col(j))),
                  pl.BlockSpec((1, tn), lambda i, j: (0, w_col(j)))],
        out_specs=pl.BlockSpec((tm, tn), lambda i, j: (i, j)),
        out_shape=jax.ShapeDtypeStruct((T, n_out * tn), BF16),
        compiler_params=_params(("parallel", "arbitrary")),
    )(h, w, b_full)


def _attn_kernel(q_ref, kp_ref, kc_ref, kn_ref, vp_ref, vc_ref, vn_ref, o_ref, lse_ref, so, sl,
                 *, tq, sub, half, hps, dil, cls_len, slopes):
    i = pl.program_id(1)
    hh = pl.program_id(2)
    key_pad = -(sub + 2 * half) % LANES
    span = sub + 2 * half + key_pad
    scale = HEAD_DIM ** -0.5
    qidx = lax.broadcasted_iota(jnp.int32, (sub, span), 0)
    kidx = lax.broadcasted_iota(jnp.int32, (sub, span), 1)
    absrel = jnp.abs(kidx - half - qidx)
    absrel_f = absrel.astype(F32)
    lane = lax.broadcasted_iota(jnp.int32, (sub, LANES), 1)
    n_sub = tq // sub
    valids = []
    for u in range(n_sub):
        kpos = i * tq + (u * sub - half) + kidx
        valids.append((absrel <= half) & (kpos >= 0) & (kpos < cls_len))
    neg_slopes = []
    for h in range(hps):
        s_h = jnp.float32(0.0)
        for b in range(len(slopes) // hps):
            s_h = jnp.where(hh == b, jnp.float32(-slopes[b * hps + h] * dil), s_h)
        neg_slopes.append(s_h)

    @pl.when(hh == 0)
    def _():
        sl[...] = jnp.zeros_like(sl)

    def keys(prev_ref, cur_ref, next_ref, r, u, cols):
        lo = u * sub - half if u > 0 else 0
        hi = (u + 1) * sub + half if u < n_sub - 1 else tq
        parts = [cur_ref[0, r, lo:hi, cols]]
        if u == 0:
            parts.insert(0, prev_ref[0, r, :, cols])
        if u == n_sub - 1:
            parts.append(next_ref[0, r, :, cols])
        if key_pad:
            parts.append(jnp.zeros((key_pad, HEAD_DIM), cur_ref.dtype))
        return parts[0] if len(parts) == 1 else jnp.concatenate(parts, axis=0)

    for r in range(dil):
        for u in range(n_sub):
            rows = pl.ds(u * sub * dil + r, sub, stride=dil) if dil > 1 else pl.ds(u * sub, sub)
            lse_tile = sl[rows, :]
            for h in range(hps):
                cols = slice(h * HEAD_DIM, (h + 1) * HEAD_DIM)
                q = q_ref[0, r, u * sub:(u + 1) * sub, cols]
                k = keys(kp_ref, kc_ref, kn_ref, r, u, cols)
                v = keys(vp_ref, vc_ref, vn_ref, r, u, cols)
                s = lax.dot_general(q, k, (((1,), (1,)), ((), ())), preferred_element_type=F32)
                s = s * scale + absrel_f * neg_slopes[h]
                s = jnp.where(valids[u], s, NEG_INF)
                m = jnp.max(s, axis=-1, keepdims=True)
                p = jnp.exp(s - m)
                den = jnp.sum(p, axis=-1, keepdims=True)
                so[h, rows, :] = jnp.dot(p.astype(v.dtype), v, preferred_element_type=F32) / den
                lse_tile = jnp.where(lane == hh * hps + h, m + jnp.log(den), lse_tile)
            sl[rows, :] = lse_tile
    for h in range(hps):
        o_ref[0, :, h * HEAD_DIM:(h + 1) * HEAD_DIM] = so[h].astype(o_ref.dtype)
    lse_ref[0] = sl[...]


def band_attention(qkv, B, S, group, hpg, n_heads_total):
    window, dil = ATTN_PATTERNS[group]
    half = window // (2 * dil)
    W = hpg * HEAD_DIM
    L = S // dil
    tq = min(L, max(ATTN_Q_TILE, ATTN_ROW_TILE // dil))
    rows = tq * dil
    sub = min(ATTN_Q_TILE, tq)
    hps = max(1, hpg * ATTN_ROW_TILE // max(rows, ATTN_ROW_TILE))
    assert S % dil == 0 and L % tq == 0 and tq % sub == 0 and tq % half == 0 and half % 16 == 0 and hpg % hps == 0
    n_hh = hpg // hps
    Wb = hps * HEAD_DIM
    hb = tq // half
    n_hb = L // half
    s_all = 2.0 ** (-8.0 * np.arange(1, n_heads_total + 1) / n_heads_total)
    slopes = tuple(float(np.float32(v)) for v in s_all[group * hpg:(group + 1) * hpg])
    qv = qkv.reshape(B, dil, L, 3 * W)

    cur = lambda part: pl.BlockSpec((1, dil, tq, Wb), lambda b, i, hh: (b, 0, i, part * n_hh + hh))
    prv = lambda part: pl.BlockSpec(
        (1, dil, half, Wb), lambda b, i, hh: (b, 0, jnp.maximum(i * hb - 1, 0), part * n_hh + hh))
    nxt = lambda part: pl.BlockSpec(
        (1, dil, half, Wb), lambda b, i, hh: (b, 0, jnp.minimum((i + 1) * hb, n_hb - 1), part * n_hh + hh))
    o, lse = pl.pallas_call(
        functools.partial(_attn_kernel, tq=tq, sub=sub, half=half, hps=hps, dil=dil, cls_len=L, slopes=slopes),
        grid=(B, L // tq, n_hh),
        in_specs=[cur(0), prv(1), cur(1), nxt(1), prv(2), cur(2), nxt(2)],
        out_specs=[pl.BlockSpec((1, rows, Wb), lambda b, i, hh: (b, i, hh)),
                   pl.BlockSpec((1, rows, LANES), lambda b, i, hh: (b, i, 0))],
        out_shape=[jax.ShapeDtypeStruct((B, S, W), BF16), jax.ShapeDtypeStruct((B, S, LANES), F32)],
        scratch_shapes=[pltpu.VMEM((hps, rows, LANES), F32), pltpu.VMEM((rows, LANES), F32)],
        compiler_params=_params(("parallel", "arbitrary", "arbitrary")),
    )(qv, qv, qv, qv, qv, qv, qv)
    return o.reshape(B * S, W), lse.reshape(B * S, LANES)


def _dft_tables(n, cols=None):
    cols = n if cols is None else cols
    k = jnp.arange(n, dtype=jnp.int32)

    def thin(s):
        ang = ((k[:, None] * s[None, :]) % n).astype(F32) * np.float32(2.0 * np.pi / n)
        return jnp.cos(ang), jnp.sin(ang)

    m = 64
    if cols <= m or cols % m:
        return thin(jnp.arange(cols, dtype=jnp.int32))
    c_hi, s_hi = thin(jnp.arange(cols // m, dtype=jnp.int32) * m)
    c_lo, s_lo = thin(jnp.arange(m, dtype=jnp.int32))
    cos = c_hi[:, :, None] * c_lo[:, None, :] - s_hi[:, :, None] * s_lo[:, None, :]
    sin = s_hi[:, :, None] * c_lo[:, None, :] + c_hi[:, :, None] * s_lo[:, None, :]
    return cos.reshape(n, cols), sin.reshape(n, cols)


def _fourier_chan_kernel(xd_ref, xa_ref, xb_ref, xh_ref, cos_ref, sin_ref, pe_ref, po_ref, ph_ref):
    i = pl.program_id(2)
    ts = xd_ref.shape[0]
    row = lax.broadcasted_iota(jnp.int32, (ts, ts), 0)
    col = lax.broadcasted_iota(jnp.int32, (ts, ts), 1)
    flip = (col == ts - row).astype(xa_ref.dtype)
    rev = jnp.dot(flip, xa_ref[...], preferred_element_type=F32)
    first = lax.broadcasted_iota(jnp.int32, rev.shape, 0) == 0
    rev = jnp.where(first, xb_ref[0:1, :].astype(F32), rev)
    xd = xd_ref[...].astype(F32)
    xe = xd + jnp.where(first & (i == 0), 0.0, rev)
    xo = xd - rev
    pe_ref[0] = jnp.dot(xe.astype(BF16), cos_ref[...], preferred_element_type=F32).astype(pe_ref.dtype)
    po_ref[0] = jnp.dot(xo.astype(BF16), sin_ref[...], preferred_element_type=F32).astype(po_ref.dtype)

    @pl.when(i == 0)
    def _():
        ph_ref[0] = jnp.dot(xh_ref[...], cos_ref[...], preferred_element_type=F32).astype(ph_ref.dtype)


def fourier_channel_stage(fg, B, S, cg, cos_c, sin_c):
    half = S // 2
    ts = _tile(half, 256, 16)
    nt = half // ts
    n_all = N_FOURIER_GROUPS * cg
    sub = 16
    return pl.pallas_call(
        _fourier_chan_kernel,
        grid=(B, N_FOURIER_GROUPS, nt),
        in_specs=[pl.BlockSpec((ts, cg), lambda b, g, i: (b * (S // ts) + i, g)),
                  pl.BlockSpec((ts, cg), lambda b, g, i: (b * (S // ts) + (S // ts - 1 - i), g)),
                  pl.BlockSpec((sub, cg), lambda b, g, i: (b * (S // sub) + ((S - i * ts) % S) // sub, g)),
                  pl.BlockSpec((sub, cg), lambda b, g, i: (b * (S // sub) + half // sub, g)),
                  pl.BlockSpec((cg, cg), lambda b, g, i: (0, 0)),
                  pl.BlockSpec((cg, cg), lambda b, g, i: (0, 0))],
        out_specs=[pl.BlockSpec((1, ts, cg), lambda b, g, i: (b, i, g)),
                   pl.BlockSpec((1, ts, cg), lambda b, g, i: (b, i, g)),
                   pl.BlockSpec((1, sub, cg), lambda b, g, i: (b, 0, g))],
        out_shape=[jax.ShapeDtypeStruct((B, half, n_all), BF16), jax.ShapeDtypeStruct((B, half, n_all), BF16),
                   jax.ShapeDtypeStruct((B, sub, n_all), BF16)],
        compiler_params=_params(("parallel", "parallel", "arbitrary")),
    )(fg, fg, fg, fg, cos_c, sin_c)


def _fourier_seq_kernel(cs_ref, ss_ref, pe_ref, po_ref, ph_ref, o_ref, acc_ref, *, scale):
    k = pl.program_id(3)

    @pl.when(k == 0)
    def _():
        acc_ref[...] = jnp.zeros_like(acc_ref)

    acc_ref[...] += (jnp.dot(cs_ref[...], pe_ref[0], preferred_element_type=F32)
                     + jnp.dot(ss_ref[...], po_ref[0], preferred_element_type=F32))

    @pl.when(k == pl.num_programs(3) - 1)
    def _():
        tm = acc_ref.shape[0]
        freq = pl.program_id(1) * tm + lax.broadcasted_iota(jnp.int32, (tm, 1), 0)
        sign = (1 - 2 * (freq & 1)).astype(F32)
        o_ref[0] = ((acc_ref[...] + sign * ph_ref[0, 0:1, :].astype(F32)) * scale).astype(o_ref.dtype)


def fourier_sequence_stage(pe, po, ph, cos_s, neg_sin_s, S, scale):
    B, half, N = pe.shape
    tm, tn, tk = _tile(S, 1024), _tile(N, 1024), _tile(half, 2048)
    out = pl.pallas_call(
        functools.partial(_fourier_seq_kernel, scale=scale),
        grid=(B, S // tm, N // tn, half // tk),
        in_specs=[pl.BlockSpec((tm, tk), lambda b, i, j, k: (i, k)),
                  pl.BlockSpec((tm, tk), lambda b, i, j, k: (i, k)),
                  pl.BlockSpec((1, tk, tn), lambda b, i, j, k: (b, k, j)),
                  pl.BlockSpec((1, tk, tn), lambda b, i, j, k: (b, k, j)),
                  pl.BlockSpec((1, ph.shape[1], tn), lambda b, i, j, k: (b, 0, j))],
        out_specs=pl.BlockSpec((1, tm, tn), lambda b, i, j, k: (b, i, j)),
        out_shape=jax.ShapeDtypeStruct((B, S, N), BF16),
        scratch_shapes=[pltpu.VMEM((tm, tn), F32)],
        compiler_params=_params(("parallel", "parallel", "parallel", "arbitrary")),
    )(cos_s, neg_sin_s, pe, po, ph)
    return out.reshape(B * S, N)


def _merge_kernel(o0_ref, o1_ref, o2_ref, l0_ref, l1_ref, l2_ref, fr_ref, wa_ref, wf_ref, *refs, hpg, n_b):
    ga_refs, gf_refs = refs[:n_b], refs[n_b:2 * n_b]
    m_ref, oc_ref = refs[2 * n_b], refs[2 * n_b + 1]
    l0, l1, l2 = l0_ref[...], l1_ref[...], l2_ref[...]
    m = jnp.maximum(jnp.maximum(l0, l1), l2)
    e0, e1, e2 = jnp.exp(l0 - m), jnp.exp(l1 - m), jnp.exp(l2 - m)
    den = e0 + e1 + e2
    w0, w1, w2 = e0 / den, e1 / den, e2 / den
    for h in range(hpg):
        cols = slice(h * HEAD_DIM, (h + 1) * HEAD_DIM)
        oc = (w0[:, h:h + 1] * o0_ref[:, cols].astype(F32)
              + w1[:, h:h + 1] * o1_ref[:, cols].astype(F32)
              + w2[:, h:h + 1] * o2_ref[:, cols].astype(F32))
        oc_ref[:, cols] = oc.astype(oc_ref.dtype)
    oc_all = oc_ref[...]
    fr = fr_ref[...]
    bw = ga_refs[0].shape[1]
    for c in range(n_b):
        cols = slice(c * bw, (c + 1) * bw)
        attn = jnp.dot(oc_all, wa_ref[:, cols], preferred_element_type=F32)
        four = jnp.dot(fr, wf_ref[:, cols], preferred_element_type=F32)
        m_ref[:, cols] = (ga_refs[c][...].astype(F32) * attn + gf_refs[c][...].astype(F32) * four).astype(m_ref.dtype)


def merge_branches(os_, lses, fr, wa, wf, fg, gate_start, hpg):
    T, W = os_[0].shape
    D = wa.shape[1]
    FW = fr.shape[1]
    tm = _tile(T, 256)
    bw = math.gcd(D, gate_start)
    n_b, g0 = D // bw, gate_start // bw
    row = lambda w: pl.BlockSpec((tm, w), lambda i: (i, 0))
    resident = lambda r: pl.BlockSpec((r, D), lambda i: (0, 0), pipeline_mode=pl.Buffered(1))
    gate = lambda blk: pl.BlockSpec((tm, bw), lambda i: (i, blk))
    return pl.pallas_call(
        functools.partial(_merge_kernel, hpg=hpg, n_b=n_b),
        grid=(T // tm,),
        in_specs=[row(W), row(W), row(W), row(LANES), row(LANES), row(LANES), row(FW), resident(W), resident(FW)]
        + [gate(g0 + c) for c in range(2 * n_b)],
        out_specs=pl.BlockSpec((tm, D), lambda i: (i, 0)),
        out_shape=jax.ShapeDtypeStruct((T, D), BF16),
        scratch_shapes=[pltpu.VMEM((tm, W), BF16)],
        compiler_params=_params(("parallel",)),
    )(*os_, *lses, fr, wa, wf, *([fg] * (2 * n_b)))


def _out_proj_kernel(m_ref, w_ref, x_ref, o_ref):
    o_ref[...] = x_ref[...] + jnp.dot(m_ref[...], w_ref[...], preferred_element_type=F32)


def out_projection(merged, w, x):
    T, K = merged.shape
    N = w.shape[1]
    tm, tn = _tile(T, 1024), _tile(N, 512)
    return pl.pallas_call(
        _out_proj_kernel,
        grid=(T // tm, N // tn),
        in_specs=[pl.BlockSpec((tm, K), lambda i, j: (i, 0)),
                  pl.BlockSpec((K, tn), lambda i, j: (0, j)),
                  pl.BlockSpec((tm, tn), lambda i, j: (i, j))],
        out_specs=pl.BlockSpec((tm, tn), lambda i, j: (i, j)),
        out_shape=jax.ShapeDtypeStruct((T, N), F32),
        compiler_params=_params(("parallel", "arbitrary")),
    )(merged, w, x)


def _router_kernel(*refs, bounds):
    n_in = len(bounds) - 1
    i = pl.program_id(0)
    for k in range(n_in):
        @pl.when((i >= bounds[k]) & (i < bounds[k + 1]))
        def _(k=k):
            _route_tile(refs[k], *refs[n_in:])


def _route_tile(x_ref, g_ref, w_ref, b_ref, eid_ref, cw_ref, hp_ref):
    x = x_ref[...]
    ms = jnp.mean(x * x, axis=-1, keepdims=True)
    hf = x * lax.rsqrt(ms + RMS_EPS) * g_ref[...]
    h = hf.astype(BF16)
    half_d = hf.shape[1] // 2
    hp_ref[...] = _pack_pairs(hf[:, :half_d], hf[:, half_d:])
    lg = jnp.dot(h, w_ref[...], preferred_element_type=F32) + b_ref[...]
    lane = lax.broadcasted_iota(jnp.int32, lg.shape, 1)
    big = jnp.int32(LANES)
    in_grp = lane < N_EXPERT_GROUPS
    gl = jnp.where(in_grp, lg, -jnp.inf)
    gmax = jnp.max(gl, axis=-1, keepdims=True)
    gsel = jnp.min(jnp.where(gl == gmax, lane, big), axis=-1, keepdims=True)
    p_group = 1.0 / jnp.sum(jnp.where(in_grp, jnp.exp(gl - gmax), 0.0), axis=-1, keepdims=True)
    lo = N_EXPERT_GROUPS + gsel * EXPERTS_PER_GROUP
    in_sel = (lane >= lo) & (lane < lo + EXPERTS_PER_GROUP)
    el = jnp.where(in_sel, lg, -jnp.inf)
    t1 = jnp.max(el, axis=-1, keepdims=True)
    i1 = jnp.min(jnp.where(el == t1, lane, big), axis=-1, keepdims=True)
    el2 = jnp.where(lane == i1, -jnp.inf, el)
    t2 = jnp.max(el2, axis=-1, keepdims=True)
    i2 = jnp.min(jnp.where(el2 == t2, lane, big), axis=-1, keepdims=True)
    e21 = jnp.exp(t2 - t1)
    p1 = 1.0 / (1.0 + e21)
    p2 = e21 / (1.0 + e21)
    eid_ref[...] = jnp.where(lane == 0, i1 - N_EXPERT_GROUPS, jnp.where(lane == 1, i2 - N_EXPERT_GROUPS, 0))
    cw_ref[...] = jnp.where(lane == 0, p_group * p1, jnp.where(lane == 1, p_group * p2, 0.0))


def route(x1s, g, w_router, b_router, tm):
    D = x1s[0].shape[1]
    bounds = [0]
    for x1 in x1s:
        bounds.append(bounds[-1] + x1.shape[0] // tm)
    T = bounds[-1] * tm

    def x_spec(k):
        lo, n = bounds[k], bounds[k + 1] - bounds[k]
        return pl.BlockSpec((tm, D), lambda i: (jnp.clip(i - lo, 0, n - 1), 0))

    return pl.pallas_call(
        functools.partial(_router_kernel, bounds=tuple(bounds)),
        grid=(bounds[-1],),
        in_specs=[x_spec(k) for k in range(len(x1s))]
        + [pl.BlockSpec((1, D), lambda i: (0, 0)),
           pl.BlockSpec((D, LANES), lambda i: (0, 0)),
           pl.BlockSpec((1, LANES), lambda i: (0, 0))],
        out_specs=[pl.BlockSpec((tm, LANES), lambda i: (i, 0)), pl.BlockSpec((tm, LANES), lambda i: (i, 0)),
                   pl.BlockSpec((tm, D // 2), lambda i: (i, 0))],
        out_shape=[jax.ShapeDtypeStruct((T, LANES), jnp.int32), jax.ShapeDtypeStruct((T, LANES), F32),
                   jax.ShapeDtypeStruct((T, D // 2), U32)],
        compiler_params=_params(("arbitrary",)),
    )(*x1s, g.reshape(1, D).astype(F32), w_router, b_router)


def _expert_up_kernel(src_ref, te_ref, nused_ref, hp_hbm, wg_ref, wu_ref, a_ref, buf, xs, sem, *, tme):
    i = pl.program_id(0)
    n_used = nused_ref[0]
    half_d = xs.shape[1] // 2

    def row_copy(tile, slot, r):
        return pltpu.make_async_copy(hp_hbm.at[pl.ds(src_ref[tile * tme + r], 1), :],
                                     buf.at[slot, pl.ds(r, 1), :], sem.at[slot])

    def drain(tile, slot):
        def body(r, c):
            row_copy(tile, slot, r).wait()
            return c
        lax.fori_loop(0, tme, body, 0, unroll=8)

    @pl.when(i == 0)
    def _():
        def body(r, c):
            row_copy(0, 0, r).start()
            return c
        lax.fori_loop(0, tme, body, 0, unroll=8)

    @pl.when(i < n_used)
    def _():
        slot = i % 2
        drain(i, slot)
        lo, hi = _unpack_pairs(buf[slot])
        xs[:, :half_d] = lo.astype(xs.dtype)
        xs[:, half_d:] = hi.astype(xs.dtype)
        nxt = jnp.minimum(i + 1, n_used - 1)
        for r in range(tme):
            row_copy(nxt, 1 - slot, r).start(priority=r % 2)
        x = xs[...]
        gate = jnp.dot(x, wg_ref[0], preferred_element_type=F32)
        up = jnp.dot(x, wu_ref[0], preferred_element_type=F32)
        a_ref[...] = (jax.nn.silu(gate) * up).astype(a_ref.dtype)

    @pl.when(i == n_used - 1)
    def _():
        drain(i, 1 - i % 2)

    @pl.when(i >= n_used)
    def _():
        a_ref[...] = jnp.zeros_like(a_ref)


def _expert_down_kernel(dst_ref, te_ref, nused_ref, a_ref, wd_ref, yt_hbm, ybuf, sem, *, tme, n_split, n_tiles):
    i = pl.program_id(0)
    n_used = nused_ref[0]
    last = n_tiles - 1
    half_d = ybuf.shape[2]
    wc = half_d // n_split

    def row_copy(tile, r):
        slot = (tile + 3) % 3
        return pltpu.make_async_copy(ybuf.at[slot, pl.ds(r, 1), :],
                                     yt_hbm.at[pl.ds(dst_ref[(tile + 1) * tme + r], 1), :], sem.at[slot])

    def drain(tile):
        def body(r, c):
            row_copy(tile, r).wait()
            return c
        lax.fori_loop(0, tme, body, 0, unroll=8)

    @pl.when(i == 0)
    def _():
        ybuf[2] = jnp.zeros(ybuf.shape[1:], ybuf.dtype)

    @pl.when((i >= 2) & (i - 3 < n_used))
    def _():
        drain(i - 3)

    @pl.when(i < n_used)
    def _():
        a = a_ref[...]
        for r in range(tme):
            row_copy(i - 1, r).start(priority=r % 2)
        for k in range(n_split):
            lo = jnp.dot(a, wd_ref[0, :, k * wc:(k + 1) * wc], preferred_element_type=F32)
            hi = jnp.dot(a, wd_ref[0, :, half_d + k * wc:half_d + (k + 1) * wc], preferred_element_type=F32)
            ybuf[i % 3, :, k * wc:(k + 1) * wc] = _pack_pairs(lo, hi)

    @pl.when(i == n_used)
    def _():
        def body(r, c):
            row_copy(i - 1, r).start()
            return c
        lax.fori_loop(0, tme, body, 0, unroll=8)

    @pl.when((i == last) & (last - 2 < n_used))
    def _():
        drain(last - 2)

    @pl.when((i == last) & (last - 1 < n_used))
    def _():
        drain(last - 1)


def expert_ffn(hp, w_gate, w_up, w_down, src, dst, tile_expert, n_used, n_tiles, tme, n_rows_out):
    D = hp.shape[1] * 2
    F = w_gate.shape[2]
    P = n_tiles * tme
    act = pl.pallas_call(
        functools.partial(_expert_up_kernel, tme=tme),
        grid_spec=pltpu.PrefetchScalarGridSpec(
            num_scalar_prefetch=3,
            grid=(n_tiles,),
            in_specs=[pl.BlockSpec(memory_space=pl.ANY),
                      pl.BlockSpec((1, D, F), lambda i, s, te, n: (te[i], 0, 0)),
                      pl.BlockSpec((1, D, F), lambda i, s, te, n: (te[i], 0, 0))],
            out_specs=pl.BlockSpec((tme, F), lambda i, s, te, n: (i, 0)),
            scratch_shapes=[pltpu.VMEM((2, tme, D // 2), U32), pltpu.VMEM((tme, D), BF16),
                            pltpu.SemaphoreType.DMA((2,))],
        ),
        out_shape=jax.ShapeDtypeStruct((P, F), BF16),
        compiler_params=_params(("arbitrary",)),
    )(src, tile_expert, n_used, hp, w_gate, w_up)
    n_split = max(1, D // 1024)
    spare = n_rows_out - tme + jnp.arange(tme, dtype=jnp.int32)
    return pl.pallas_call(
        functools.partial(_expert_down_kernel, tme=tme, n_split=n_split, n_tiles=n_tiles),
        grid_spec=pltpu.PrefetchScalarGridSpec(
            num_scalar_prefetch=3,
            grid=(n_tiles,),
            in_specs=[pl.BlockSpec((tme, F), lambda i, d, te, n: (i, 0)),
                      pl.BlockSpec((1, F, D), lambda i, d, te, n: (te[i], 0, 0))],
            out_specs=pl.BlockSpec(memory_space=pl.ANY),
            scratch_shapes=[pltpu.VMEM((3, tme, D // 2), U32), pltpu.SemaphoreType.DMA((3,))],
        ),
        out_shape=jax.ShapeDtypeStruct((n_rows_out, D // 2), U32),
        compiler_params=_params(("arbitrary",)),
    )(jnp.concatenate([spare, dst]), tile_expert, n_used, act, w_down)


def _final_kernel(x_ref, cw_ref, y0_ref, y1_ref, g_ref, o_ref):
    half_d = y0_ref.shape[1]
    cw0 = cw_ref[:, 0:1]
    cw1 = cw_ref[:, 1:2]
    lo0, hi0 = _unpack_pairs(y0_ref[...])
    lo1, hi1 = _unpack_pairs(y1_ref[...])
    xl = x_ref[:, :half_d] + (cw0 * lo0 + cw1 * lo1)
    xh = x_ref[:, half_d:] + (cw0 * hi0 + cw1 * hi1)
    ss = jnp.sum(xl * xl, axis=-1, keepdims=True) + jnp.sum(xh * xh, axis=-1, keepdims=True)
    inv = lax.rsqrt(ss / (2 * half_d) + RMS_EPS)
    o_ref[:, :half_d] = xl * inv * g_ref[:, :half_d]
    o_ref[:, half_d:] = xh * inv * g_ref[:, half_d:]


def final_combine(x1, cw, yt, g, tm, tile0, n_tok_tiles):
    T, D = x1.shape
    return pl.pallas_call(
        _final_kernel,
        grid=(T // tm,),
        in_specs=[pl.BlockSpec((tm, D), lambda i: (i, 0)),
                  pl.BlockSpec((tm, LANES), lambda i: (tile0 + i, 0)),
                  pl.BlockSpec((tm, D // 2), lambda i: (tile0 + i, 0)),
                  pl.BlockSpec((tm, D // 2), lambda i: (n_tok_tiles + tile0 + i, 0)),
                  pl.BlockSpec((1, D), lambda i: (0, 0))],
        out_specs=pl.BlockSpec((tm, D), lambda i: (i, 0)),
        out_shape=jax.ShapeDtypeStruct((T, D), F32),
        compiler_params=_params(("parallel",)),
    )(x1, cw, yt, yt, g.reshape(1, D).astype(F32))


def _sorted_layout(eid, tme):
    T = eid.shape[0]
    n_tiles = (2 * T) // tme + N_EXPERTS
    P = n_tiles * tme
    flat_e = eid.reshape(-1)
    onehot = (flat_e[:, None] == jnp.arange(N_EXPERTS, dtype=jnp.int32)[None, :]).astype(jnp.int32)
    csum = jnp.cumsum(onehot, axis=0)
    rank = jnp.take_along_axis(csum, flat_e[:, None], axis=1)[:, 0] - 1
    counts = csum[-1]
    padded = ((counts + tme - 1) // tme) * tme
    ends = jnp.cumsum(padded)
    pos = (ends - padded)[flat_e] + rank
    spare = 2 * T + jnp.arange(P, dtype=jnp.int32) % tme
    a = jnp.arange(2 * T, dtype=jnp.int32)
    dst = spare.at[pos].set((a % 2) * T + a // 2)
    src = jnp.where(dst < 2 * T, dst % T, 0)
    n_used = (ends[-1] // tme).astype(jnp.int32)
    tile_start = jnp.minimum(jnp.arange(n_tiles, dtype=jnp.int32), n_used - 1) * tme
    tile_expert = jnp.minimum(jnp.searchsorted(ends, tile_start, side="right"), N_EXPERTS - 1).astype(jnp.int32)
    return src, dst, tile_expert, n_used.reshape(1), n_tiles


def _encoder_trunk(x, wts):
    B, S, D = x.shape
    T = B * S
    hpg, cg = wts["hpg"], wts["cg"]
    n_grp = len(ATTN_PATTERNS)
    W = hpg * HEAD_DIM
    f_start = 3 * n_grp * W
    f_width = N_FOURIER_GROUPS * cg
    x2d = x.reshape(T, D)

    dils = sorted({d for _, d in ATTN_PATTERNS if d > 1})
    h, h_cm = rmsnorm_cast(x2d, wts["attn_norm_g"], B, S, dils)
    h_cm[1] = h

    outs, lses = [], []
    for g, (_, dil) in enumerate(ATTN_PATTERNS):
        qkv = in_projection(h_cm[dil], wts["w_in"], wts["b_full"], W, 3, lambda j, g=g: j * n_grp + g)
        o_g, lse_g = band_attention(qkv, B, S, g, hpg, n_grp * hpg)
        outs.append(o_g)
        lses.append(lse_g)

    tn = _tile(math.gcd(math.gcd(f_start, f_width), 2 * D), 1024)
    c0 = f_start // tn
    fg = in_projection(h, wts["w_in"], wts["b_full"], tn, (f_width + 2 * D) // tn, lambda j: c0 + j,
                       gate_tile0=f_width // tn)

    pe, po, ph = fourier_channel_stage(fg, B, S, cg, wts["cos_c"], wts["sin_c"])
    cos_s, sin_s = _dft_tables(S, S // 2)
    fr = fourier_sequence_stage(pe, po, ph, cos_s.astype(BF16), (-sin_s).astype(BF16), S,
                                float(1.0 / math.sqrt(S * cg)))

    merged = merge_branches(outs, lses, fr, wts["w_branch_attn"], wts["w_branch_fourier"], fg, f_width, hpg)
    return out_projection(merged, wts["w_out"], x2d)


def _moe_and_final_norm(x1s, wts):
    t_all = sum(x1.shape[0] for x1 in x1s)
    tm = _tile(math.gcd(*[x1.shape[0] for x1 in x1s]), 256, 8)
    eid, cw, hp = route(x1s, wts["ffn_norm_g"], wts["w_router"], wts["b_router"], tm)
    tme = _tile(2 * t_all, EXPERT_ROW_TILE, 8)
    src, dst, tile_expert, n_used, n_tiles = _sorted_layout(eid[:, :2], tme)
    yt = expert_ffn(hp, wts["w_expert_gate"], wts["w_expert_up"], wts["w_expert_down"], src, dst, tile_expert,
                    n_used, n_tiles, tme, 2 * t_all + tme)
    outs, tile0 = [], 0
    for x1 in x1s:
        outs.append(final_combine(x1, cw, yt, wts["final_norm_g"], tm, tile0, t_all // tm))
        tile0 += x1.shape[0] // tm
    return outs


def kernel(x_prompt, x_sample, attn_norm_g, w_in, w_branch_attn, w_branch_fourier, b_gate, w_out, ffn_norm_g, w_router_group, b_router_group, w_router_expert, b_router_expert, w_expert_gate, w_expert_up, w_expert_down, final_norm_g):
    assert w_in.shape[0] == 1, "the final norm is fused into the layer's last kernel: one layer only"
    l = 0
    D = x_prompt.shape[-1]
    hpg = w_branch_attn.shape[1] // HEAD_DIM
    cg = w_branch_fourier.shape[1] // N_FOURIER_GROUPS
    in_width = w_in.shape[2]
    n_gate = b_gate.shape[1]
    cos_c, sin_c = _dft_tables(cg)
    w_r = jnp.concatenate(
        [w_router_group[l], jnp.transpose(w_router_expert[l], (1, 0, 2)).reshape(D, N_EXPERTS)], axis=1)
    b_r = jnp.concatenate([b_router_group[l], b_router_expert[l].reshape(-1)])
    pad = LANES - w_r.shape[1]
    wts = dict(
        hpg=hpg, cg=cg, cos_c=cos_c.astype(BF16), sin_c=sin_c.astype(BF16),
        attn_norm_g=attn_norm_g[l], ffn_norm_g=ffn_norm_g[l], final_norm_g=final_norm_g,
        w_in=w_in[l].astype(BF16),
        b_full=jnp.concatenate([jnp.zeros((in_width - n_gate,), F32), b_gate[l].astype(F32)]).reshape(1, in_width),
        w_branch_attn=w_branch_attn[l].astype(BF16),
        w_branch_fourier=w_branch_fourier[l].astype(BF16),
        w_out=w_out[l].astype(BF16),
        w_router=jnp.pad(w_r, ((0, 0), (0, pad))).astype(BF16),
        b_router=jnp.pad(b_r, (0, pad)).reshape(1, LANES).astype(F32),
        w_expert_gate=w_expert_gate[l].astype(BF16),
        w_expert_up=w_expert_up[l].astype(BF16),
        w_expert_down=w_expert_down[l].astype(BF16),
    )
    xs = (x_prompt, x_sample)
    outs = _moe_and_final_norm([_encoder_trunk(x, wts) for x in xs], wts)
    return tuple(o.reshape(x.shape) for o, x in zip(outs, xs))
```

```python
import functools
import math

import numpy as np
import jax
import jax.numpy as jnp
from jax import lax
from jax.experimental import pallas as pl
from jax.experimental.pallas import tpu as pltpu

F32 = jnp.float32
BF16 = jnp.bfloat16
U32 = jnp.uint32

RMS_EPS = 1e-6
NEG_INF = -1e30
HEAD_DIM = 128
ATTN_PATTERNS = ((128, 1), (512, 4), (2048, 16))
N_FOURIER_GROUPS = 4
N_EXPERT_GROUPS = 4
EXPERTS_PER_GROUP = 4
N_EXPERTS = N_EXPERT_GROUPS * EXPERTS_PER_GROUP
LANES = 128
V7X_VMEM_LIMIT = 56 * 1024 * 1024
ATTN_Q_TILE = 128
ATTN_ROW_TILE = 1024
EXPERT_ROW_TILE = 512
HI_MASK = 0xFFFF0000


def _tile(n, pref, mult=LANES):
    if n <= pref:
        return n
    t = (pref // mult) * mult
    while t >= mult:
        if n % t == 0:
            return t
        t -= mult
    raise ValueError(f"no tile for {n} <= {pref}")


def _params(sem, vmem=V7X_VMEM_LIMIT):
    return pltpu.CompilerParams(dimension_semantics=sem, vmem_limit_bytes=vmem)


def _pack_pairs(lo, hi):
    a = pltpu.bitcast(lo.astype(BF16).astype(F32), U32) >> 16
    b = pltpu.bitcast(hi.astype(BF16).astype(F32), U32) & jnp.uint32(HI_MASK)
    return a | b


def _unpack_pairs(w):
    return pltpu.bitcast(w << 16, F32), pltpu.bitcast(w & jnp.uint32(HI_MASK), F32)


def _rms_kernel(x_ref, g_ref, o_ref, *cm_refs, dils):
    x = x_ref[...]
    ms = jnp.mean(x * x, axis=-1, keepdims=True)
    h = (x * lax.rsqrt(ms + RMS_EPS) * g_ref[...]).astype(o_ref.dtype)
    o_ref[...] = h
    tm = x.shape[0]
    row = lax.broadcasted_iota(jnp.int32, (tm, tm), 0)
    col = lax.broadcasted_iota(jnp.int32, (tm, tm), 1)
    for cm_ref, d in zip(cm_refs, dils):
        n = tm // d
        perm = (col == (row % n) * d + row // n).astype(h.dtype)
        hp = jnp.dot(perm, h, preferred_element_type=F32).astype(cm_ref.dtype)
        for r in range(d):
            cm_ref[0, r] = hp[r * n:(r + 1) * n]


def rmsnorm_cast(x, g, B, S, dils):
    T, D = x.shape
    tm = _tile(S, 256, 16 * max(dils, default=1))
    spb = S // tm
    outs = pl.pallas_call(
        functools.partial(_rms_kernel, dils=tuple(dils)),
        grid=(T // tm,),
        in_specs=[pl.BlockSpec((tm, D), lambda i: (i, 0)), pl.BlockSpec((1, D), lambda i: (0, 0))],
        out_specs=[pl.BlockSpec((tm, D), lambda i: (i, 0))]
        + [pl.BlockSpec((1, d, tm // d, D), lambda i: (i // spb, 0, i % spb, 0)) for d in dils],
        out_shape=[jax.ShapeDtypeStruct((T, D), BF16)]
        + [jax.ShapeDtypeStruct((B, d, S // d, D), BF16) for d in dils],
        compiler_params=_params(("parallel",)),
    )(x, g.reshape(1, D).astype(F32))
    return outs[0], {d: o.reshape(T, D) for d, o in zip(dils, outs[1:])}


def _in_proj_kernel(h_ref, w_ref, b_ref, o_ref, *, gate_tile0):
    acc = jnp.dot(h_ref[...], w_ref[...], preferred_element_type=F32)
    if gate_tile0 is None:
        o_ref[...] = acc.astype(o_ref.dtype)
        return
    j = pl.program_id(1)

    @pl.when(j < gate_tile0)
    def _():
        o_ref[...] = acc.astype(o_ref.dtype)

    @pl.when(j >= gate_tile0)
    def _():
        o_ref[...] = jax.nn.sigmoid(acc + b_ref[...]).astype(o_ref.dtype)


def in_projection(h, w, b_full, tn, n_out, w_col, gate_tile0=None):
    T, K = h.shape
    tm = _tile(T, 1024)
    return pl.pallas_call(
        functools.partial(_in_proj_kernel, gate_tile0=gate_tile0),
        grid=(T // tm, n_out),
        in_specs=[pl.BlockSpec((tm, K), lambda i, j: (i, 0)),
                  pl.BlockSpec((K, tn), lambda i, j: (0, w_col(j))),
                  pl.BlockSpec((1, tn), lambda i, j: (0, w_col(j)))],
        out_specs=pl.BlockSpec((tm, tn), lambda i, j: (i, j)),
        out_shape=jax.ShapeDtypeStruct((T, n_out * tn), BF16),
        compiler_params=_params(("parallel", "arbitrary")),
    )(h, w, b_full)


def _attn_kernel(q_ref, kp_ref, kc_ref, kn_ref, vp_ref, vc_ref, vn_ref, o_ref, lse_ref, so, sl,
                 *, tq, sub, half, hps, dil, cls_len, slopes):
    i = pl.program_id(1)
    hh = pl.program_id(2)
    key_pad = -(sub + 2 * half) % LANES
    span = sub + 2 * half + key_pad
    scale = HEAD_DIM ** -0.5
    qidx = lax.broadcasted_iota(jnp.int32, (sub, span), 0)
    kidx = lax.broadcasted_iota(jnp.int32, (sub, span), 1)
    absrel = jnp.abs(kidx - half - qidx)
    absrel_f = absrel.astype(F32)
    lane = lax.broadcasted_iota(jnp.int32, (sub, LANES), 1)
    n_sub = tq // sub
    valids = []
    for u in range(n_sub):
        kpos = i * tq + (u * sub - half) + kidx
        valids.append((absrel <= half) & (kpos >= 0) & (kpos < cls_len))
    neg_slopes = []
    for h in range(hps):
        s_h = jnp.float32(0.0)
        for b in range(len(slopes) // hps):
            s_h = jnp.where(hh == b, jnp.float32(-slopes[b * hps + h] * dil), s_h)
        neg_slopes.append(s_h)

    @pl.when(hh == 0)
    def _():
        sl[...] = jnp.zeros_like(sl)

    def keys(prev_ref, cur_ref, next_ref, r, u, cols):
        lo = u * sub - half if u > 0 else 0
        hi = (u + 1) * sub + half if u < n_sub - 1 else tq
        parts = [cur_ref[0, r, lo:hi, cols]]
        if u == 0:
            parts.insert(0, prev_ref[0, r, :, cols])
        if u == n_sub - 1:
            parts.append(next_ref[0, r, :, cols])
        if key_pad:
            parts.append(jnp.zeros((key_pad, HEAD_DIM), cur_ref.dtype))
        return parts[0] if len(parts) == 1 else jnp.concatenate(parts, axis=0)

    for r in range(dil):
        for u in range(n_sub):
            rows = pl.ds(u * sub * dil + r, sub, stride=dil) if dil > 1 else pl.ds(u * sub, sub)
            lse_tile = sl[rows, :]
            for h in range(hps):
                cols = slice(h * HEAD_DIM, (h + 1) * HEAD_DIM)
                q = q_ref[0, r, u * sub:(u + 1) * sub, cols]
                k = keys(kp_ref, kc_ref, kn_ref, r, u, cols)
                v = keys(vp_ref, vc_ref, vn_ref, r, u, cols)
                s = lax.dot_general(q, k, (((1,), (1,)), ((), ())), preferred_element_type=F32)
                s = s * scale + absrel_f * neg_slopes[h]
                s = jnp.where(valids[u], s, NEG_INF)
                m = jnp.max(s, axis=-1, keepdims=True)
                p = jnp.exp(s - m)
                den = jnp.sum(p, axis=-1, keepdims=True)
                so[h, rows, :] = jnp.dot(p.astype(v.dtype), v, preferred_element_type=F32) / den
                lse_tile = jnp.where(lane == hh * hps + h, m + jnp.log(den), lse_tile)
            sl[rows, :] = lse_tile
    for h in range(hps):
        o_ref[0, :, h * HEAD_DIM:(h + 1) * HEAD_DIM] = so[h].astype(o_ref.dtype)
    lse_ref[0] = sl[...]


def band_attention(qkv, B, S, group, hpg, n_heads_total):
    window, dil = ATTN_PATTERNS[group]
    half = window // (2 * dil)
    W = hpg * HEAD_DIM
    L = S // dil
    tq = min(L, max(ATTN_Q_TILE, ATTN_ROW_TILE // dil))
    rows = tq * dil
    sub = min(ATTN_Q_TILE, tq)
    hps = max(1, hpg * ATTN_ROW_TILE // max(rows, ATTN_ROW_TILE))
    assert S % dil == 0 and L % tq == 0 and tq % sub == 0 and tq % half == 0 and half % 16 == 0 and hpg % hps == 0
    n_hh = hpg // hps
    Wb = hps * HEAD_DIM
    hb = tq // half
    n_hb = L // half
    s_all = 2.0 ** (-8.0 * np.arange(1, n_heads_total + 1) / n_heads_total)
    slopes = tuple(float(np.float32(v)) for v in s_all[group * hpg:(group + 1) * hpg])
    qv = qkv.reshape(B, dil, L, 3 * W)

    cur = lambda part: pl.BlockSpec((1, dil, tq, Wb), lambda b, i, hh: (b, 0, i, part * n_hh + hh))
    prv = lambda part: pl.BlockSpec(
        (1, dil, half, Wb), lambda b, i, hh: (b, 0, jnp.maximum(i * hb - 1, 0), part * n_hh + hh))
    nxt = lambda part: pl.BlockSpec(
        (1, dil, half, Wb), lambda b, i, hh: (b, 0, jnp.minimum((i + 1) * hb, n_hb - 1), part * n_hh + hh))
    o, lse = pl.pallas_call(
        functools.partial(_attn_kernel, tq=tq, sub=sub, half=half, hps=hps, dil=dil, cls_len=L, slopes=slopes),
        grid=(B, L // tq, n_hh),
        in_specs=[cur(0), prv(1), cur(1), nxt(1), prv(2), cur(2), nxt(2)],
        out_specs=[pl.BlockSpec((1, rows, Wb), lambda b, i, hh: (b, i, hh)),
                   pl.BlockSpec((1, rows, LANES), lambda b, i, hh: (b, i, 0))],
        out_shape=[jax.ShapeDtypeStruct((B, S, W), BF16), jax.ShapeDtypeStruct((B, S, LANES), F32)],
        scratch_shapes=[pltpu.VMEM((hps, rows, LANES), F32), pltpu.VMEM((rows, LANES), F32)],
        compiler_params=_params(("parallel", "arbitrary", "arbitrary")),
    )(qv, qv, qv, qv, qv, qv, qv)
    return o.reshape(B * S, W), lse.reshape(B * S, LANES)


def _dft_tables(n, cols=None):
    cols = n if cols is None else cols
    k = jnp.arange(n, dtype=jnp.int32)

    def thin(s):
        ang = ((k[:, None] * s[None, :]) % n).astype(F32) * np.float32(2.0 * np.pi / n)
        return jnp.cos(ang), jnp.sin(ang)

    m = 64
    if cols <= m or cols % m:
        return thin(jnp.arange(cols, dtype=jnp.int32))
    c_hi, s_hi = thin(jnp.arange(cols // m, dtype=jnp.int32) * m)
    c_lo, s_lo = thin(jnp.arange(m, dtype=jnp.int32))
    cos = c_hi[:, :, None] * c_lo[:, None, :] - s_hi[:, :, None] * s_lo[:, None, :]
    sin = s_hi[:, :, None] * c_lo[:, None, :] + c_hi[:, :, None] * s_lo[:, None, :]
    return cos.reshape(n, cols), sin.reshape(n, cols)


def _fourier_chan_kernel(xd_ref, xa_ref, xb_ref, xh_ref, cos_ref, sin_ref, pe_ref, po_ref, ph_ref):
    i = pl.program_id(2)
    ts = xd_ref.shape[0]
    row = lax.broadcasted_iota(jnp.int32, (ts, ts), 0)
    col = lax.broadcasted_iota(jnp.int32, (ts, ts), 1)
    flip = (col == ts - row).astype(xa_ref.dtype)
    rev = jnp.dot(flip, xa_ref[...], preferred_element_type=F32)
    first = lax.broadcasted_iota(jnp.int32, rev.shape, 0) == 0
    rev = jnp.where(first, xb_ref[0:1, :].astype(F32), rev)
    xd = xd_ref[...].astype(F32)
    xe = xd + jnp.where(first & (i == 0), 0.0, rev)
    xo = xd - rev
    pe_ref[0] = jnp.dot(xe.astype(BF16), cos_ref[...], preferred_element_type=F32).astype(pe_ref.dtype)
    po_ref[0] = jnp.dot(xo.astype(BF16), sin_ref[...], preferred_element_type=F32).astype(po_ref.dtype)

    @pl.when(i == 0)
    def _():
        ph_ref[0] = jnp.dot(xh_ref[...], cos_ref[...], preferred_element_type=F32).astype(ph_ref.dtype)


def fourier_channel_stage(fg, B, S, cg, cos_c, sin_c):
    half = S // 2
    ts = _tile(half, 256, 16)
    nt = half // ts
    n_all = N_FOURIER_GROUPS * cg
    sub = 16
    return pl.pallas_call(
        _fourier_chan_kernel,
        grid=(B, N_FOURIER_GROUPS, nt),
        in_specs=[pl.BlockSpec((ts, cg), lambda b, g, i: (b * (S // ts) + i, g)),
                  pl.BlockSpec((ts, cg), lambda b, g, i: (b * (S // ts) + (S // ts - 1 - i), g)),
                  pl.BlockSpec((sub, cg), lambda b, g, i: (b * (S // sub) + ((S - i * ts) % S) // sub, g)),
                  pl.BlockSpec((sub, cg), lambda b, g, i: (b * (S // sub) + half // sub, g)),
                  pl.BlockSpec((cg, cg), lambda b, g, i: (0, 0)),
                  pl.BlockSpec((cg, cg), lambda b, g, i: (0, 0))],
        out_specs=[pl.BlockSpec((1, ts, cg), lambda b, g, i: (b, i, g)),
                   pl.BlockSpec((1, ts, cg), lambda b, g, i: (b, i, g)),
                   pl.BlockSpec((1, sub, cg), lambda b, g, i: (b, 0, g))],
        out_shape=[jax.ShapeDtypeStruct((B, half, n_all), BF16), jax.ShapeDtypeStruct((B, half, n_all), BF16),
                   jax.ShapeDtypeStruct((B, sub, n_all), BF16)],
        compiler_params=_params(("parallel", "parallel", "arbitrary")),
    )(fg, fg, fg, fg, cos_c, sin_c)


def _fourier_seq_kernel(cs_ref, ss_ref, pe_ref, po_ref, ph_ref, o_ref, acc_ref, *, scale):
    k = pl.program_id(3)

    @pl.when(k == 0)
    def _():
        acc_ref[...] = jnp.zeros_like(acc_ref)

    acc_ref[...] += (jnp.dot(cs_ref[...], pe_ref[0], preferred_element_type=F32)
                     + jnp.dot(ss_ref[...], po_ref[0], preferred_element_type=F32))

    @pl.when(k == pl.num_programs(3) - 1)
    def _():
        tm = acc_ref.shape[0]
        freq = pl.program_id(1) * tm + lax.broadcasted_iota(jnp.int32, (tm, 1), 0)
        sign = (1 - 2 * (freq & 1)).astype(F32)
        o_ref[0] = ((acc_ref[...] + sign * ph_ref[0, 0:1, :].astype(F32)) * scale).astype(o_ref.dtype)


def fourier_sequence_stage(pe, po, ph, cos_s, neg_sin_s, S, scale):
    B, half, N = pe.shape
    tm, tn, tk = _tile(S, 1024), _tile(N, 1024), _tile(half, 2048)
    out = pl.pallas_call(
        functools.partial(_fourier_seq_kernel, scale=scale),
        grid=(B, S // tm, N // tn, half // tk),
        in_specs=[pl.BlockSpec((tm, tk), lambda b, i, j, k: (i, k)),
                  pl.BlockSpec((tm, tk), lambda b, i, j, k: (i, k)),
                  pl.BlockSpec((1, tk, tn), lambda b, i, j, k: (b, k, j)),
                  pl.BlockSpec((1, tk, tn), lambda b, i, j, k: (b, k, j)),
                  pl.BlockSpec((1, ph.shape[1], tn), lambda b, i, j, k: (b, 0, j))],
        out_specs=pl.BlockSpec((1, tm, tn), lambda b, i, j, k: (b, i, j)),
        out_shape=jax.ShapeDtypeStruct((B, S, N), BF16),
        scratch_shapes=[pltpu.VMEM((tm, tn), F32)],
        compiler_params=_params(("parallel", "parallel", "parallel", "arbitrary")),
    )(cos_s, neg_sin_s, pe, po, ph)
    return out.reshape(B * S, N)


def _merge_kernel(o0_ref, o1_ref, o2_ref, l0_ref, l1_ref, l2_ref, fr_ref, wa_ref, wf_ref, *refs, hpg, n_b):
    ga_refs, gf_refs = refs[:n_b], refs[n_b:2 * n_b]
    m_ref, oc_ref = refs[2 * n_b], refs[2 * n_b + 1]
    bw = ga_refs[0].shape[1]
    fr = fr_ref[...]
    fours = [jnp.dot(fr, wf_ref[:, c * bw:(c + 1) * bw], preferred_element_type=F32) for c in range(n_b)]
    l0, l1, l2 = l0_ref[...], l1_ref[...], l2_ref[...]
    m = jnp.maximum(jnp.maximum(l0, l1), l2)
    e0, e1, e2 = jnp.exp(l0 - m), jnp.exp(l1 - m), jnp.exp(l2 - m)
    den = e0 + e1 + e2
    w0, w1, w2 = e0 / den, e1 / den, e2 / den
    for h in range(hpg):
        cols = slice(h * HEAD_DIM, (h + 1) * HEAD_DIM)
        oc = (w0[:, h:h + 1] * o0_ref[:, cols].astype(F32)
              + w1[:, h:h + 1] * o1_ref[:, cols].astype(F32)
              + w2[:, h:h + 1] * o2_ref[:, cols].astype(F32))
        oc_ref[:, cols] = oc.astype(oc_ref.dtype)
    oc_all = oc_ref[...]
    for c in range(n_b):
        cols = slice(c * bw, (c + 1) * bw)
        attn = jnp.dot(oc_all, wa_ref[:, cols], preferred_element_type=F32)
        m_ref[:, cols] = (ga_refs[c][...].astype(F32) * attn
                          + gf_refs[c][...].astype(F32) * fours[c]).astype(m_ref.dtype)


def merge_branches(os_, lses, fr, wa, wf, fg, gate_start, hpg):
    T, W = os_[0].shape
    D = wa.shape[1]
    FW = fr.shape[1]
    tm = _tile(T, 256)
    bw = math.gcd(D, gate_start)
    n_b, g0 = D // bw, gate_start // bw
    row = lambda w: pl.BlockSpec((tm, w), lambda i: (i, 0))
    resident = lambda r: pl.BlockSpec((r, D), lambda i: (0, 0), pipeline_mode=pl.Buffered(1))
    gate = lambda blk: pl.BlockSpec((tm, bw), lambda i: (i, blk))
    return pl.pallas_call(
        functools.partial(_merge_kernel, hpg=hpg, n_b=n_b),
        grid=(T // tm,),
        in_specs=[row(W), row(W), row(W), row(LANES), row(LANES), row(LANES), row(FW), resident(W), resident(FW)]
        + [gate(g0 + c) for c in range(2 * n_b)],
        out_specs=pl.BlockSpec((tm, D), lambda i: (i, 0)),
        out_shape=jax.ShapeDtypeStruct((T, D), BF16),
        scratch_shapes=[pltpu.VMEM((tm, W), BF16)],
        compiler_params=_params(("parallel",)),
    )(*os_, *lses, fr, wa, wf, *([fg] * (2 * n_b)))


def _out_proj_kernel(m_ref, w_ref, x_ref, o_ref):
    o_ref[...] = x_ref[...] + jnp.dot(m_ref[...], w_ref[...], preferred_element_type=F32)


def out_projection(merged, w, x):
    T, K = merged.shape
    N = w.shape[1]
    tm, tn = _tile(T, 1024), _tile(N, 512)
    return pl.pallas_call(
        _out_proj_kernel,
        grid=(T // tm, N // tn),
        in_specs=[pl.BlockSpec((tm, K), lambda i, j: (i, 0)),
                  pl.BlockSpec((K, tn), lambda i, j: (0, j)),
                  pl.BlockSpec((tm, tn), lambda i, j: (i, j))],
        out_specs=pl.BlockSpec((tm, tn), lambda i, j: (i, j)),
        out_shape=jax.ShapeDtypeStruct((T, N), F32),
        compiler_params=_params(("parallel", "arbitrary")),
    )(merged, w, x)


def _router_kernel(*refs, bounds):
    n_in = len(bounds) - 1
    i = pl.program_id(0)
    for k in range(n_in):
        @pl.when((i >= bounds[k]) & (i < bounds[k + 1]))
        def _(k=k):
            _route_tile(refs[k], *refs[n_in:])


def _route_tile(x_ref, g_ref, w_ref, b_ref, eid_ref, cw_ref, hp_ref):
    x = x_ref[...]
    ms = jnp.mean(x * x, axis=-1, keepdims=True)
    hf = x * lax.rsqrt(ms + RMS_EPS) * g_ref[...]
    h = hf.astype(BF16)
    half_d = hf.shape[1] // 2
    packed = _pack_pairs(hf[:, :half_d], hf[:, half_d:])
    for c in range(hp_ref.shape[1]):
        hp_ref[:, c, :] = packed[:, c * LANES:(c + 1) * LANES]
    lg = jnp.dot(h, w_ref[...], preferred_element_type=F32) + b_ref[...]
    lane = lax.broadcasted_iota(jnp.int32, lg.shape, 1)
    big = jnp.int32(LANES)
    in_grp = lane < N_EXPERT_GROUPS
    gl = jnp.where(in_grp, lg, -jnp.inf)
    gmax = jnp.max(gl, axis=-1, keepdims=True)
    gsel = jnp.min(jnp.where(gl == gmax, lane, big), axis=-1, keepdims=True)
    p_group = 1.0 / jnp.sum(jnp.where(in_grp, jnp.exp(gl - gmax), 0.0), axis=-1, keepdims=True)
    lo = N_EXPERT_GROUPS + gsel * EXPERTS_PER_GROUP
    in_sel = (lane >= lo) & (lane < lo + EXPERTS_PER_GROUP)
    el = jnp.where(in_sel, lg, -jnp.inf)
    t1 = jnp.max(el, axis=-1, keepdims=True)
    i1 = jnp.min(jnp.where(el == t1, lane, big), axis=-1, keepdims=True)
    el2 = jnp.where(lane == i1, -jnp.inf, el)
    t2 = jnp.max(el2, axis=-1, keepdims=True)
    i2 = jnp.min(jnp.where(el2 == t2, lane, big), axis=-1, keepdims=True)
    e21 = jnp.exp(t2 - t1)
    p1 = 1.0 / (1.0 + e21)
    p2 = e21 / (1.0 + e21)
    eid_ref[...] = jnp.where(lane == 0, i1 - N_EXPERT_GROUPS, jnp.where(lane == 1, i2 - N_EXPERT_GROUPS, 0))
    cw_ref[...] = jnp.where(lane == 0, p_group * p1, jnp.where(lane == 1, p_group * p2, 0.0))


def route(x1s, g, w_router, b_router, tm):
    D = x1s[0].shape[1]
    bounds = [0]
    for x1 in x1s:
        bounds.append(bounds[-1] + x1.shape[0] // tm)
    T = bounds[-1] * tm

    def x_spec(k):
        lo, n = bounds[k], bounds[k + 1] - bounds[k]
        return pl.BlockSpec((tm, D), lambda i: (jnp.clip(i - lo, 0, n - 1), 0))

    return pl.pallas_call(
        functools.partial(_router_kernel, bounds=tuple(bounds)),
        grid=(bounds[-1],),
        in_specs=[x_spec(k) for k in range(len(x1s))]
        + [pl.BlockSpec((1, D), lambda i: (0, 0)),
           pl.BlockSpec((D, LANES), lambda i: (0, 0)),
           pl.BlockSpec((1, LANES), lambda i: (0, 0))],
        out_specs=[pl.BlockSpec((tm, LANES), lambda i: (i, 0)), pl.BlockSpec((tm, LANES), lambda i: (i, 0)),
                   pl.BlockSpec((tm, D // (2 * LANES), LANES), lambda i: (i, 0, 0))],
        out_shape=[jax.ShapeDtypeStruct((T, LANES), jnp.int32), jax.ShapeDtypeStruct((T, LANES), F32),
                   jax.ShapeDtypeStruct((T, D // (2 * LANES), LANES), U32)],
        compiler_params=_params(("arbitrary",)),
    )(*x1s, g.reshape(1, D).astype(F32), w_router, b_router)


def _expert_up_kernel(src_ref, te_ref, nused_ref, hp_hbm, wg_ref, wu_ref, a_ref, buf, sem, *, tme, nc, n_k):
    i = pl.program_id(0)
    n_used = nused_ref[0]
    half_d = nc * LANES

    def row_copy(tile, slot, r):
        return pltpu.make_async_copy(hp_hbm.at[src_ref[tile * tme + r]], buf.at[slot, pl.ds(r * nc, nc), :],
                                     sem.at[slot])

    def drain(tile, slot):
        def body(r, c):
            row_copy(tile, slot, r).wait()
            return c
        lax.fori_loop(0, tme, body, 0, unroll=8)

    @pl.when(i == 0)
    def _():
        def body(r, c):
            row_copy(0, 0, r).start()
            return c
        lax.fori_loop(0, tme, body, 0, unroll=8)

    @pl.when(i < n_used)
    def _():
        slot = i % 2
        drain(i, slot)
        nxt = jnp.minimum(i + 1, n_used - 1)
        for r in range(tme):
            row_copy(nxt, 1 - slot, r).start(priority=r % 2)
        cpk = nc // n_k
        wk = cpk * LANES
        gate = up = None
        for q in range(n_k):
            parts = [_unpack_pairs(buf[slot, pl.ds(q * cpk + cc, tme, stride=nc), :]) for cc in range(cpk)]
            lo = jnp.concatenate([p[0].astype(BF16) for p in parts], axis=1)
            hi = jnp.concatenate([p[1].astype(BF16) for p in parts], axis=1)
            rows_lo = slice(q * wk, (q + 1) * wk)
            rows_hi = slice(half_d + q * wk, half_d + (q + 1) * wk)
            g = (jnp.dot(lo, wg_ref[0, rows_lo, :], preferred_element_type=F32)
                 + jnp.dot(hi, wg_ref[0, rows_hi, :], preferred_element_type=F32))
            u = (jnp.dot(lo, wu_ref[0, rows_lo, :], preferred_element_type=F32)
                 + jnp.dot(hi, wu_ref[0, rows_hi, :], preferred_element_type=F32))
            gate = g if gate is None else gate + g
            up = u if up is None else up + u
        a_ref[...] = (jax.nn.silu(gate) * up).astype(a_ref.dtype)

    @pl.when(i == n_used - 1)
    def _():
        drain(i, 1 - i % 2)

    @pl.when(i >= n_used)
    def _():
        a_ref[...] = jnp.zeros_like(a_ref)


def _expert_down_kernel(dst_ref, te_ref, nused_ref, a_ref, wd_ref, yt_hbm, ybuf, sem, *, tme, n_split, n_tiles):
    i = pl.program_id(0)
    n_used = nused_ref[0]
    last = n_tiles - 1
    half_d = ybuf.shape[2]
    wc = half_d // n_split

    def row_copy(tile, r):
        slot = (tile + 3) % 3
        return pltpu.make_async_copy(ybuf.at[slot, pl.ds(r, 1), :],
                                     yt_hbm.at[pl.ds(dst_ref[(tile + 1) * tme + r], 1), :], sem.at[slot])

    def drain(tile):
        def body(r, c):
            row_copy(tile, r).wait()
            return c
        lax.fori_loop(0, tme, body, 0, unroll=8)

    @pl.when(i == 0)
    def _():
        ybuf[2] = jnp.zeros(ybuf.shape[1:], ybuf.dtype)

    @pl.when((i >= 2) & (i - 3 < n_used))
    def _():
        drain(i - 3)

    @pl.when(i < n_used)
    def _():
        a = a_ref[...]
        for r in range(tme):
            row_copy(i - 1, r).start(priority=r % 2)
        for k in range(n_split):
            lo = jnp.dot(a, wd_ref[0, :, k * wc:(k + 1) * wc], preferred_element_type=F32)
            hi = jnp.dot(a, wd_ref[0, :, half_d + k * wc:half_d + (k + 1) * wc], preferred_element_type=F32)
            ybuf[i % 3, :, k * wc:(k + 1) * wc] = _pack_pairs(lo, hi)

    @pl.when(i == n_used)
    def _():
        def body(r, c):
            row_copy(i - 1, r).start()
            return c
        lax.fori_loop(0, tme, body, 0, unroll=8)

    @pl.when((i == last) & (last - 2 < n_used))
    def _():
        drain(last - 2)

    @pl.when((i == last) & (last - 1 < n_used))
    def _():
        drain(last - 1)


def expert_ffn(hp, w_gate, w_up, w_down, src, dst, tile_expert, n_used, n_tiles, tme, n_rows_out):
    nc = hp.shape[1]
    D = nc * 2 * LANES
    F = w_gate.shape[2]
    P = n_tiles * tme
    act = pl.pallas_call(
        functools.partial(_expert_up_kernel, tme=tme, nc=nc, n_k=max(1, nc // 4)),
        grid_spec=pltpu.PrefetchScalarGridSpec(
            num_scalar_prefetch=3,
            grid=(n_tiles,),
            in_specs=[pl.BlockSpec(memory_space=pl.ANY),
                      pl.BlockSpec((1, D, F), lambda i, s, te, n: (te[i], 0, 0)),
                      pl.BlockSpec((1, D, F), lambda i, s, te, n: (te[i], 0, 0))],
            out_specs=pl.BlockSpec((tme, F), lambda i, s, te, n: (i, 0)),
            scratch_shapes=[pltpu.VMEM((2, tme * nc, LANES), U32), pltpu.SemaphoreType.DMA((2,))],
        ),
        out_shape=jax.ShapeDtypeStruct((P, F), BF16),
        compiler_params=_params(("arbitrary",)),
    )(src, tile_expert, n_used, hp, w_gate, w_up)
    n_split = max(1, D // 1024)
    spare = n_rows_out - tme + jnp.arange(tme, dtype=jnp.int32)
    return pl.pallas_call(
        functools.partial(_expert_down_kernel, tme=tme, n_split=n_split, n_tiles=n_tiles),
        grid_spec=pltpu.PrefetchScalarGridSpec(
            num_scalar_prefetch=3,
            grid=(n_tiles,),
            in_specs=[pl.BlockSpec((tme, F), lambda i, d, te, n: (i, 0)),
                      pl.BlockSpec((1, F, D), lambda i, d, te, n: (te[i], 0, 0))],
            out_specs=pl.BlockSpec(memory_space=pl.ANY),
            scratch_shapes=[pltpu.VMEM((3, tme, D // 2), U32), pltpu.SemaphoreType.DMA((3,))],
        ),
        out_shape=jax.ShapeDtypeStruct((n_rows_out, D // 2), U32),
        compiler_params=_params(("arbitrary",)),
    )(jnp.concatenate([spare, dst]), tile_expert, n_used, act, w_down)


def _final_kernel(x_ref, cw_ref, y0_ref, y1_ref, g_ref, o_ref):
    half_d = y0_ref.shape[1]
    cw0 = cw_ref[:, 0:1]
    cw1 = cw_ref[:, 1:2]
    lo0, hi0 = _unpack_pairs(y0_ref[...])
    lo1, hi1 = _unpack_pairs(y1_ref[...])
    xl = x_ref[:, :half_d] + (cw0 * lo0 + cw1 * lo1)
    xh = x_ref[:, half_d:] + (cw0 * hi0 + cw1 * hi1)
    ss = jnp.sum(xl * xl, axis=-1, keepdims=True) + jnp.sum(xh * xh, axis=-1, keepdims=True)
    inv = lax.rsqrt(ss / (2 * half_d) + RMS_EPS)
    o_ref[:, :half_d] = xl * inv * g_ref[:, :half_d]
    o_ref[:, half_d:] = xh * inv * g_ref[:, half_d:]


def final_combine(x1, cw, yt, g, tm, tile0, n_tok_tiles):
    T, D = x1.shape
    return pl.pallas_call(
        _final_kernel,
        grid=(T // tm,),
        in_specs=[pl.BlockSpec((tm, D), lambda i: (i, 0)),
                  pl.BlockSpec((tm, LANES), lambda i: (tile0 + i, 0)),
                  pl.BlockSpec((tm, D // 2), lambda i: (tile0 + i, 0)),
                  pl.BlockSpec((tm, D // 2), lambda i: (n_tok_tiles + tile0 + i, 0)),
                  pl.BlockSpec((1, D), lambda i: (0, 0))],
        out_specs=pl.BlockSpec((tm, D), lambda i: (i, 0)),
        out_shape=jax.ShapeDtypeStruct((T, D), F32),
        compiler_params=_params(("parallel",)),
    )(x1, cw, yt, yt, g.reshape(1, D).astype(F32))


def _sorted_layout(eid, tme):
    T = eid.shape[0]
    n_tiles = (2 * T) // tme + N_EXPERTS
    P = n_tiles * tme
    flat_e = eid.reshape(-1)
    onehot = (flat_e[:, None] == jnp.arange(N_EXPERTS, dtype=jnp.int32)[None, :]).astype(jnp.int32)
    csum = jnp.cumsum(onehot, axis=0)
    rank = jnp.take_along_axis(csum, flat_e[:, None], axis=1)[:, 0] - 1
    counts = csum[-1]
    padded = ((counts + tme - 1) // tme) * tme
    ends = jnp.cumsum(padded)
    pos = (ends - padded)[flat_e] + rank
    spare = 2 * T + jnp.arange(P, dtype=jnp.int32) % tme
    a = jnp.arange(2 * T, dtype=jnp.int32)
    dst = spare.at[pos].set((a % 2) * T + a // 2)
    src = jnp.where(dst < 2 * T, dst % T, 0)
    n_used = (ends[-1] // tme).astype(jnp.int32)
    tile_start = jnp.minimum(jnp.arange(n_tiles, dtype=jnp.int32), n_used - 1) * tme
    tile_expert = jnp.minimum(jnp.searchsorted(ends, tile_start, side="right"), N_EXPERTS - 1).astype(jnp.int32)
    return src, dst, tile_expert, n_used.reshape(1), n_tiles


def _encoder_trunk(x, wts):
    B, S, D = x.shape
    T = B * S
    hpg, cg = wts["hpg"], wts["cg"]
    n_grp = len(ATTN_PATTERNS)
    W = hpg * HEAD_DIM
    f_start = 3 * n_grp * W
    f_width = N_FOURIER_GROUPS * cg
    x2d = x.reshape(T, D)

    dils = sorted({d for _, d in ATTN_PATTERNS if d > 1})
    h, h_cm = rmsnorm_cast(x2d, wts["attn_norm_g"], B, S, dils)
    h_cm[1] = h

    outs, lses = [], []
    for g, (_, dil) in enumerate(ATTN_PATTERNS):
        qkv = in_projection(h_cm[dil], wts["w_in"], wts["b_full"], W, 3, lambda j, g=g: j * n_grp + g)
        o_g, lse_g = band_attention(qkv, B, S, g, hpg, n_grp * hpg)
        outs.append(o_g)
        lses.append(lse_g)

    tn = _tile(math.gcd(math.gcd(f_start, f_width), 2 * D), 1024)
    c0 = f_start // tn
    fg = in_projection(h, wts["w_in"], wts["b_full"], tn, (f_width + 2 * D) // tn, lambda j: c0 + j,
                       gate_tile0=f_width // tn)

    pe, po, ph = fourier_channel_stage(fg, B, S, cg, wts["cos_c"], wts["sin_c"])
    cos_s, sin_s = _dft_tables(S, S // 2)
    fr = fourier_sequence_stage(pe, po, ph, cos_s.astype(BF16), (-sin_s).astype(BF16), S,
                                float(1.0 / math.sqrt(S * cg)))

    merged = merge_branches(outs, lses, fr, wts["w_branch_attn"], wts["w_branch_fourier"], fg, f_width, hpg)
    return out_projection(merged, wts["w_out"], x2d)


def _moe_and_final_norm(x1s, wts):
    t_all = sum(x1.shape[0] for x1 in x1s)
    tm = _tile(math.gcd(*[x1.shape[0] for x1 in x1s]), 256, 8)
    eid, cw, hp = route(x1s, wts["ffn_norm_g"], wts["w_router"], wts["b_router"], tm)
    tme = _tile(2 * t_all, EXPERT_ROW_TILE, 8)
    src, dst, tile_expert, n_used, n_tiles = _sorted_layout(eid[:, :2], tme)
    yt = expert_ffn(hp, wts["w_expert_gate"], wts["w_expert_up"], wts["w_expert_down"], src, dst, tile_expert,
                    n_used, n_tiles, tme, 2 * t_all + tme)
    outs, tile0 = [], 0
    for x1 in x1s:
        outs.append(final_combine(x1, cw, yt, wts["final_norm_g"], tm, tile0, t_all // tm))
        tile0 += x1.shape[0] // tm
    return outs


def kernel(x_prompt, x_sample, attn_norm_g, w_in, w_branch_attn, w_branch_fourier, b_gate, w_out, ffn_norm_g, w_router_group, b_router_group, w_router_expert, b_router_expert, w_expert_gate, w_expert_up, w_expert_down, final_norm_g):
    assert w_in.shape[0] == 1, "the final norm is fused into the layer's last kernel: one layer only"
    l = 0
    D = x_prompt.shape[-1]
    hpg = w_branch_attn.shape[1] // HEAD_DIM
    cg = w_branch_fourier.shape[1] // N_FOURIER_GROUPS
    in_width = w_in.shape[2]
    n_gate = b_gate.shape[1]
    cos_c, sin_c = _dft_tables(cg)
    w_r = jnp.concatenate(
        [w_router_group[l], jnp.transpose(w_router_expert[l], (1, 0, 2)).reshape(D, N_EXPERTS)], axis=1)
    b_r = jnp.concatenate([b_router_group[l], b_router_expert[l].reshape(-1)])
    pad = LANES - w_r.shape[1]
    wts = dict(
        hpg=hpg, cg=cg, cos_c=cos_c.astype(BF16), sin_c=sin_c.astype(BF16),
        attn_norm_g=attn_norm_g[l], ffn_norm_g=ffn_norm_g[l], final_norm_g=final_norm_g,
        w_in=w_in[l].astype(BF16),
        b_full=jnp.concatenate([jnp.zeros((in_width - n_gate,), F32), b_gate[l].astype(F32)]).reshape(1, in_width),
        w_branch_attn=w_branch_attn[l].astype(BF16),
        w_branch_fourier=w_branch_fourier[l].astype(BF16),
        w_out=w_out[l].astype(BF16),
        w_router=jnp.pad(w_r, ((0, 0), (0, pad))).astype(BF16),
        b_router=jnp.pad(b_r, (0, pad)).reshape(1, LANES).astype(F32),
        w_expert_gate=w_expert_gate[l].astype(BF16),
        w_expert_up=w_expert_up[l].astype(BF16),
        w_expert_down=w_expert_down[l].astype(BF16),
    )
    xs = (x_prompt, x_sample)
    outs = _moe_and_final_norm([_encoder_trunk(x, wts) for x in xs], wts)
    return tuple(o.reshape(x.shape) for o, x in zip(outs, xs))
```

```python
import functools
import math

import numpy as np
import jax
import jax.numpy as jnp
from jax import lax
from jax.experimental import pallas as pl
from jax.experimental.pallas import tpu as pltpu

F32 = jnp.float32
BF16 = jnp.bfloat16
U32 = jnp.uint32

RMS_EPS = 1e-6
NEG_INF = -1e30
HEAD_DIM = 128
ATTN_PATTERNS = ((128, 1), (512, 4), (2048, 16))
N_FOURIER_GROUPS = 4
N_EXPERT_GROUPS = 4
EXPERTS_PER_GROUP = 4
N_EXPERTS = N_EXPERT_GROUPS * EXPERTS_PER_GROUP
LANES = 128
V7X_VMEM_LIMIT = 56 * 1024 * 1024
ATTN_Q_TILE = 128
ATTN_ROW_TILE = 1024
EXPERT_ROW_TILE = 512
HI_MASK = 0xFFFF0000


def _tile(n, pref, mult=LANES):
    if n <= pref:
        return n
    t = (pref // mult) * mult
    while t >= mult:
        if n % t == 0:
            return t
        t -= mult
    raise ValueError(f"no tile for {n} <= {pref}")


def _params(sem, vmem=V7X_VMEM_LIMIT):
    return pltpu.CompilerParams(dimension_semantics=sem, vmem_limit_bytes=vmem)


def _pack_pairs(lo, hi):
    a = pltpu.bitcast(lo.astype(BF16).astype(F32), U32) >> 16
    b = pltpu.bitcast(hi.astype(BF16).astype(F32), U32) & jnp.uint32(HI_MASK)
    return a | b


def _unpack_pairs(w):
    return pltpu.bitcast(w << 16, F32), pltpu.bitcast(w & jnp.uint32(HI_MASK), F32)


def _rms_kernel(x_ref, g_ref, o_ref, *cm_refs, dils):
    x = x_ref[...]
    ms = jnp.mean(x * x, axis=-1, keepdims=True)
    h = (x * lax.rsqrt(ms + RMS_EPS) * g_ref[...]).astype(o_ref.dtype)
    o_ref[...] = h
    tm = x.shape[0]
    row = lax.broadcasted_iota(jnp.int32, (tm, tm), 0)
    col = lax.broadcasted_iota(jnp.int32, (tm, tm), 1)
    for cm_ref, d in zip(cm_refs, dils):
        n = tm // d
        perm = (col == (row % n) * d + row // n).astype(h.dtype)
        hp = jnp.dot(perm, h, preferred_element_type=F32).astype(cm_ref.dtype)
        for r in range(d):
            cm_ref[0, r] = hp[r * n:(r + 1) * n]


def rmsnorm_cast(x, g, B, S, dils):
    T, D = x.shape
    tm = _tile(S, 256, 16 * max(dils, default=1))
    spb = S // tm
    outs = pl.pallas_call(
        functools.partial(_rms_kernel, dils=tuple(dils)),
        grid=(T // tm,),
        in_specs=[pl.BlockSpec((tm, D), lambda i: (i, 0)), pl.BlockSpec((1, D), lambda i: (0, 0))],
        out_specs=[pl.BlockSpec((tm, D), lambda i: (i, 0))]
        + [pl.BlockSpec((1, d, tm // d, D), lambda i: (i // spb, 0, i % spb, 0)) for d in dils],
        out_shape=[jax.ShapeDtypeStruct((T, D), BF16)]
        + [jax.ShapeDtypeStruct((B, d, S // d, D), BF16) for d in dils],
        compiler_params=_params(("parallel",)),
    )(x, g.reshape(1, D).astype(F32))
    return outs[0], {d: o.reshape(T, D) for d, o in zip(dils, outs[1:])}


def _in_proj_kernel(h_ref, w_ref, b_ref, o_ref, *, gate_tile0):
    acc = jnp.dot(h_ref[...], w_ref[...], preferred_element_type=F32)
    if gate_tile0 is None:
        o_ref[...] = acc.astype(o_ref.dtype)
        return
    j = pl.program_id(1)

    @pl.when(j < gate_tile0)
    def _():
        o_ref[...] = acc.astype(o_ref.dtype)

    @pl.when(j >= gate_tile0)
    def _():
        o_ref[...] = jax.nn.sigmoid(acc + b_ref[...]).astype(o_ref.dtype)


def in_projection(h, w, b_full, tn, n_out, w_col, gate_tile0=None):
    T, K = h.shape
    tm = _tile(T, 1024)
    return pl.pallas_call(
        functools.partial(_in_proj_kernel, gate_tile0=gate_tile0),
        grid=(T // tm, n_out),
        in_specs=[pl.BlockSpec((tm, K), lambda i, j: (i, 0)),
                  pl.BlockSpec((K, tn), lambda i, j: (0, w_col(j))),
                  pl.BlockSpec((1, tn), lambda i, j: (0, w_col(j)))],
        out_specs=pl.BlockSpec((tm, tn), lambda i, j: (i, j)),
        out_shape=jax.ShapeDtypeStruct((T, n_out * tn), BF16),
        compiler_params=_params(("parallel", "arbitrary")),
    )(h, w, b_full)


def _attn_kernel(q_ref, kp_ref, kc_ref, kn_ref, vp_ref, vc_ref, vn_ref, o_ref, lse_ref, so, sl,
                 *, tq, sub, half, hps, dil, cls_len, slopes):
    i = pl.program_id(1)
    hh = pl.program_id(2)
    key_pad = -(sub + 2 * half) % LANES
    span = sub + 2 * half + key_pad
    scale = HEAD_DIM ** -0.5
    qidx = lax.broadcasted_iota(jnp.int32, (sub, span), 0)
    kidx = lax.broadcasted_iota(jnp.int32, (sub, span), 1)
    absrel = jnp.abs(kidx - half - qidx)
    absrel_f = absrel.astype(F32)
    lane = lax.broadcasted_iota(jnp.int32, (sub, LANES), 1)
    n_sub = tq // sub
    valids = []
    for u in range(n_sub):
        kpos = i * tq + (u * sub - half) + kidx
        valids.append((absrel <= half) & (kpos >= 0) & (kpos < cls_len))
    neg_slopes = []
    for h in range(hps):
        s_h = jnp.float32(0.0)
        for b in range(len(slopes) // hps):
            s_h = jnp.where(hh == b, jnp.float32(-slopes[b * hps + h] * dil), s_h)
        neg_slopes.append(s_h)

    @pl.when(hh == 0)
    def _():
        sl[...] = jnp.zeros_like(sl)

    def keys(prev_ref, cur_ref, next_ref, r, u, cols):
        lo = u * sub - half if u > 0 else 0
        hi = (u + 1) * sub + half if u < n_sub - 1 else tq
        parts = [cur_ref[0, r, lo:hi, cols]]
        if u == 0:
            parts.insert(0, prev_ref[0, r, :, cols])
        if u == n_sub - 1:
            parts.append(next_ref[0, r, :, cols])
        if key_pad:
            parts.append(jnp.zeros((key_pad, HEAD_DIM), cur_ref.dtype))
        return parts[0] if len(parts) == 1 else jnp.concatenate(parts, axis=0)

    for r in range(dil):
        for u in range(n_sub):
            rows = pl.ds(u * sub * dil + r, sub, stride=dil) if dil > 1 else pl.ds(u * sub, sub)
            lse_tile = sl[rows, :]
            for h in range(hps):
                cols = slice(h * HEAD_DIM, (h + 1) * HEAD_DIM)
                q = q_ref[0, r, u * sub:(u + 1) * sub, cols]
                k = keys(kp_ref, kc_ref, kn_ref, r, u, cols)
                v = keys(vp_ref, vc_ref, vn_ref, r, u, cols)
                s = lax.dot_general(q, k, (((1,), (1,)), ((), ())), preferred_element_type=F32)
                s = s * scale + absrel_f * neg_slopes[h]
                s = jnp.where(valids[u], s, NEG_INF)
                m = jnp.max(s, axis=-1, keepdims=True)
                p = jnp.exp(s - m)
                den = jnp.sum(p, axis=-1, keepdims=True)
                so[h, rows, :] = jnp.dot(p.astype(v.dtype), v, preferred_element_type=F32) / den
                lse_tile = jnp.where(lane == hh * hps + h, m + jnp.log(den), lse_tile)
            sl[rows, :] = lse_tile
    for h in range(hps):
        o_ref[0, :, h * HEAD_DIM:(h + 1) * HEAD_DIM] = so[h].astype(o_ref.dtype)
    lse_ref[0] = sl[...]


def band_attention(qkv, B, S, group, hpg, n_heads_total):
    window, dil = ATTN_PATTERNS[group]
    half = window // (2 * dil)
    W = hpg * HEAD_DIM
    L = S // dil
    tq = min(L, max(ATTN_Q_TILE, ATTN_ROW_TILE // dil))
    rows = tq * dil
    sub = min(ATTN_Q_TILE, tq)
    hps = max(1, hpg * ATTN_ROW_TILE // max(rows, ATTN_ROW_TILE))
    assert S % dil == 0 and L % tq == 0 and tq % sub == 0 and tq % half == 0 and half % 16 == 0 and hpg % hps == 0
    n_hh = hpg // hps
    Wb = hps * HEAD_DIM
    hb = tq // half
    n_hb = L // half
    s_all = 2.0 ** (-8.0 * np.arange(1, n_heads_total + 1) / n_heads_total)
    slopes = tuple(float(np.float32(v)) for v in s_all[group * hpg:(group + 1) * hpg])
    qv = qkv.reshape(B, dil, L, 3 * W)

    cur = lambda part: pl.BlockSpec((1, dil, tq, Wb), lambda b, i, hh: (b, 0, i, part * n_hh + hh))
    prv = lambda part: pl.BlockSpec(
        (1, dil, half, Wb), lambda b, i, hh: (b, 0, jnp.maximum(i * hb - 1, 0), part * n_hh + hh))
    nxt = lambda part: pl.BlockSpec(
        (1, dil, half, Wb), lambda b, i, hh: (b, 0, jnp.minimum((i + 1) * hb, n_hb - 1), part * n_hh + hh))
    o, lse = pl.pallas_call(
        functools.partial(_attn_kernel, tq=tq, sub=sub, half=half, hps=hps, dil=dil, cls_len=L, slopes=slopes),
        grid=(B, L // tq, n_hh),
        in_specs=[cur(0), prv(1), cur(1), nxt(1), prv(2), cur(2), nxt(2)],
        out_specs=[pl.BlockSpec((1, rows, Wb), lambda b, i, hh: (b, i, hh)),
                   pl.BlockSpec((1, rows, LANES), lambda b, i, hh: (b, i, 0))],
        out_shape=[jax.ShapeDtypeStruct((B, S, W), BF16), jax.ShapeDtypeStruct((B, S, LANES), F32)],
        scratch_shapes=[pltpu.VMEM((hps, rows, LANES), F32), pltpu.VMEM((rows, LANES), F32)],
        compiler_params=_params(("parallel", "arbitrary", "arbitrary")),
    )(qv, qv, qv, qv, qv, qv, qv)
    return o.reshape(B * S, W), lse.reshape(B * S, LANES)


def _dft_tables(n, cols=None):
    cols = n if cols is None else cols
    k = jnp.arange(n, dtype=jnp.int32)

    def thin(s):
        ang = ((k[:, None] * s[None, :]) % n).astype(F32) * np.float32(2.0 * np.pi / n)
        return jnp.cos(ang), jnp.sin(ang)

    m = 64
    if cols <= m or cols % m:
        return thin(jnp.arange(cols, dtype=jnp.int32))
    c_hi, s_hi = thin(jnp.arange(cols // m, dtype=jnp.int32) * m)
    c_lo, s_lo = thin(jnp.arange(m, dtype=jnp.int32))
    cos = c_hi[:, :, None] * c_lo[:, None, :] - s_hi[:, :, None] * s_lo[:, None, :]
    sin = s_hi[:, :, None] * c_lo[:, None, :] + c_hi[:, :, None] * s_lo[:, None, :]
    return cos.reshape(n, cols), sin.reshape(n, cols)


def _fourier_chan_kernel(xd_ref, xa_ref, xb_ref, xh_ref, cos_ref, sin_ref, pe_ref, po_ref, ph_ref):
    i = pl.program_id(1)
    ts = xd_ref.shape[0]
    cg = cos_ref.shape[0]
    row = lax.broadcasted_iota(jnp.int32, (ts, ts), 0)
    col = lax.broadcasted_iota(jnp.int32, (ts, ts), 1)
    flip = (col == ts - row).astype(xa_ref.dtype)
    rev = jnp.dot(flip, xa_ref[...], preferred_element_type=F32)
    first = lax.broadcasted_iota(jnp.int32, rev.shape, 0) == 0
    rev = jnp.where(first, xb_ref[0:1, :].astype(F32), rev)
    xd = xd_ref[...].astype(F32)
    xe = xd + jnp.where(first & (i == 0), 0.0, rev)
    xo = xd - rev
    xe, xo = xe.astype(BF16), xo.astype(BF16)
    for g in range(N_FOURIER_GROUPS):
        cols = slice(g * cg, (g + 1) * cg)
        pe_ref[0, :, cols] = jnp.dot(xe[:, cols], cos_ref[...], preferred_element_type=F32).astype(pe_ref.dtype)
        po_ref[0, :, cols] = jnp.dot(xo[:, cols], sin_ref[...], preferred_element_type=F32).astype(po_ref.dtype)

    @pl.when(i == 0)
    def _():
        for g in range(N_FOURIER_GROUPS):
            cols = slice(g * cg, (g + 1) * cg)
            ph_ref[0, :, cols] = jnp.dot(xh_ref[:, cols], cos_ref[...],
                                         preferred_element_type=F32).astype(ph_ref.dtype)


def fourier_channel_stage(fg, B, S, cg, cos_c, sin_c):
    half = S // 2
    ts = _tile(half, 256, 16)
    nt = half // ts
    n_all = N_FOURIER_GROUPS * cg
    sub = 16
    return pl.pallas_call(
        _fourier_chan_kernel,
        grid=(B, nt),
        in_specs=[pl.BlockSpec((ts, n_all), lambda b, i: (b * (S // ts) + i, 0)),
                  pl.BlockSpec((ts, n_all), lambda b, i: (b * (S // ts) + (S // ts - 1 - i), 0)),
                  pl.BlockSpec((sub, n_all), lambda b, i: (b * (S // sub) + ((S - i * ts) % S) // sub, 0)),
                  pl.BlockSpec((sub, n_all), lambda b, i: (b * (S // sub) + half // sub, 0)),
                  pl.BlockSpec((cg, cg), lambda b, i: (0, 0)),
                  pl.BlockSpec((cg, cg), lambda b, i: (0, 0))],
        out_specs=[pl.BlockSpec((1, ts, n_all), lambda b, i: (b, i, 0)),
                   pl.BlockSpec((1, ts, n_all), lambda b, i: (b, i, 0)),
                   pl.BlockSpec((1, sub, n_all), lambda b, i: (b, 0, 0))],
        out_shape=[jax.ShapeDtypeStruct((B, half, n_all), BF16), jax.ShapeDtypeStruct((B, half, n_all), BF16),
                   jax.ShapeDtypeStruct((B, sub, n_all), BF16)],
        compiler_params=_params(("parallel", "arbitrary")),
    )(fg, fg, fg, fg, cos_c, sin_c)


def _fourier_seq_kernel(cs_ref, ss_ref, pe_ref, po_ref, ph_ref, o_ref, acc_ref, *, scale):
    k = pl.program_id(3)

    @pl.when(k == 0)
    def _():
        acc_ref[...] = jnp.zeros_like(acc_ref)

    acc_ref[...] += (jnp.dot(cs_ref[...], pe_ref[0], preferred_element_type=F32)
                     + jnp.dot(ss_ref[...], po_ref[0], preferred_element_type=F32))

    @pl.when(k == pl.num_programs(3) - 1)
    def _():
        tm = acc_ref.shape[0]
        freq = pl.program_id(1) * tm + lax.broadcasted_iota(jnp.int32, (tm, 1), 0)
        sign = (1 - 2 * (freq & 1)).astype(F32)
        o_ref[0] = ((acc_ref[...] + sign * ph_ref[0, 0:1, :].astype(F32)) * scale).astype(o_ref.dtype)


def fourier_sequence_stage(pe, po, ph, cos_s, neg_sin_s, S, scale):
    B, half, N = pe.shape
    tm, tn, tk = _tile(S, 1024), _tile(N, 1024), _tile(half, 2048)
    out = pl.pallas_call(
        functools.partial(_fourier_seq_kernel, scale=scale),
        grid=(B, S // tm, N // tn, half // tk),
        in_specs=[pl.BlockSpec((tm, tk), lambda b, i, j, k: (i, k)),
                  pl.BlockSpec((tm, tk), lambda b, i, j, k: (i, k)),
                  pl.BlockSpec((1, tk, tn), lambda b, i, j, k: (b, k, j)),
                  pl.BlockSpec((1, tk, tn), lambda b, i, j, k: (b, k, j)),
                  pl.BlockSpec((1, ph.shape[1], tn), lambda b, i, j, k: (b, 0, j))],
        out_specs=pl.BlockSpec((1, tm, tn), lambda b, i, j, k: (b, i, j)),
        out_shape=jax.ShapeDtypeStruct((B, S, N), BF16),
        scratch_shapes=[pltpu.VMEM((tm, tn), F32)],
        compiler_params=_params(("parallel", "parallel", "parallel", "arbitrary")),
    )(cos_s, neg_sin_s, pe, po, ph)
    return out.reshape(B * S, N)


def _merge_kernel(o0_ref, o1_ref, o2_ref, l0_ref, l1_ref, l2_ref, fr_ref, wa_ref, wf_ref, *refs, hpg, n_b):
    ga_refs, gf_refs = refs[:n_b], refs[n_b:2 * n_b]
    m_ref, oc_ref = refs[2 * n_b], refs[2 * n_b + 1]
    bw = ga_refs[0].shape[1]
    fr = fr_ref[...]
    fours = [jnp.dot(fr, wf_ref[:, c * bw:(c + 1) * bw], preferred_element_type=F32) for c in range(n_b)]
    l0, l1, l2 = l0_ref[...], l1_ref[...], l2_ref[...]
    m = jnp.maximum(jnp.maximum(l0, l1), l2)
    e0, e1, e2 = jnp.exp(l0 - m), jnp.exp(l1 - m), jnp.exp(l2 - m)
    den = e0 + e1 + e2
    w0, w1, w2 = e0 / den, e1 / den, e2 / den
    for h in range(hpg):
        cols = slice(h * HEAD_DIM, (h + 1) * HEAD_DIM)
        oc = (w0[:, h:h + 1] * o0_ref[:, cols].astype(F32)
              + w1[:, h:h + 1] * o1_ref[:, cols].astype(F32)
              + w2[:, h:h + 1] * o2_ref[:, cols].astype(F32))
        oc_ref[:, cols] = oc.astype(oc_ref.dtype)
    oc_all = oc_ref[...]
    for c in range(n_b):
        cols = slice(c * bw, (c + 1) * bw)
        attn = jnp.dot(oc_all, wa_ref[:, cols], preferred_element_type=F32)
        m_ref[:, cols] = (ga_refs[c][...].astype(F32) * attn
                          + gf_refs[c][...].astype(F32) * fours[c]).astype(m_ref.dtype)


def merge_branches(os_, lses, fr, wa, wf, fg, gate_start, hpg):
    T, W = os_[0].shape
    D = wa.shape[1]
    FW = fr.shape[1]
    tm = _tile(T, 256)
    bw = math.gcd(D, gate_start)
    n_b, g0 = D // bw, gate_start // bw
    row = lambda w: pl.BlockSpec((tm, w), lambda i: (i, 0))
    resident = lambda r: pl.BlockSpec((r, D), lambda i: (0, 0), pipeline_mode=pl.Buffered(1))
    gate = lambda blk: pl.BlockSpec((tm, bw), lambda i: (i, blk))
    return pl.pallas_call(
        functools.partial(_merge_kernel, hpg=hpg, n_b=n_b),
        grid=(T // tm,),
        in_specs=[row(W), row(W), row(W), row(LANES), row(LANES), row(LANES), row(FW), resident(W), resident(FW)]
        + [gate(g0 + c) for c in range(2 * n_b)],
        out_specs=pl.BlockSpec((tm, D), lambda i: (i, 0)),
        out_shape=jax.ShapeDtypeStruct((T, D), BF16),
        scratch_shapes=[pltpu.VMEM((tm, W), BF16)],
        compiler_params=_params(("parallel",)),
    )(*os_, *lses, fr, wa, wf, *([fg] * (2 * n_b)))


def _out_proj_kernel(m_ref, w_ref, x_ref, o_ref):
    o_ref[...] = x_ref[...] + jnp.dot(m_ref[...], w_ref[...], preferred_element_type=F32)


def out_projection(merged, w, x):
    T, K = merged.shape
    N = w.shape[1]
    tm, tn = _tile(T, 1024), _tile(N, 1024)
    return pl.pallas_call(
        _out_proj_kernel,
        grid=(T // tm, N // tn),
        in_specs=[pl.BlockSpec((tm, K), lambda i, j: (i, 0)),
                  pl.BlockSpec((K, tn), lambda i, j: (0, j)),
                  pl.BlockSpec((tm, tn), lambda i, j: (i, j))],
        out_specs=pl.BlockSpec((tm, tn), lambda i, j: (i, j)),
        out_shape=jax.ShapeDtypeStruct((T, N), F32),
        compiler_params=_params(("parallel", "arbitrary")),
    )(merged, w, x)


def _router_kernel(*refs, bounds):
    n_in = len(bounds) - 1
    i = pl.program_id(0)
    for k in range(n_in):
        @pl.when((i >= bounds[k]) & (i < bounds[k + 1]))
        def _(k=k):
            _route_tile(refs[k], *refs[n_in:])


def _route_tile(x_ref, g_ref, w_ref, b_ref, eid_ref, cw_ref, hp_ref):
    x = x_ref[...]
    ms = jnp.mean(x * x, axis=-1, keepdims=True)
    hf = x * lax.rsqrt(ms + RMS_EPS) * g_ref[...]
    h = hf.astype(BF16)
    half_d = hf.shape[1] // 2
    packed = _pack_pairs(hf[:, :half_d], hf[:, half_d:])
    for c in range(hp_ref.shape[1]):
        hp_ref[:, c, :] = packed[:, c * LANES:(c + 1) * LANES]
    lg = jnp.dot(h, w_ref[...], preferred_element_type=F32) + b_ref[...]
    lane = lax.broadcasted_iota(jnp.int32, lg.shape, 1)
    big = jnp.int32(LANES)
    in_grp = lane < N_EXPERT_GROUPS
    gl = jnp.where(in_grp, lg, -jnp.inf)
    gmax = jnp.max(gl, axis=-1, keepdims=True)
    gsel = jnp.min(jnp.where(gl == gmax, lane, big), axis=-1, keepdims=True)
    p_group = 1.0 / jnp.sum(jnp.where(in_grp, jnp.exp(gl - gmax), 0.0), axis=-1, keepdims=True)
    lo = N_EXPERT_GROUPS + gsel * EXPERTS_PER_GROUP
    in_sel = (lane >= lo) & (lane < lo + EXPERTS_PER_GROUP)
    el = jnp.where(in_sel, lg, -jnp.inf)
    t1 = jnp.max(el, axis=-1, keepdims=True)
    i1 = jnp.min(jnp.where(el == t1, lane, big), axis=-1, keepdims=True)
    el2 = jnp.where(lane == i1, -jnp.inf, el)
    t2 = jnp.max(el2, axis=-1, keepdims=True)
    i2 = jnp.min(jnp.where(el2 == t2, lane, big), axis=-1, keepdims=True)
    e21 = jnp.exp(t2 - t1)
    p1 = 1.0 / (1.0 + e21)
    p2 = e21 / (1.0 + e21)
    eid_ref[...] = jnp.where(lane == 0, i1 - N_EXPERT_GROUPS, jnp.where(lane == 1, i2 - N_EXPERT_GROUPS, 0))
    cw_ref[...] = jnp.where(lane == 0, p_group * p1, jnp.where(lane == 1, p_group * p2, 0.0))


def route(x1s, g, w_router, b_router, tm):
    D = x1s[0].shape[1]
    bounds = [0]
    for x1 in x1s:
        bounds.append(bounds[-1] + x1.shape[0] // tm)
    T = bounds[-1] * tm

    def x_spec(k):
        lo, n = bounds[k], bounds[k + 1] - bounds[k]
        return pl.BlockSpec((tm, D), lambda i: (jnp.clip(i - lo, 0, n - 1), 0))

    return pl.pallas_call(
        functools.partial(_router_kernel, bounds=tuple(bounds)),
        grid=(bounds[-1],),
        in_specs=[x_spec(k) for k in range(len(x1s))]
        + [pl.BlockSpec((1, D), lambda i: (0, 0)),
           pl.BlockSpec((D, LANES), lambda i: (0, 0)),
           pl.BlockSpec((1, LANES), lambda i: (0, 0))],
        out_specs=[pl.BlockSpec((tm, LANES), lambda i: (i, 0)), pl.BlockSpec((tm, LANES), lambda i: (i, 0)),
                   pl.BlockSpec((tm, D // (2 * LANES), LANES), lambda i: (i, 0, 0))],
        out_shape=[jax.ShapeDtypeStruct((T, LANES), jnp.int32), jax.ShapeDtypeStruct((T, LANES), F32),
                   jax.ShapeDtypeStruct((T, D // (2 * LANES), LANES), U32)],
        compiler_params=_params(("arbitrary",)),
    )(*x1s, g.reshape(1, D).astype(F32), w_router, b_router)


def _expert_up_kernel(src_ref, te_ref, nused_ref, hp_hbm, wg_ref, wu_ref, a_ref, buf, sem, *, tme, nc, n_k):
    i = pl.program_id(0)
    n_used = nused_ref[0]
    half_d = nc * LANES

    def row_copy(tile, slot, r):
        return pltpu.make_async_copy(hp_hbm.at[src_ref[tile * tme + r]], buf.at[slot, pl.ds(r * nc, nc), :],
                                     sem.at[slot])

    def drain(tile, slot):
        def body(r, c):
            row_copy(tile, slot, r).wait()
            return c
        lax.fori_loop(0, tme, body, 0, unroll=8)

    @pl.when(i == 0)
    def _():
        def body(r, c):
            row_copy(0, 0, r).start()
            return c
        lax.fori_loop(0, tme, body, 0, unroll=8)

    @pl.when(i < n_used)
    def _():
        slot = i % 2
        drain(i, slot)
        nxt = jnp.minimum(i + 1, n_used - 1)
        for r in range(tme):
            row_copy(nxt, 1 - slot, r).start(priority=r % 2)
        cpk = nc // n_k
        wk = cpk * LANES
        gate = up = None
        for q in range(n_k):
            parts = [_unpack_pairs(buf[slot, pl.ds(q * cpk + cc, tme, stride=nc), :]) for cc in range(cpk)]
            lo = jnp.concatenate([p[0].astype(BF16) for p in parts], axis=1)
            hi = jnp.concatenate([p[1].astype(BF16) for p in parts], axis=1)
            rows_lo = slice(q * wk, (q + 1) * wk)
            rows_hi = slice(half_d + q * wk, half_d + (q + 1) * wk)
            g = (jnp.dot(lo, wg_ref[0, rows_lo, :], preferred_element_type=F32)
                 + jnp.dot(hi, wg_ref[0, rows_hi, :], preferred_element_type=F32))
            u = (jnp.dot(lo, wu_ref[0, rows_lo, :], preferred_element_type=F32)
                 + jnp.dot(hi, wu_ref[0, rows_hi, :], preferred_element_type=F32))
            gate = g if gate is None else gate + g
            up = u if up is None else up + u
        a_ref[...] = (jax.nn.silu(gate) * up).astype(a_ref.dtype)

    @pl.when(i == n_used - 1)
    def _():
        drain(i, 1 - i % 2)

    @pl.when(i >= n_used)
    def _():
        a_ref[...] = jnp.zeros_like(a_ref)


def _expert_down_kernel(dst_ref, te_ref, nused_ref, a_ref, wd_ref, yt_hbm, ybuf, sem, *, tme, n_split, n_tiles):
    i = pl.program_id(0)
    n_used = nused_ref[0]
    last = n_tiles - 1
    half_d = ybuf.shape[2]
    wc = half_d // n_split

    def row_copy(tile, r):
        slot = (tile + 3) % 3
        return pltpu.make_async_copy(ybuf.at[slot, pl.ds(r, 1), :],
                                     yt_hbm.at[pl.ds(dst_ref[(tile + 1) * tme + r], 1), :], sem.at[slot])

    def drain(tile):
        def body(r, c):
            row_copy(tile, r).wait()
            return c
        lax.fori_loop(0, tme, body, 0, unroll=8)

    @pl.when(i == 0)
    def _():
        ybuf[2] = jnp.zeros(ybuf.shape[1:], ybuf.dtype)

    @pl.when((i >= 2) & (i - 3 < n_used))
    def _():
        drain(i - 3)

    @pl.when(i < n_used)
    def _():
        a = a_ref[...]
        for r in range(tme):
            row_copy(i - 1, r).start(priority=r % 2)
        for k in range(n_split):
            lo = jnp.dot(a, wd_ref[0, :, k * wc:(k + 1) * wc], preferred_element_type=F32)
            hi = jnp.dot(a, wd_ref[0, :, half_d + k * wc:half_d + (k + 1) * wc], preferred_element_type=F32)
            ybuf[i % 3, :, k * wc:(k + 1) * wc] = _pack_pairs(lo, hi)

    @pl.when(i == n_used)
    def _():
        def body(r, c):
            row_copy(i - 1, r).start()
            return c
        lax.fori_loop(0, tme, body, 0, unroll=8)

    @pl.when((i == last) & (last - 2 < n_used))
    def _():
        drain(last - 2)

    @pl.when((i == last) & (last - 1 < n_used))
    def _():
        drain(last - 1)


def expert_ffn(hp, w_gate, w_up, w_down, src, dst, tile_expert, n_used, n_tiles, tme, n_rows_out):
    nc = hp.shape[1]
    D = nc * 2 * LANES
    F = w_gate.shape[2]
    P = n_tiles * tme
    act = pl.pallas_call(
        functools.partial(_expert_up_kernel, tme=tme, nc=nc, n_k=max(1, nc // 4)),
        grid_spec=pltpu.PrefetchScalarGridSpec(
            num_scalar_prefetch=3,
            grid=(n_tiles,),
            in_specs=[pl.BlockSpec(memory_space=pl.ANY),
                      pl.BlockSpec((1, D, F), lambda i, s, te, n: (te[i], 0, 0)),
                      pl.BlockSpec((1, D, F), lambda i, s, te, n: (te[i], 0, 0))],
            out_specs=pl.BlockSpec((tme, F), lambda i, s, te, n: (i, 0)),
            scratch_shapes=[pltpu.VMEM((2, tme * nc, LANES), U32), pltpu.SemaphoreType.DMA((2,))],
        ),
        out_shape=jax.ShapeDtypeStruct((P, F), BF16),
        compiler_params=_params(("arbitrary",)),
    )(src, tile_expert, n_used, hp, w_gate, w_up)
    n_split = max(1, D // 1024)
    spare = n_rows_out - tme + jnp.arange(tme, dtype=jnp.int32)
    return pl.pallas_call(
        functools.partial(_expert_down_kernel, tme=tme, n_split=n_split, n_tiles=n_tiles),
        grid_spec=pltpu.PrefetchScalarGridSpec(
            num_scalar_prefetch=3,
            grid=(n_tiles,),
            in_specs=[pl.BlockSpec((tme, F), lambda i, d, te, n: (i, 0)),
                      pl.BlockSpec((1, F, D), lambda i, d, te, n: (te[i], 0, 0))],
            out_specs=pl.BlockSpec(memory_space=pl.ANY),
            scratch_shapes=[pltpu.VMEM((3, tme, D // 2), U32), pltpu.SemaphoreType.DMA((3,))],
        ),
        out_shape=jax.ShapeDtypeStruct((n_rows_out, D // 2), U32),
        compiler_params=_params(("arbitrary",)),
    )(jnp.concatenate([spare, dst]), tile_expert, n_used, act, w_down)


def _final_kernel(x_ref, cw_ref, y0_ref, y1_ref, g_ref, o_ref):
    half_d = y0_ref.shape[1]
    cw0 = cw_ref[:, 0:1]
    cw1 = cw_ref[:, 1:2]
    lo0, hi0 = _unpack_pairs(y0_ref[...])
    lo1, hi1 = _unpack_pairs(y1_ref[...])
    xl = x_ref[:, :half_d] + (cw0 * lo0 + cw1 * lo1)
    xh = x_ref[:, half_d:] + (cw0 * hi0 + cw1 * hi1)
    ss = jnp.sum(xl * xl, axis=-1, keepdims=True) + jnp.sum(xh * xh, axis=-1, keepdims=True)
    inv = lax.rsqrt(ss / (2 * half_d) + RMS_EPS)
    o_ref[:, :half_d] = xl * inv * g_ref[:, :half_d]
    o_ref[:, half_d:] = xh * inv * g_ref[:, half_d:]


def final_combine(x1, cw, yt, g, tm, tile0, n_tok_tiles):
    T, D = x1.shape
    return pl.pallas_call(
        _final_kernel,
        grid=(T // tm,),
        in_specs=[pl.BlockSpec((tm, D), lambda i: (i, 0)),
                  pl.BlockSpec((tm, LANES), lambda i: (tile0 + i, 0)),
                  pl.BlockSpec((tm, D // 2), lambda i: (tile0 + i, 0)),
                  pl.BlockSpec((tm, D // 2), lambda i: (n_tok_tiles + tile0 + i, 0)),
                  pl.BlockSpec((1, D), lambda i: (0, 0))],
        out_specs=pl.BlockSpec((tm, D), lambda i: (i, 0)),
        out_shape=jax.ShapeDtypeStruct((T, D), F32),
        compiler_params=_params(("parallel",)),
    )(x1, cw, yt, yt, g.reshape(1, D).astype(F32))


def _sorted_layout(eid, tme):
    T = eid.shape[0]
    n_tiles = (2 * T) // tme + N_EXPERTS
    P = n_tiles * tme
    flat_e = eid.reshape(-1)
    onehot = (flat_e[:, None] == jnp.arange(N_EXPERTS, dtype=jnp.int32)[None, :]).astype(jnp.int32)
    csum = jnp.cumsum(onehot, axis=0)
    rank = jnp.take_along_axis(csum, flat_e[:, None], axis=1)[:, 0] - 1
    counts = csum[-1]
    padded = ((counts + tme - 1) // tme) * tme
    ends = jnp.cumsum(padded)
    pos = (ends - padded)[flat_e] + rank
    spare = 2 * T + jnp.arange(P, dtype=jnp.int32) % tme
    a = jnp.arange(2 * T, dtype=jnp.int32)
    dst = spare.at[pos].set((a % 2) * T + a // 2)
    src = jnp.where(dst < 2 * T, dst % T, 0)
    n_used = (ends[-1] // tme).astype(jnp.int32)
    tile_start = jnp.minimum(jnp.arange(n_tiles, dtype=jnp.int32), n_used - 1) * tme
    tile_expert = jnp.minimum(jnp.searchsorted(ends, tile_start, side="right"), N_EXPERTS - 1).astype(jnp.int32)
    return src, dst, tile_expert, n_used.reshape(1), n_tiles


def _encoder_trunk(x, wts):
    B, S, D = x.shape
    T = B * S
    hpg, cg = wts["hpg"], wts["cg"]
    n_grp = len(ATTN_PATTERNS)
    W = hpg * HEAD_DIM
    f_start = 3 * n_grp * W
    f_width = N_FOURIER_GROUPS * cg
    x2d = x.reshape(T, D)

    dils = sorted({d for _, d in ATTN_PATTERNS if d > 1})
    h, h_cm = rmsnorm_cast(x2d, wts["attn_norm_g"], B, S, dils)
    h_cm[1] = h

    outs, lses = [], []
    for g, (_, dil) in enumerate(ATTN_PATTERNS):
        qkv = in_projection(h_cm[dil], wts["w_in"], wts["b_full"], W, 3, lambda j, g=g: j * n_grp + g)
        o_g, lse_g = band_attention(qkv, B, S, g, hpg, n_grp * hpg)
        outs.append(o_g)
        lses.append(lse_g)

    tn = _tile(math.gcd(math.gcd(f_start, f_width), 2 * D), 1024)
    c0 = f_start // tn
    fg = in_projection(h, wts["w_in"], wts["b_full"], tn, (f_width + 2 * D) // tn, lambda j: c0 + j,
                       gate_tile0=f_width // tn)

    pe, po, ph = fourier_channel_stage(fg, B, S, cg, wts["cos_c"], wts["sin_c"])
    cos_s, sin_s = _dft_tables(S, S // 2)
    fr = fourier_sequence_stage(pe, po, ph, cos_s.astype(BF16), (-sin_s).astype(BF16), S,
                                float(1.0 / math.sqrt(S * cg)))

    merged = merge_branches(outs, lses, fr, wts["w_branch_attn"], wts["w_branch_fourier"], fg, f_width, hpg)
    return out_projection(merged, wts["w_out"], x2d)


def _moe_and_final_norm(x1s, wts):
    t_all = sum(x1.shape[0] for x1 in x1s)
    tm = _tile(math.gcd(*[x1.shape[0] for x1 in x1s]), 256, 8)
    eid, cw, hp = route(x1s, wts["ffn_norm_g"], wts["w_router"], wts["b_router"], tm)
    tme = _tile(2 * t_all, EXPERT_ROW_TILE, 8)
    src, dst, tile_expert, n_used, n_tiles = _sorted_layout(eid[:, :2], tme)
    yt = expert_ffn(hp, wts["w_expert_gate"], wts["w_expert_up"], wts["w_expert_down"], src, dst, tile_expert,
                    n_used, n_tiles, tme, 2 * t_all + tme)
    outs, tile0 = [], 0
    for x1 in x1s:
        outs.append(final_combine(x1, cw, yt, wts["final_norm_g"], tm, tile0, t_all // tm))
        tile0 += x1.shape[0] // tm
    return outs


def kernel(x_prompt, x_sample, attn_norm_g, w_in, w_branch_attn, w_branch_fourier, b_gate, w_out, ffn_norm_g, w_router_group, b_router_group, w_router_expert, b_router_expert, w_expert_gate, w_expert_up, w_expert_down, final_norm_g):
    assert w_in.shape[0] == 1, "the final norm is fused into the layer's last kernel: one layer only"
    l = 0
    D = x_prompt.shape[-1]
    hpg = w_branch_attn.shape[1] // HEAD_DIM
    cg = w_branch_fourier.shape[1] // N_FOURIER_GROUPS
    in_width = w_in.shape[2]
    n_gate = b_gate.shape[1]
    cos_c, sin_c = _dft_tables(cg)
    w_r = jnp.concatenate(
        [w_router_group[l], jnp.transpose(w_router_expert[l], (1, 0, 2)).reshape(D, N_EXPERTS)], axis=1)
    b_r = jnp.concatenate([b_router_group[l], b_router_expert[l].reshape(-1)])
    pad = LANES - w_r.shape[1]
    wts = dict(
        hpg=hpg, cg=cg, cos_c=cos_c.astype(BF16), sin_c=sin_c.astype(BF16),
        attn_norm_g=attn_norm_g[l], ffn_norm_g=ffn_norm_g[l], final_norm_g=final_norm_g,
        w_in=w_in[l].astype(BF16),
        b_full=jnp.concatenate([jnp.zeros((in_width - n_gate,), F32), b_gate[l].astype(F32)]).reshape(1, in_width),
        w_branch_attn=w_branch_attn[l].astype(BF16),
        w_branch_fourier=w_branch_fourier[l].astype(BF16),
        w_out=w_out[l].astype(BF16),
        w_router=jnp.pad(w_r, ((0, 0), (0, pad))).astype(BF16),
        b_router=jnp.pad(b_r, (0, pad)).reshape(1, LANES).astype(F32),
        w_expert_gate=w_expert_gate[l].astype(BF16),
        w_expert_up=w_expert_up[l].astype(BF16),
        w_expert_down=w_expert_down[l].astype(BF16),
    )
    xs = (x_prompt, x_sample)
    outs = _moe_and_final_norm([_encoder_trunk(x, wts) for x in xs], wts)
    return tuple(o.reshape(x.shape) for o, x in zip(outs, xs))
```

```python
import functools
import math

import numpy as np
import jax
import jax.numpy as jnp
from jax import lax
from jax.experimental import pallas as pl
from jax.experimental.pallas import tpu as pltpu

F32 = jnp.float32
BF16 = jnp.bfloat16
U32 = jnp.uint32

RMS_EPS = 1e-6
NEG_INF = -1e30
HEAD_DIM = 128
ATTN_PATTERNS = ((128, 1), (512, 4), (2048, 16))
N_FOURIER_GROUPS = 4
N_EXPERT_GROUPS = 4
EXPERTS_PER_GROUP = 4
N_EXPERTS = N_EXPERT_GROUPS * EXPERTS_PER_GROUP
LANES = 128
V7X_MXU_COLS = 256
V7X_VMEM_LIMIT = 56 * 1024 * 1024
ATTN_Q_TILE = 128
ATTN_ROW_TILE = 1024
EXPERT_ROW_TILE = 512
HI_MASK = 0xFFFF0000


def _tile(n, pref, mult=LANES):
    if n <= pref:
        return n
    t = (pref // mult) * mult
    while t >= mult:
        if n % t == 0:
            return t
        t -= mult
    raise ValueError(f"no tile for {n} <= {pref}")


def _params(sem, vmem=V7X_VMEM_LIMIT):
    return pltpu.CompilerParams(dimension_semantics=sem, vmem_limit_bytes=vmem)


def _pack_pairs(lo, hi):
    a = pltpu.bitcast(lo.astype(BF16).astype(F32), U32) >> 16
    b = pltpu.bitcast(hi.astype(BF16).astype(F32), U32) & jnp.uint32(HI_MASK)
    return a | b


def _unpack_pairs(w):
    return pltpu.bitcast(w << 16, F32), pltpu.bitcast(w & jnp.uint32(HI_MASK), F32)


def _rms_kernel(x_ref, g_ref, o_ref, *cm_refs, dils):
    x = x_ref[...]
    ms = jnp.mean(x * x, axis=-1, keepdims=True)
    h = (x * lax.rsqrt(ms + RMS_EPS) * g_ref[...]).astype(o_ref.dtype)
    o_ref[...] = h
    tm = x.shape[0]
    row = lax.broadcasted_iota(jnp.int32, (tm, tm), 0)
    col = lax.broadcasted_iota(jnp.int32, (tm, tm), 1)
    for cm_ref, d in zip(cm_refs, dils):
        n = tm // d
        perm = (col == (row % n) * d + row // n).astype(h.dtype)
        hp = jnp.dot(perm, h, preferred_element_type=F32).astype(cm_ref.dtype)
        for r in range(d):
            cm_ref[0, r] = hp[r * n:(r + 1) * n]


def rmsnorm_cast(x, g, B, S, dils):
    T, D = x.shape
    tm = _tile(S, 256, 16 * max(dils, default=1))
    spb = S // tm
    outs = pl.pallas_call(
        functools.partial(_rms_kernel, dils=tuple(dils)),
        grid=(T // tm,),
        in_specs=[pl.BlockSpec((tm, D), lambda i: (i, 0)), pl.BlockSpec((1, D), lambda i: (0, 0))],
        out_specs=[pl.BlockSpec((tm, D), lambda i: (i, 0))]
        + [pl.BlockSpec((1, d, tm // d, D), lambda i: (i // spb, 0, i % spb, 0)) for d in dils],
        out_shape=[jax.ShapeDtypeStruct((T, D), BF16)]
        + [jax.ShapeDtypeStruct((B, d, S // d, D), BF16) for d in dils],
        compiler_params=_params(("parallel",)),
    )(x, g.reshape(1, D).astype(F32))
    return outs[0], {d: o.reshape(T, D) for d, o in zip(dils, outs[1:])}


def _in_proj_kernel(h_ref, w_ref, b_ref, o_ref, *, gate_tile0):
    def plain():
        o_ref[...] = jnp.dot(h_ref[...], w_ref[...], preferred_element_type=F32).astype(o_ref.dtype)

    if gate_tile0 is None:
        plain()
        return
    j = pl.program_id(1)
    pl.when(j < gate_tile0)(plain)

    @pl.when(j >= gate_tile0)
    def _():
        h = h_ref[...]
        for c0 in range(0, o_ref.shape[1], V7X_MXU_COLS):
            cols = slice(c0, c0 + V7X_MXU_COLS)
            acc = jnp.dot(h, w_ref[:, cols], preferred_element_type=F32)
            o_ref[:, cols] = jax.nn.sigmoid(acc + b_ref[:, cols]).astype(o_ref.dtype)


def in_projection(h, w, b_full, tn, n_out, w_col, gate_tile0=None):
    T, K = h.shape
    tm = _tile(T, 1024)
    return pl.pallas_call(
        functools.partial(_in_proj_kernel, gate_tile0=gate_tile0),
        grid=(T // tm, n_out),
        in_specs=[pl.BlockSpec((tm, K), lambda i, j: (i, 0)),
                  pl.BlockSpec((K, tn), lambda i, j: (0, w_col(j))),
                  pl.BlockSpec((1, tn), lambda i, j: (0, w_col(j)))],
        out_specs=pl.BlockSpec((tm, tn), lambda i, j: (i, j)),
        out_shape=jax.ShapeDtypeStruct((T, n_out * tn), BF16),
        compiler_params=_params(("parallel", "arbitrary")),
    )(h, w, b_full)


def _attn_kernel(q_ref, kp_ref, kc_ref, kn_ref, vp_ref, vc_ref, vn_ref, o_ref, lse_ref, so, sl,
                 *, tq, sub, half, hps, dil, cls_len, slopes):
    i = pl.program_id(1)
    hh = pl.program_id(2)
    key_pad = -(sub + 2 * half) % LANES
    span = sub + 2 * half + key_pad
    scale = HEAD_DIM ** -0.5
    qidx = lax.broadcasted_iota(jnp.int32, (sub, span), 0)
    kidx = lax.broadcasted_iota(jnp.int32, (sub, span), 1)
    absrel = jnp.abs(kidx - half - qidx)
    absrel_f = absrel.astype(F32)
    lane = lax.broadcasted_iota(jnp.int32, (sub, LANES), 1)
    n_sub = tq // sub
    valids = []
    for u in range(n_sub):
        kpos = i * tq + (u * sub - half) + kidx
        valids.append((absrel <= half) & (kpos >= 0) & (kpos < cls_len))
    neg_slopes = []
    for h in range(hps):
        s_h = jnp.float32(0.0)
        for b in range(len(slopes) // hps):
            s_h = jnp.where(hh == b, jnp.float32(-slopes[b * hps + h] * dil), s_h)
        neg_slopes.append(s_h)

    @pl.when(hh == 0)
    def _():
        sl[...] = jnp.zeros_like(sl)

    def keys(prev_ref, cur_ref, next_ref, r, u, cols):
        lo = u * sub - half if u > 0 else 0
        hi = (u + 1) * sub + half if u < n_sub - 1 else tq
        parts = [cur_ref[0, r, lo:hi, cols]]
        if u == 0:
            parts.insert(0, prev_ref[0, r, :, cols])
        if u == n_sub - 1:
            parts.append(next_ref[0, r, :, cols])
        if key_pad:
            parts.append(jnp.zeros((key_pad, HEAD_DIM), cur_ref.dtype))
        return parts[0] if len(parts) == 1 else jnp.concatenate(parts, axis=0)

    for r in range(dil):
        for u in range(n_sub):
            rows = pl.ds(u * sub * dil + r, sub, stride=dil) if dil > 1 else pl.ds(u * sub, sub)
            lse_tile = sl[rows, :]
            for h in range(hps):
                cols = slice(h * HEAD_DIM, (h + 1) * HEAD_DIM)
                q = q_ref[0, r, u * sub:(u + 1) * sub, cols]
                k = keys(kp_ref, kc_ref, kn_ref, r, u, cols)
                v = keys(vp_ref, vc_ref, vn_ref, r, u, cols)
                s = lax.dot_general(q, k, (((1,), (1,)), ((), ())), preferred_element_type=F32)
                s = s * scale + absrel_f * neg_slopes[h]
                s = jnp.where(valids[u], s, NEG_INF)
                m = jnp.max(s, axis=-1, keepdims=True)
                p = jnp.exp(s - m)
                den = jnp.sum(p, axis=-1, keepdims=True)
                so[h, rows, :] = jnp.dot(p.astype(v.dtype), v, preferred_element_type=F32) / den
                lse_tile = jnp.where(lane == hh * hps + h, m + jnp.log(den), lse_tile)
            sl[rows, :] = lse_tile
    for h in range(hps):
        o_ref[0, :, h * HEAD_DIM:(h + 1) * HEAD_DIM] = so[h].astype(o_ref.dtype)
    lse_ref[0] = sl[...]


def band_attention(qkv, B, S, group, hpg, n_heads_total):
    window, dil = ATTN_PATTERNS[group]
    half = window // (2 * dil)
    W = hpg * HEAD_DIM
    L = S // dil
    tq = min(L, max(ATTN_Q_TILE, ATTN_ROW_TILE // dil))
    rows = tq * dil
    sub = min(ATTN_Q_TILE, tq)
    hps = max(1, hpg * ATTN_ROW_TILE // max(rows, ATTN_ROW_TILE))
    assert S % dil == 0 and L % tq == 0 and tq % sub == 0 and tq % half == 0 and half % 16 == 0 and hpg % hps == 0
    n_hh = hpg // hps
    Wb = hps * HEAD_DIM
    hb = tq // half
    n_hb = L // half
    s_all = 2.0 ** (-8.0 * np.arange(1, n_heads_total + 1) / n_heads_total)
    slopes = tuple(float(np.float32(v)) for v in s_all[group * hpg:(group + 1) * hpg])
    qv = qkv.reshape(B, dil, L, 3 * W)

    cur = lambda part: pl.BlockSpec((1, dil, tq, Wb), lambda b, i, hh: (b, 0, i, part * n_hh + hh))
    prv = lambda part: pl.BlockSpec(
        (1, dil, half, Wb), lambda b, i, hh: (b, 0, jnp.maximum(i * hb - 1, 0), part * n_hh + hh))
    nxt = lambda part: pl.BlockSpec(
        (1, dil, half, Wb), lambda b, i, hh: (b, 0, jnp.minimum((i + 1) * hb, n_hb - 1), part * n_hh + hh))
    o, lse = pl.pallas_call(
        functools.partial(_attn_kernel, tq=tq, sub=sub, half=half, hps=hps, dil=dil, cls_len=L, slopes=slopes),
        grid=(B, L // tq, n_hh),
        in_specs=[cur(0), prv(1), cur(1), nxt(1), prv(2), cur(2), nxt(2)],
        out_specs=[pl.BlockSpec((1, rows, Wb), lambda b, i, hh: (b, i, hh)),
                   pl.BlockSpec((1, rows, LANES), lambda b, i, hh: (b, i, 0))],
        out_shape=[jax.ShapeDtypeStruct((B, S, W), BF16), jax.ShapeDtypeStruct((B, S, LANES), F32)],
        scratch_shapes=[pltpu.VMEM((hps, rows, LANES), F32), pltpu.VMEM((rows, LANES), F32)],
        compiler_params=_params(("parallel", "arbitrary", "arbitrary")),
    )(qv, qv, qv, qv, qv, qv, qv)
    return o.reshape(B * S, W), lse.reshape(B * S, LANES)


def _dft_tables(n, cols=None):
    cols = n if cols is None else cols
    k = jnp.arange(n, dtype=jnp.int32)

    def thin(s):
        ang = ((k[:, None] * s[None, :]) % n).astype(F32) * np.float32(2.0 * np.pi / n)
        return jnp.cos(ang), jnp.sin(ang)

    m = 64
    if cols <= m or cols % m:
        return thin(jnp.arange(cols, dtype=jnp.int32))
    c_hi, s_hi = thin(jnp.arange(cols // m, dtype=jnp.int32) * m)
    c_lo, s_lo = thin(jnp.arange(m, dtype=jnp.int32))
    cos = c_hi[:, :, None] * c_lo[:, None, :] - s_hi[:, :, None] * s_lo[:, None, :]
    sin = s_hi[:, :, None] * c_lo[:, None, :] + c_hi[:, :, None] * s_lo[:, None, :]
    return cos.reshape(n, cols), sin.reshape(n, cols)


def _fourier_chan_kernel(xd_ref, xa_ref, xb_ref, xh_ref, cos_ref, sin_ref, pe_ref, po_ref, ph_ref):
    i = pl.program_id(1)
    ts = xd_ref.shape[0]
    cg = cos_ref.shape[0]
    row = lax.broadcasted_iota(jnp.int32, (ts, ts), 0)
    col = lax.broadcasted_iota(jnp.int32, (ts, ts), 1)
    flip = (col == ts - row).astype(xa_ref.dtype)
    rev = jnp.dot(flip, xa_ref[...], preferred_element_type=F32)
    first = lax.broadcasted_iota(jnp.int32, rev.shape, 0) == 0
    rev = jnp.where(first, xb_ref[0:1, :].astype(F32), rev)
    xd = xd_ref[...].astype(F32)
    xe = xd + jnp.where(first & (i == 0), 0.0, rev)
    xo = xd - rev
    xe, xo = xe.astype(BF16), xo.astype(BF16)
    for g in range(N_FOURIER_GROUPS):
        cols = slice(g * cg, (g + 1) * cg)
        pe_ref[0, :, cols] = jnp.dot(xe[:, cols], cos_ref[...], preferred_element_type=F32).astype(pe_ref.dtype)
        po_ref[0, :, cols] = jnp.dot(xo[:, cols], sin_ref[...], preferred_element_type=F32).astype(po_ref.dtype)

    @pl.when(i == 0)
    def _():
        for g in range(N_FOURIER_GROUPS):
            cols = slice(g * cg, (g + 1) * cg)
            ph_ref[0, :, cols] = jnp.dot(xh_ref[:, cols], cos_ref[...],
                                         preferred_element_type=F32).astype(ph_ref.dtype)


def fourier_channel_stage(fg, B, S, cg, cos_c, sin_c):
    half = S // 2
    ts = _tile(half, 256, 16)
    nt = half // ts
    n_all = N_FOURIER_GROUPS * cg
    sub = 16
    return pl.pallas_call(
        _fourier_chan_kernel,
        grid=(B, nt),
        in_specs=[pl.BlockSpec((ts, n_all), lambda b, i: (b * (S // ts) + i, 0)),
                  pl.BlockSpec((ts, n_all), lambda b, i: (b * (S // ts) + (S // ts - 1 - i), 0)),
                  pl.BlockSpec((sub, n_all), lambda b, i: (b * (S // sub) + ((S - i * ts) % S) // sub, 0)),
                  pl.BlockSpec((sub, n_all), lambda b, i: (b * (S // sub) + half // sub, 0)),
                  pl.BlockSpec((cg, cg), lambda b, i: (0, 0)),
                  pl.BlockSpec((cg, cg), lambda b, i: (0, 0))],
        out_specs=[pl.BlockSpec((1, ts, n_all), lambda b, i: (b, i, 0)),
                   pl.BlockSpec((1, ts, n_all), lambda b, i: (b, i, 0)),
                   pl.BlockSpec((1, sub, n_all), lambda b, i: (b, 0, 0))],
        out_shape=[jax.ShapeDtypeStruct((B, half, n_all), BF16), jax.ShapeDtypeStruct((B, half, n_all), BF16),
                   jax.ShapeDtypeStruct((B, sub, n_all), BF16)],
        compiler_params=_params(("parallel", "arbitrary")),
    )(fg, fg, fg, fg, cos_c, sin_c)


def _fourier_seq_kernel(cs_ref, ss_ref, pe_ref, po_ref, ph_ref, o_ref, acc_ref, *, scale):
    k = pl.program_id(3)

    @pl.when(k == 0)
    def _():
        acc_ref[...] = jnp.zeros_like(acc_ref)

    acc_ref[...] += (jnp.dot(cs_ref[...], pe_ref[0], preferred_element_type=F32)
                     + jnp.dot(ss_ref[...], po_ref[0], preferred_element_type=F32))

    @pl.when(k == pl.num_programs(3) - 1)
    def _():
        tm = acc_ref.shape[0]
        freq = pl.program_id(1) * tm + lax.broadcasted_iota(jnp.int32, (tm, 1), 0)
        sign = (1 - 2 * (freq & 1)).astype(F32)
        o_ref[0] = ((acc_ref[...] + sign * ph_ref[0, 0:1, :].astype(F32)) * scale).astype(o_ref.dtype)


def fourier_sequence_stage(pe, po, ph, cos_s, neg_sin_s, S, scale):
    B, half, N = pe.shape
    tm, tn, tk = _tile(S, 1024), _tile(N, 1024), _tile(half, 2048)
    out = pl.pallas_call(
        functools.partial(_fourier_seq_kernel, scale=scale),
        grid=(B, S // tm, N // tn, half // tk),
        in_specs=[pl.BlockSpec((tm, tk), lambda b, i, j, k: (i, k)),
                  pl.BlockSpec((tm, tk), lambda b, i, j, k: (i, k)),
                  pl.BlockSpec((1, tk, tn), lambda b, i, j, k: (b, k, j)),
                  pl.BlockSpec((1, tk, tn), lambda b, i, j, k: (b, k, j)),
                  pl.BlockSpec((1, ph.shape[1], tn), lambda b, i, j, k: (b, 0, j))],
        out_specs=pl.BlockSpec((1, tm, tn), lambda b, i, j, k: (b, i, j)),
        out_shape=jax.ShapeDtypeStruct((B, S, N), BF16),
        scratch_shapes=[pltpu.VMEM((tm, tn), F32)],
        compiler_params=_params(("parallel", "parallel", "parallel", "arbitrary")),
    )(cos_s, neg_sin_s, pe, po, ph)
    return out.reshape(B * S, N)


def _merge_kernel(o0_ref, o1_ref, o2_ref, l0_ref, l1_ref, l2_ref, fr_ref, wa_ref, wf_ref, *refs, hpg, n_b):
    ga_refs, gf_refs = refs[:n_b], refs[n_b:2 * n_b]
    m_ref, oc_ref = refs[2 * n_b], refs[2 * n_b + 1]
    bw = ga_refs[0].shape[1]
    fr = fr_ref[...]
    fours = [jnp.dot(fr, wf_ref[:, c * bw:(c + 1) * bw], preferred_element_type=F32) for c in range(n_b)]
    l0, l1, l2 = l0_ref[...], l1_ref[...], l2_ref[...]
    m = jnp.maximum(jnp.maximum(l0, l1), l2)
    e0, e1, e2 = jnp.exp(l0 - m), jnp.exp(l1 - m), jnp.exp(l2 - m)
    den = e0 + e1 + e2
    w0, w1, w2 = e0 / den, e1 / den, e2 / den
    for h in range(hpg):
        cols = slice(h * HEAD_DIM, (h + 1) * HEAD_DIM)
        oc = (w0[:, h:h + 1] * o0_ref[:, cols].astype(F32)
              + w1[:, h:h + 1] * o1_ref[:, cols].astype(F32)
              + w2[:, h:h + 1] * o2_ref[:, cols].astype(F32))
        oc_ref[:, cols] = oc.astype(oc_ref.dtype)
    oc_all = oc_ref[...]
    for c in range(n_b):
        cols = slice(c * bw, (c + 1) * bw)
        attn = jnp.dot(oc_all, wa_ref[:, cols], preferred_element_type=F32)
        m_ref[:, cols] = (ga_refs[c][...].astype(F32) * attn
                          + gf_refs[c][...].astype(F32) * fours[c]).astype(m_ref.dtype)


def merge_branches(os_, lses, fr, wa, wf, fg, gate_start, hpg):
    T, W = os_[0].shape
    D = wa.shape[1]
    FW = fr.shape[1]
    tm = _tile(T, 256)
    bw = math.gcd(D, gate_start)
    n_b, g0 = D // bw, gate_start // bw
    row = lambda w: pl.BlockSpec((tm, w), lambda i: (i, 0))
    resident = lambda r: pl.BlockSpec((r, D), lambda i: (0, 0), pipeline_mode=pl.Buffered(1))
    gate = lambda blk: pl.BlockSpec((tm, bw), lambda i: (i, blk))
    return pl.pallas_call(
        functools.partial(_merge_kernel, hpg=hpg, n_b=n_b),
        grid=(T // tm,),
        in_specs=[row(W), row(W), row(W), row(LANES), row(LANES), row(LANES), row(FW), resident(W), resident(FW)]
        + [gate(g0 + c) for c in range(2 * n_b)],
        out_specs=pl.BlockSpec((tm, D), lambda i: (i, 0)),
        out_shape=jax.ShapeDtypeStruct((T, D), BF16),
        scratch_shapes=[pltpu.VMEM((tm, W), BF16)],
        compiler_params=_params(("parallel",)),
    )(*os_, *lses, fr, wa, wf, *([fg] * (2 * n_b)))


def _out_proj_kernel(m_ref, w_ref, x_ref, o_ref):
    o_ref[...] = x_ref[...] + jnp.dot(m_ref[...], w_ref[...], preferred_element_type=F32)


def out_projection(merged, w, x):
    T, K = merged.shape
    N = w.shape[1]
    tm, tn = _tile(T, 1024), _tile(N, 1024)
    return pl.pallas_call(
        _out_proj_kernel,
        grid=(T // tm, N // tn),
        in_specs=[pl.BlockSpec((tm, K), lambda i, j: (i, 0)),
                  pl.BlockSpec((K, tn), lambda i, j: (0, j)),
                  pl.BlockSpec((tm, tn), lambda i, j: (i, j))],
        out_specs=pl.BlockSpec((tm, tn), lambda i, j: (i, j)),
        out_shape=jax.ShapeDtypeStruct((T, N), F32),
        compiler_params=_params(("parallel", "arbitrary")),
    )(merged, w, x)


def _router_kernel(*refs, bounds):
    n_in = len(bounds) - 1
    i = pl.program_id(0)
    for k in range(n_in):
        @pl.when((i >= bounds[k]) & (i < bounds[k + 1]))
        def _(k=k):
            _route_tile(refs[k], *refs[n_in:])


def _route_tile(x_ref, g_ref, w_ref, b_ref, eid_ref, cw_ref, hp_ref):
    x = x_ref[...]
    ms = jnp.mean(x * x, axis=-1, keepdims=True)
    hf = x * lax.rsqrt(ms + RMS_EPS) * g_ref[...]
    h = hf.astype(BF16)
    half_d = hf.shape[1] // 2
    packed = _pack_pairs(hf[:, :half_d], hf[:, half_d:])
    for c in range(hp_ref.shape[1]):
        hp_ref[:, c, :] = packed[:, c * LANES:(c + 1) * LANES]
    lg = jnp.dot(h, w_ref[...], preferred_element_type=F32) + b_ref[...]
    lane = lax.broadcasted_iota(jnp.int32, lg.shape, 1)
    big = jnp.int32(LANES)
    in_grp = lane < N_EXPERT_GROUPS
    gl = jnp.where(in_grp, lg, -jnp.inf)
    gmax = jnp.max(gl, axis=-1, keepdims=True)
    gsel = jnp.min(jnp.where(gl == gmax, lane, big), axis=-1, keepdims=True)
    p_group = 1.0 / jnp.sum(jnp.where(in_grp, jnp.exp(gl - gmax), 0.0), axis=-1, keepdims=True)
    lo = N_EXPERT_GROUPS + gsel * EXPERTS_PER_GROUP
    in_sel = (lane >= lo) & (lane < lo + EXPERTS_PER_GROUP)
    el = jnp.where(in_sel, lg, -jnp.inf)
    t1 = jnp.max(el, axis=-1, keepdims=True)
    i1 = jnp.min(jnp.where(el == t1, lane, big), axis=-1, keepdims=True)
    el2 = jnp.where(lane == i1, -jnp.inf, el)
    t2 = jnp.max(el2, axis=-1, keepdims=True)
    i2 = jnp.min(jnp.where(el2 == t2, lane, big), axis=-1, keepdims=True)
    e21 = jnp.exp(t2 - t1)
    p1 = 1.0 / (1.0 + e21)
    p2 = e21 / (1.0 + e21)
    eid_ref[...] = jnp.where(lane == 0, i1 - N_EXPERT_GROUPS, jnp.where(lane == 1, i2 - N_EXPERT_GROUPS, 0))
    cw_ref[...] = jnp.where(lane == 0, p_group * p1, jnp.where(lane == 1, p_group * p2, 0.0))


def route(x1s, g, w_router, b_router, tm):
    D = x1s[0].shape[1]
    bounds = [0]
    for x1 in x1s:
        bounds.append(bounds[-1] + x1.shape[0] // tm)
    T = bounds[-1] * tm

    def x_spec(k):
        lo, n = bounds[k], bounds[k + 1] - bounds[k]
        return pl.BlockSpec((tm, D), lambda i: (jnp.clip(i - lo, 0, n - 1), 0))

    return pl.pallas_call(
        functools.partial(_router_kernel, bounds=tuple(bounds)),
        grid=(bounds[-1],),
        in_specs=[x_spec(k) for k in range(len(x1s))]
        + [pl.BlockSpec((1, D), lambda i: (0, 0)),
           pl.BlockSpec((D, LANES), lambda i: (0, 0)),
           pl.BlockSpec((1, LANES), lambda i: (0, 0))],
        out_specs=[pl.BlockSpec((tm, LANES), lambda i: (i, 0)), pl.BlockSpec((tm, LANES), lambda i: (i, 0)),
                   pl.BlockSpec((tm, D // (2 * LANES), LANES), lambda i: (i, 0, 0))],
        out_shape=[jax.ShapeDtypeStruct((T, LANES), jnp.int32), jax.ShapeDtypeStruct((T, LANES), F32),
                   jax.ShapeDtypeStruct((T, D // (2 * LANES), LANES), U32)],
        compiler_params=_params(("arbitrary",)),
    )(*x1s, g.reshape(1, D).astype(F32), w_router, b_router)


def _expert_up_kernel(src_ref, te_ref, nused_ref, hp_hbm, wg_ref, wu_ref, a_ref, buf, sem, *, tme, nc, n_k):
    i = pl.program_id(0)
    n_used = nused_ref[0]
    half_d = nc * LANES

    def row_copy(tile, slot, r):
        return pltpu.make_async_copy(hp_hbm.at[src_ref[tile * tme + r]], buf.at[slot, pl.ds(r * nc, nc), :],
                                     sem.at[slot])

    def drain(tile, slot):
        def body(r, c):
            row_copy(tile, slot, r).wait()
            return c
        lax.fori_loop(0, tme, body, 0, unroll=8)

    @pl.when(i == 0)
    def _():
        def body(r, c):
            row_copy(0, 0, r).start()
            return c
        lax.fori_loop(0, tme, body, 0, unroll=8)

    @pl.when(i < n_used)
    def _():
        slot = i % 2
        drain(i, slot)
        nxt = jnp.minimum(i + 1, n_used - 1)
        for r in range(tme):
            row_copy(nxt, 1 - slot, r).start(priority=r % 2)
        cpk = nc // n_k
        wk = cpk * LANES
        gate = up = None
        for q in range(n_k):
            parts = [_unpack_pairs(buf[slot, pl.ds(q * cpk + cc, tme, stride=nc), :]) for cc in range(cpk)]
            lo = jnp.concatenate([p[0].astype(BF16) for p in parts], axis=1)
            hi = jnp.concatenate([p[1].astype(BF16) for p in parts], axis=1)
            rows_lo = slice(q * wk, (q + 1) * wk)
            rows_hi = slice(half_d + q * wk, half_d + (q + 1) * wk)
            g = (jnp.dot(lo, wg_ref[0, rows_lo, :], preferred_element_type=F32)
                 + jnp.dot(hi, wg_ref[0, rows_hi, :], preferred_element_type=F32))
            u = (jnp.dot(lo, wu_ref[0, rows_lo, :], preferred_element_type=F32)
                 + jnp.dot(hi, wu_ref[0, rows_hi, :], preferred_element_type=F32))
            gate = g if gate is None else gate + g
            up = u if up is None else up + u
        a_ref[...] = (jax.nn.silu(gate) * up).astype(a_ref.dtype)

    @pl.when(i == n_used - 1)
    def _():
        drain(i, 1 - i % 2)

    @pl.when(i >= n_used)
    def _():
        a_ref[...] = jnp.zeros_like(a_ref)


def _expert_down_kernel(dst_ref, te_ref, nused_ref, a_ref, wd_ref, yt_hbm, ybuf, sem, *, tme, n_split, n_tiles):
    i = pl.program_id(0)
    n_used = nused_ref[0]
    last = n_tiles - 1
    half_d = ybuf.shape[2]
    wc = half_d // n_split

    def row_copy(tile, r):
        slot = (tile + 3) % 3
        return pltpu.make_async_copy(ybuf.at[slot, pl.ds(r, 1), :],
                                     yt_hbm.at[pl.ds(dst_ref[(tile + 1) * tme + r], 1), :], sem.at[slot])

    def drain(tile):
        def body(r, c):
            row_copy(tile, r).wait()
            return c
        lax.fori_loop(0, tme, body, 0, unroll=8)

    @pl.when(i == 0)
    def _():
        ybuf[2] = jnp.zeros(ybuf.shape[1:], ybuf.dtype)

    @pl.when((i >= 2) & (i - 3 < n_used))
    def _():
        drain(i - 3)

    @pl.when(i < n_used)
    def _():
        a = a_ref[...]
        for r in range(tme):
            row_copy(i - 1, r).start(priority=r % 2)
        for k in range(n_split):
            lo = jnp.dot(a, wd_ref[0, :, k * wc:(k + 1) * wc], preferred_element_type=F32)
            hi = jnp.dot(a, wd_ref[0, :, half_d + k * wc:half_d + (k + 1) * wc], preferred_element_type=F32)
            ybuf[i % 3, :, k * wc:(k + 1) * wc] = _pack_pairs(lo, hi)

    @pl.when(i == n_used)
    def _():
        def body(r, c):
            row_copy(i - 1, r).start()
            return c
        lax.fori_loop(0, tme, body, 0, unroll=8)

    @pl.when((i == last) & (last - 2 < n_used))
    def _():
        drain(last - 2)

    @pl.when((i == last) & (last - 1 < n_used))
    def _():
        drain(last - 1)


def expert_ffn(hp, w_gate, w_up, w_down, src, dst, tile_expert, n_used, n_tiles, tme, n_rows_out):
    nc = hp.shape[1]
    D = nc * 2 * LANES
    F = w_gate.shape[2]
    P = n_tiles * tme
    act = pl.pallas_call(
        functools.partial(_expert_up_kernel, tme=tme, nc=nc, n_k=max(1, nc // 4)),
        grid_spec=pltpu.PrefetchScalarGridSpec(
            num_scalar_prefetch=3,
            grid=(n_tiles,),
            in_specs=[pl.BlockSpec(memory_space=pl.ANY),
                      pl.BlockSpec((1, D, F), lambda i, s, te, n: (te[i], 0, 0)),
                      pl.BlockSpec((1, D, F), lambda i, s, te, n: (te[i], 0, 0))],
            out_specs=pl.BlockSpec((tme, F), lambda i, s, te, n: (i, 0)),
            scratch_shapes=[pltpu.VMEM((2, tme * nc, LANES), U32), pltpu.SemaphoreType.DMA((2,))],
        ),
        out_shape=jax.ShapeDtypeStruct((P, F), BF16),
        compiler_params=_params(("arbitrary",)),
    )(src, tile_expert, n_used, hp, w_gate, w_up)
    n_split = max(1, D // 1024)
    spare = n_rows_out - tme + jnp.arange(tme, dtype=jnp.int32)
    return pl.pallas_call(
        functools.partial(_expert_down_kernel, tme=tme, n_split=n_split, n_tiles=n_tiles),
        grid_spec=pltpu.PrefetchScalarGridSpec(
            num_scalar_prefetch=3,
            grid=(n_tiles,),
            in_specs=[pl.BlockSpec((tme, F), lambda i, d, te, n: (i, 0)),
                      pl.BlockSpec((1, F, D), lambda i, d, te, n: (te[i], 0, 0))],
            out_specs=pl.BlockSpec(memory_space=pl.ANY),
            scratch_shapes=[pltpu.VMEM((3, tme, D // 2), U32), pltpu.SemaphoreType.DMA((3,))],
        ),
        out_shape=jax.ShapeDtypeStruct((n_rows_out, D // 2), U32),
        compiler_params=_params(("arbitrary",)),
    )(jnp.concatenate([spare, dst]), tile_expert, n_used, act, w_down)


def _final_kernel(x_ref, cw_ref, y0_ref, y1_ref, g_ref, o_ref):
    half_d = y0_ref.shape[1]
    cw0 = cw_ref[:, 0:1]
    cw1 = cw_ref[:, 1:2]
    lo0, hi0 = _unpack_pairs(y0_ref[...])
    lo1, hi1 = _unpack_pairs(y1_ref[...])
    xl = x_ref[:, :half_d] + (cw0 * lo0 + cw1 * lo1)
    xh = x_ref[:, half_d:] + (cw0 * hi0 + cw1 * hi1)
    ss = jnp.sum(xl * xl, axis=-1, keepdims=True) + jnp.sum(xh * xh, axis=-1, keepdims=True)
    inv = lax.rsqrt(ss / (2 * half_d) + RMS_EPS)
    o_ref[:, :half_d] = xl * inv * g_ref[:, :half_d]
    o_ref[:, half_d:] = xh * inv * g_ref[:, half_d:]


def final_combine(x1, cw, yt, g, tm, tile0, n_tok_tiles):
    T, D = x1.shape
    return pl.pallas_call(
        _final_kernel,
        grid=(T // tm,),
        in_specs=[pl.BlockSpec((tm, D), lambda i: (i, 0)),
                  pl.BlockSpec((tm, LANES), lambda i: (tile0 + i, 0)),
                  pl.BlockSpec((tm, D // 2), lambda i: (tile0 + i, 0)),
                  pl.BlockSpec((tm, D // 2), lambda i: (n_tok_tiles + tile0 + i, 0)),
                  pl.BlockSpec((1, D), lambda i: (0, 0))],
        out_specs=pl.BlockSpec((tm, D), lambda i: (i, 0)),
        out_shape=jax.ShapeDtypeStruct((T, D), F32),
        compiler_params=_params(("parallel",)),
    )(x1, cw, yt, yt, g.reshape(1, D).astype(F32))


def _sorted_layout(eid, tme):
    T = eid.shape[0]
    n_tiles = (2 * T) // tme + N_EXPERTS
    P = n_tiles * tme
    flat_e = eid.reshape(-1)
    onehot = (flat_e[:, None] == jnp.arange(N_EXPERTS, dtype=jnp.int32)[None, :]).astype(jnp.int32)
    csum = jnp.cumsum(onehot, axis=0)
    rank = jnp.take_along_axis(csum, flat_e[:, None], axis=1)[:, 0] - 1
    counts = csum[-1]
    padded = ((counts + tme - 1) // tme) * tme
    ends = jnp.cumsum(padded)
    pos = (ends - padded)[flat_e] + rank
    spare = 2 * T + jnp.arange(P, dtype=jnp.int32) % tme
    a = jnp.arange(2 * T, dtype=jnp.int32)
    dst = spare.at[pos].set((a % 2) * T + a // 2)
    src = jnp.where(dst < 2 * T, dst % T, 0)
    n_used = (ends[-1] // tme).astype(jnp.int32)
    tile_start = jnp.minimum(jnp.arange(n_tiles, dtype=jnp.int32), n_used - 1) * tme
    tile_expert = jnp.minimum(jnp.searchsorted(ends, tile_start, side="right"), N_EXPERTS - 1).astype(jnp.int32)
    return src, dst, tile_expert, n_used.reshape(1), n_tiles


def _encoder_trunk(x, wts):
    B, S, D = x.shape
    T = B * S
    hpg, cg = wts["hpg"], wts["cg"]
    n_grp = len(ATTN_PATTERNS)
    W = hpg * HEAD_DIM
    f_start = 3 * n_grp * W
    f_width = N_FOURIER_GROUPS * cg
    x2d = x.reshape(T, D)

    dils = sorted({d for _, d in ATTN_PATTERNS if d > 1})
    h, h_cm = rmsnorm_cast(x2d, wts["attn_norm_g"], B, S, dils)
    h_cm[1] = h

    outs, lses = [], []
    for g, (_, dil) in enumerate(ATTN_PATTERNS):
        qkv = in_projection(h_cm[dil], wts["w_in"], wts["b_full"], W, 3, lambda j, g=g: j * n_grp + g)
        o_g, lse_g = band_attention(qkv, B, S, g, hpg, n_grp * hpg)
        outs.append(o_g)
        lses.append(lse_g)

    tn = _tile(math.gcd(math.gcd(f_start, f_width), 2 * D), 1024)
    c0 = f_start // tn
    fg = in_projection(h, wts["w_in"], wts["b_full"], tn, (f_width + 2 * D) // tn, lambda j: c0 + j,
                       gate_tile0=f_width // tn)

    pe, po, ph = fourier_channel_stage(fg, B, S, cg, wts["cos_c"], wts["sin_c"])
    cos_s, sin_s = _dft_tables(S, S // 2)
    fr = fourier_sequence_stage(pe, po, ph, cos_s.astype(BF16), (-sin_s).astype(BF16), S,
                                float(1.0 / math.sqrt(S * cg)))

    merged = merge_branches(outs, lses, fr, wts["w_branch_attn"], wts["w_branch_fourier"], fg, f_width, hpg)
    return out_projection(merged, wts["w_out"], x2d)


def _moe_and_final_norm(x1s, wts):
    t_all = sum(x1.shape[0] for x1 in x1s)
    tm = _tile(math.gcd(*[x1.shape[0] for x1 in x1s]), 256, 8)
    eid, cw, hp = route(x1s, wts["ffn_norm_g"], wts["w_router"], wts["b_router"], tm)
    tme = _tile(2 * t_all, EXPERT_ROW_TILE, 8)
    src, dst, tile_expert, n_used, n_tiles = _sorted_layout(eid[:, :2], tme)
    yt = expert_ffn(hp, wts["w_expert_gate"], wts["w_expert_up"], wts["w_expert_down"], src, dst, tile_expert,
                    n_used, n_tiles, tme, 2 * t_all + tme)
    outs, tile0 = [], 0
    for x1 in x1s:
        outs.append(final_combine(x1, cw, yt, wts["final_norm_g"], tm, tile0, t_all // tm))
        tile0 += x1.shape[0] // tm
    return outs


def kernel(x_prompt, x_sample, attn_norm_g, w_in, w_branch_attn, w_branch_fourier, b_gate, w_out, ffn_norm_g, w_router_group, b_router_group, w_router_expert, b_router_expert, w_expert_gate, w_expert_up, w_expert_down, final_norm_g):
    assert w_in.shape[0] == 1, "the final norm is fused into the layer's last kernel: one layer only"
    l = 0
    D = x_prompt.shape[-1]
    hpg = w_branch_attn.shape[1] // HEAD_DIM
    cg = w_branch_fourier.shape[1] // N_FOURIER_GROUPS
    in_width = w_in.shape[2]
    n_gate = b_gate.shape[1]
    cos_c, sin_c = _dft_tables(cg)
    w_r = jnp.concatenate(
        [w_router_group[l], jnp.transpose(w_router_expert[l], (1, 0, 2)).reshape(D, N_EXPERTS)], axis=1)
    b_r = jnp.concatenate([b_router_group[l], b_router_expert[l].reshape(-1)])
    pad = LANES - w_r.shape[1]
    wts = dict(
        hpg=hpg, cg=cg, cos_c=cos_c.astype(BF16), sin_c=sin_c.astype(BF16),
        attn_norm_g=attn_norm_g[l], ffn_norm_g=ffn_norm_g[l], final_norm_g=final_norm_g,
        w_in=w_in[l].astype(BF16),
        b_full=jnp.concatenate([jnp.zeros((in_width - n_gate,), F32), b_gate[l].astype(F32)]).reshape(1, in_width),
        w_branch_attn=w_branch_attn[l].astype(BF16),
        w_branch_fourier=w_branch_fourier[l].astype(BF16),
        w_out=w_out[l].astype(BF16),
        w_router=jnp.pad(w_r, ((0, 0), (0, pad))).astype(BF16),
        b_router=jnp.pad(b_r, (0, pad)).reshape(1, LANES).astype(F32),
        w_expert_gate=w_expert_gate[l].astype(BF16),
        w_expert_up=w_expert_up[l].astype(BF16),
        w_expert_down=w_expert_down[l].astype(BF16),
    )
    xs = (x_prompt, x_sample)
    outs = _moe_and_final_norm([_encoder_trunk(x, wts) for x in xs], wts)
    return tuple(o.reshape(x.shape) for o, x in zip(outs, xs))
```

```python
import functools
import math

import numpy as np
import jax
import jax.numpy as jnp
from jax import lax
from jax.experimental import pallas as pl
from jax.experimental.pallas import tpu as pltpu

F32 = jnp.float32
BF16 = jnp.bfloat16
U32 = jnp.uint32

RMS_EPS = 1e-6
NEG_INF = -1e30
HEAD_DIM = 128
ATTN_PATTERNS = ((128, 1), (512, 4), (2048, 16))
N_FOURIER_GROUPS = 4
N_EXPERT_GROUPS = 4
EXPERTS_PER_GROUP = 4
N_EXPERTS = N_EXPERT_GROUPS * EXPERTS_PER_GROUP
LANES = 128
V7X_MXU_COLS = 256
V7X_VMEM_LIMIT = 56 * 1024 * 1024
ATTN_Q_TILE = 128
ATTN_ROW_TILE = 1024
EXPERT_ROW_TILE = 512
HI_MASK = 0xFFFF0000


def _tile(n, pref, mult=LANES):
    if n <= pref:
        return n
    t = (pref // mult) * mult
    while t >= mult:
        if n % t == 0:
            return t
        t -= mult
    raise ValueError(f"no tile for {n} <= {pref}")


def _params(sem, vmem=V7X_VMEM_LIMIT):
    return pltpu.CompilerParams(dimension_semantics=sem, vmem_limit_bytes=vmem)


def _pack_pairs(lo, hi):
    a = pltpu.bitcast(lo.astype(BF16).astype(F32), U32) >> 16
    b = pltpu.bitcast(hi.astype(BF16).astype(F32), U32) & jnp.uint32(HI_MASK)
    return a | b


def _unpack_pairs(w):
    return pltpu.bitcast(w << 16, F32), pltpu.bitcast(w & jnp.uint32(HI_MASK), F32)


def _rms_kernel(x_ref, g_ref, o_ref, *cm_refs, dils):
    x = x_ref[...]
    ms = jnp.mean(x * x, axis=-1, keepdims=True)
    h = (x * lax.rsqrt(ms + RMS_EPS) * g_ref[...]).astype(o_ref.dtype)
    o_ref[...] = h
    tm = x.shape[0]
    row = lax.broadcasted_iota(jnp.int32, (tm, tm), 0)
    col = lax.broadcasted_iota(jnp.int32, (tm, tm), 1)
    for cm_ref, d in zip(cm_refs, dils):
        n = tm // d
        perm = (col == (row % n) * d + row // n).astype(h.dtype)
        hp = jnp.dot(perm, h, preferred_element_type=F32).astype(cm_ref.dtype)
        for r in range(d):
            cm_ref[0, r] = hp[r * n:(r + 1) * n]


def rmsnorm_cast(x, g, B, S, dils):
    T, D = x.shape
    tm = _tile(S, 256, 16 * max(dils, default=1))
    spb = S // tm
    outs = pl.pallas_call(
        functools.partial(_rms_kernel, dils=tuple(dils)),
        grid=(T // tm,),
        in_specs=[pl.BlockSpec((tm, D), lambda i: (i, 0)), pl.BlockSpec((1, D), lambda i: (0, 0))],
        out_specs=[pl.BlockSpec((tm, D), lambda i: (i, 0))]
        + [pl.BlockSpec((1, d, tm // d, D), lambda i: (i // spb, 0, i % spb, 0)) for d in dils],
        out_shape=[jax.ShapeDtypeStruct((T, D), BF16)]
        + [jax.ShapeDtypeStruct((B, d, S // d, D), BF16) for d in dils],
        compiler_params=_params(("parallel",)),
    )(x, g.reshape(1, D).astype(F32))
    return outs[0], {d: o.reshape(T, D) for d, o in zip(dils, outs[1:])}


def _in_proj_kernel(h_ref, w_ref, b_ref, o_ref, *, gate_tile0):
    def plain():
        o_ref[...] = jnp.dot(h_ref[...], w_ref[...], preferred_element_type=F32).astype(o_ref.dtype)

    if gate_tile0 is None:
        plain()
        return
    j = pl.program_id(1)
    pl.when(j < gate_tile0)(plain)

    @pl.when(j >= gate_tile0)
    def _():
        h = h_ref[...]
        for c0 in range(0, o_ref.shape[1], V7X_MXU_COLS):
            cols = slice(c0, c0 + V7X_MXU_COLS)
            acc = jnp.dot(h, w_ref[:, cols], preferred_element_type=F32)
            o_ref[:, cols] = jax.nn.sigmoid(acc + b_ref[:, cols]).astype(o_ref.dtype)


def in_projection(h, w, b_full, tn, n_out, w_col, gate_tile0=None):
    T, K = h.shape
    tm = _tile(T, 1024)
    return pl.pallas_call(
        functools.partial(_in_proj_kernel, gate_tile0=gate_tile0),
        grid=(T // tm, n_out),
        in_specs=[pl.BlockSpec((tm, K), lambda i, j: (i, 0)),
                  pl.BlockSpec((K, tn), lambda i, j: (0, w_col(j))),
                  pl.BlockSpec((1, tn), lambda i, j: (0, w_col(j)))],
        out_specs=pl.BlockSpec((tm, tn), lambda i, j: (i, j)),
        out_shape=jax.ShapeDtypeStruct((T, n_out * tn), BF16),
        compiler_params=_params(("parallel", "arbitrary")),
    )(h, w, b_full)


def _attn_kernel(q_ref, kp_ref, kc_ref, kn_ref, vp_ref, vc_ref, vn_ref, o_ref, lse_ref, so, sl,
                 *, tq, sub, half, hps, dil, cls_len, slopes):
    i = pl.program_id(1)
    hh = pl.program_id(2)
    key_pad = -(sub + 2 * half) % LANES
    span = sub + 2 * half + key_pad
    scale = HEAD_DIM ** -0.5
    qidx = lax.broadcasted_iota(jnp.int32, (sub, span), 0)
    kidx = lax.broadcasted_iota(jnp.int32, (sub, span), 1)
    absrel = jnp.abs(kidx - half - qidx)
    absrel_f = absrel.astype(F32)
    lane = lax.broadcasted_iota(jnp.int32, (sub, LANES), 1)
    n_sub = tq // sub
    valids = []
    for u in range(n_sub):
        kpos = i * tq + (u * sub - half) + kidx
        valids.append((absrel <= half) & (kpos >= 0) & (kpos < cls_len))
    neg_slopes = []
    for h in range(hps):
        s_h = jnp.float32(0.0)
        for b in range(len(slopes) // hps):
            s_h = jnp.where(hh == b, jnp.float32(-slopes[b * hps + h] * dil), s_h)
        neg_slopes.append(s_h)

    @pl.when(hh == 0)
    def _():
        sl[...] = jnp.zeros_like(sl)

    def keys(prev_ref, cur_ref, next_ref, r, u, cols):
        lo = u * sub - half if u > 0 else 0
        hi = (u + 1) * sub + half if u < n_sub - 1 else tq
        parts = [cur_ref[0, r, lo:hi, cols]]
        if u == 0:
            parts.insert(0, prev_ref[0, r, :, cols])
        if u == n_sub - 1:
            parts.append(next_ref[0, r, :, cols])
        if key_pad:
            parts.append(jnp.zeros((key_pad, HEAD_DIM), cur_ref.dtype))
        return parts[0] if len(parts) == 1 else jnp.concatenate(parts, axis=0)

    for r in range(dil):
        for u in range(n_sub):
            rows = pl.ds(u * sub * dil + r, sub, stride=dil) if dil > 1 else pl.ds(u * sub, sub)
            lse_tile = sl[rows, :]
            for h in range(hps):
                cols = slice(h * HEAD_DIM, (h + 1) * HEAD_DIM)
                q = q_ref[0, r, u * sub:(u + 1) * sub, cols]
                k = keys(kp_ref, kc_ref, kn_ref, r, u, cols)
                v = keys(vp_ref, vc_ref, vn_ref, r, u, cols)
                s = lax.dot_general(q, k, (((1,), (1,)), ((), ())), preferred_element_type=F32)
                s = s * scale + absrel_f * neg_slopes[h]
                s = jnp.where(valids[u], s, NEG_INF)
                m = jnp.max(s, axis=-1, keepdims=True)
                p = jnp.exp(s - m)
                den = jnp.sum(p, axis=-1, keepdims=True)
                so[h, rows, :] = jnp.dot(p.astype(v.dtype), v, preferred_element_type=F32) / den
                lse_tile = jnp.where(lane == hh * hps + h, m + jnp.log(den), lse_tile)
            sl[rows, :] = lse_tile
    for h in range(hps):
        o_ref[0, :, h * HEAD_DIM:(h + 1) * HEAD_DIM] = so[h].astype(o_ref.dtype)
    lse_ref[0] = sl[...]


def band_attention(qkv, B, S, group, hpg, n_heads_total):
    window, dil = ATTN_PATTERNS[group]
    half = window // (2 * dil)
    W = hpg * HEAD_DIM
    L = S // dil
    tq = min(L, max(ATTN_Q_TILE, ATTN_ROW_TILE // dil))
    rows = tq * dil
    sub = min(ATTN_Q_TILE, tq)
    hps = max(1, hpg * ATTN_ROW_TILE // max(rows, ATTN_ROW_TILE))
    assert S % dil == 0 and L % tq == 0 and tq % sub == 0 and tq % half == 0 and half % 16 == 0 and hpg % hps == 0
    n_hh = hpg // hps
    Wb = hps * HEAD_DIM
    hb = tq // half
    n_hb = L // half
    s_all = 2.0 ** (-8.0 * np.arange(1, n_heads_total + 1) / n_heads_total)
    slopes = tuple(float(np.float32(v)) for v in s_all[group * hpg:(group + 1) * hpg])
    qv = qkv.reshape(B, dil, L, 3 * W)

    cur = lambda part: pl.BlockSpec((1, dil, tq, Wb), lambda b, i, hh: (b, 0, i, part * n_hh + hh))
    prv = lambda part: pl.BlockSpec(
        (1, dil, half, Wb), lambda b, i, hh: (b, 0, jnp.maximum(i * hb - 1, 0), part * n_hh + hh))
    nxt = lambda part: pl.BlockSpec(
        (1, dil, half, Wb), lambda b, i, hh: (b, 0, jnp.minimum((i + 1) * hb, n_hb - 1), part * n_hh + hh))
    o, lse = pl.pallas_call(
        functools.partial(_attn_kernel, tq=tq, sub=sub, half=half, hps=hps, dil=dil, cls_len=L, slopes=slopes),
        grid=(B, L // tq, n_hh),
        in_specs=[cur(0), prv(1), cur(1), nxt(1), prv(2), cur(2), nxt(2)],
        out_specs=[pl.BlockSpec((1, rows, Wb), lambda b, i, hh: (b, i, hh)),
                   pl.BlockSpec((1, rows, LANES), lambda b, i, hh: (b, i, 0))],
        out_shape=[jax.ShapeDtypeStruct((B, S, W), BF16), jax.ShapeDtypeStruct((B, S, LANES), F32)],
        scratch_shapes=[pltpu.VMEM((hps, rows, LANES), F32), pltpu.VMEM((rows, LANES), F32)],
        compiler_params=_params(("parallel", "arbitrary", "arbitrary")),
    )(qv, qv, qv, qv, qv, qv, qv)
    return o.reshape(B * S, W), lse.reshape(B * S, LANES)


def _dft_tables(n, cols=None):
    cols = n if cols is None else cols
    k = jnp.arange(n, dtype=jnp.int32)

    def thin(s):
        ang = ((k[:, None] * s[None, :]) % n).astype(F32) * np.float32(2.0 * np.pi / n)
        return jnp.cos(ang), jnp.sin(ang)

    m = 64
    if cols <= m or cols % m:
        return thin(jnp.arange(cols, dtype=jnp.int32))
    c_hi, s_hi = thin(jnp.arange(cols // m, dtype=jnp.int32) * m)
    c_lo, s_lo = thin(jnp.arange(m, dtype=jnp.int32))
    cos = c_hi[:, :, None] * c_lo[:, None, :] - s_hi[:, :, None] * s_lo[:, None, :]
    sin = s_hi[:, :, None] * c_lo[:, None, :] + c_hi[:, :, None] * s_lo[:, None, :]
    return cos.reshape(n, cols), sin.reshape(n, cols)


def _fourier_chan_kernel(xd_ref, xa_ref, xb_ref, xh_ref, cos_ref, sin_ref, pe_ref, po_ref, ph_ref):
    i = pl.program_id(1)
    ts = xd_ref.shape[0]
    cg = cos_ref.shape[0]
    row = lax.broadcasted_iota(jnp.int32, (ts, ts), 0)
    col = lax.broadcasted_iota(jnp.int32, (ts, ts), 1)
    flip = (col == ts - row).astype(xa_ref.dtype)
    rev = jnp.dot(flip, xa_ref[...], preferred_element_type=F32)
    first = lax.broadcasted_iota(jnp.int32, rev.shape, 0) == 0
    rev = jnp.where(first, xb_ref[0:1, :].astype(F32), rev)
    xd = xd_ref[...].astype(F32)
    xe = xd + jnp.where(first & (i == 0), 0.0, rev)
    xo = xd - rev
    xe, xo = xe.astype(BF16), xo.astype(BF16)
    for g in range(N_FOURIER_GROUPS):
        cols = slice(g * cg, (g + 1) * cg)
        pe_ref[0, :, cols] = jnp.dot(xe[:, cols], cos_ref[...], preferred_element_type=F32).astype(pe_ref.dtype)
        po_ref[0, :, cols] = jnp.dot(xo[:, cols], sin_ref[...], preferred_element_type=F32).astype(po_ref.dtype)

    @pl.when(i == 0)
    def _():
        for g in range(N_FOURIER_GROUPS):
            cols = slice(g * cg, (g + 1) * cg)
            ph_ref[0, :, cols] = jnp.dot(xh_ref[:, cols], cos_ref[...],
                                         preferred_element_type=F32).astype(ph_ref.dtype)


def fourier_channel_stage(fg, B, S, cg, cos_c, sin_c):
    half = S // 2
    ts = _tile(half, 256, 16)
    nt = half // ts
    n_all = N_FOURIER_GROUPS * cg
    sub = 16
    return pl.pallas_call(
        _fourier_chan_kernel,
        grid=(B, nt),
        in_specs=[pl.BlockSpec((ts, n_all), lambda b, i: (b * (S // ts) + i, 0)),
                  pl.BlockSpec((ts, n_all), lambda b, i: (b * (S // ts) + (S // ts - 1 - i), 0)),
                  pl.BlockSpec((sub, n_all), lambda b, i: (b * (S // sub) + ((S - i * ts) % S) // sub, 0)),
                  pl.BlockSpec((sub, n_all), lambda b, i: (b * (S // sub) + half // sub, 0)),
                  pl.BlockSpec((cg, cg), lambda b, i: (0, 0)),
                  pl.BlockSpec((cg, cg), lambda b, i: (0, 0))],
        out_specs=[pl.BlockSpec((1, ts, n_all), lambda b, i: (b, i, 0)),
                   pl.BlockSpec((1, ts, n_all), lambda b, i: (b, i, 0)),
                   pl.BlockSpec((1, sub, n_all), lambda b, i: (b, 0, 0))],
        out_shape=[jax.ShapeDtypeStruct((B, half, n_all), BF16), jax.ShapeDtypeStruct((B, half, n_all), BF16),
                   jax.ShapeDtypeStruct((B, sub, n_all), BF16)],
        compiler_params=_params(("parallel", "arbitrary")),
    )(fg, fg, fg, fg, cos_c, sin_c)


def _fourier_seq_kernel(cs_ref, ss_ref, pe_ref, po_ref, ph_ref, o_ref, acc_ref, *, scale):
    k = pl.program_id(3)

    @pl.when(k == 0)
    def _():
        acc_ref[...] = jnp.zeros_like(acc_ref)

    acc_ref[...] += (jnp.dot(cs_ref[...], pe_ref[0], preferred_element_type=F32)
                     + jnp.dot(ss_ref[...], po_ref[0], preferred_element_type=F32))

    @pl.when(k == pl.num_programs(3) - 1)
    def _():
        tm = acc_ref.shape[0]
        freq = pl.program_id(1) * tm + lax.broadcasted_iota(jnp.int32, (tm, 1), 0)
        sign = (1 - 2 * (freq & 1)).astype(F32)
        o_ref[0] = ((acc_ref[...] + sign * ph_ref[0, 0:1, :].astype(F32)) * scale).astype(o_ref.dtype)


def fourier_sequence_stage(pe, po, ph, cos_s, neg_sin_s, S, scale):
    B, half, N = pe.shape
    tm, tn, tk = _tile(S, 1024), _tile(N, 1024), _tile(half, 2048)
    out = pl.pallas_call(
        functools.partial(_fourier_seq_kernel, scale=scale),
        grid=(B, S // tm, N // tn, half // tk),
        in_specs=[pl.BlockSpec((tm, tk), lambda b, i, j, k: (i, k)),
                  pl.BlockSpec((tm, tk), lambda b, i, j, k: (i, k)),
                  pl.BlockSpec((1, tk, tn), lambda b, i, j, k: (b, k, j)),
                  pl.BlockSpec((1, tk, tn), lambda b, i, j, k: (b, k, j)),
                  pl.BlockSpec((1, ph.shape[1], tn), lambda b, i, j, k: (b, 0, j))],
        out_specs=pl.BlockSpec((1, tm, tn), lambda b, i, j, k: (b, i, j)),
        out_shape=jax.ShapeDtypeStruct((B, S, N), BF16),
        scratch_shapes=[pltpu.VMEM((tm, tn), F32)],
        compiler_params=_params(("parallel", "parallel", "parallel", "arbitrary")),
    )(cos_s, neg_sin_s, pe, po, ph)
    return out.reshape(B * S, N)


def _merge_kernel(o0_ref, o1_ref, o2_ref, l0_ref, l1_ref, l2_ref, fr_ref, wa_ref, wf_ref, *refs, hpg, n_b):
    ga_refs, gf_refs = refs[:n_b], refs[n_b:2 * n_b]
    m_ref, oc_ref = refs[2 * n_b], refs[2 * n_b + 1]
    bw = ga_refs[0].shape[1]
    fr = fr_ref[...]
    fours = [jnp.dot(fr, wf_ref[:, c * bw:(c + 1) * bw], preferred_element_type=F32) for c in range(n_b)]
    l0, l1, l2 = l0_ref[...], l1_ref[...], l2_ref[...]
    m = jnp.maximum(jnp.maximum(l0, l1), l2)
    e0, e1, e2 = jnp.exp(l0 - m), jnp.exp(l1 - m), jnp.exp(l2 - m)
    den = e0 + e1 + e2
    w0, w1, w2 = e0 / den, e1 / den, e2 / den
    for h in range(hpg):
        cols = slice(h * HEAD_DIM, (h + 1) * HEAD_DIM)
        oc = (w0[:, h:h + 1] * o0_ref[:, cols].astype(F32)
              + w1[:, h:h + 1] * o1_ref[:, cols].astype(F32)
              + w2[:, h:h + 1] * o2_ref[:, cols].astype(F32))
        oc_ref[:, cols] = oc.astype(oc_ref.dtype)
    oc_all = oc_ref[...]
    for c in range(n_b):
        cols = slice(c * bw, (c + 1) * bw)
        attn = jnp.dot(oc_all, wa_ref[:, cols], preferred_element_type=F32)
        m_ref[:, cols] = (ga_refs[c][...].astype(F32) * attn
                          + gf_refs[c][...].astype(F32) * fours[c]).astype(m_ref.dtype)


def merge_branches(os_, lses, fr, wa, wf, fg, gate_start, hpg):
    T, W = os_[0].shape
    D = wa.shape[1]
    FW = fr.shape[1]
    tm = _tile(T, 256)
    bw = math.gcd(D, gate_start)
    n_b, g0 = D // bw, gate_start // bw
    row = lambda w: pl.BlockSpec((tm, w), lambda i: (i, 0))
    resident = lambda r: pl.BlockSpec((r, D), lambda i: (0, 0), pipeline_mode=pl.Buffered(1))
    gate = lambda blk: pl.BlockSpec((tm, bw), lambda i: (i, blk))
    return pl.pallas_call(
        functools.partial(_merge_kernel, hpg=hpg, n_b=n_b),
        grid=(T // tm,),
        in_specs=[row(W), row(W), row(W), row(LANES), row(LANES), row(LANES), row(FW), resident(W), resident(FW)]
        + [gate(g0 + c) for c in range(2 * n_b)],
        out_specs=pl.BlockSpec((tm, D), lambda i: (i, 0)),
        out_shape=jax.ShapeDtypeStruct((T, D), BF16),
        scratch_shapes=[pltpu.VMEM((tm, W), BF16)],
        compiler_params=_params(("parallel",)),
    )(*os_, *lses, fr, wa, wf, *([fg] * (2 * n_b)))


def _out_proj_kernel(m_ref, w_ref, x_ref, o_ref):
    o_ref[...] = x_ref[...] + jnp.dot(m_ref[...], w_ref[...], preferred_element_type=F32)


def out_projection(merged, w, x):
    T, K = merged.shape
    N = w.shape[1]
    tm, tn = _tile(T, 1024), _tile(N, 1024)
    return pl.pallas_call(
        _out_proj_kernel,
        grid=(T // tm, N // tn),
        in_specs=[pl.BlockSpec((tm, K), lambda i, j: (i, 0)),
                  pl.BlockSpec((K, tn), lambda i, j: (0, j)),
                  pl.BlockSpec((tm, tn), lambda i, j: (i, j))],
        out_specs=pl.BlockSpec((tm, tn), lambda i, j: (i, j)),
        out_shape=jax.ShapeDtypeStruct((T, N), F32),
        compiler_params=_params(("parallel", "arbitrary")),
    )(merged, w, x)


def _router_kernel(*refs, bounds):
    n_in = len(bounds) - 1
    i = pl.program_id(0)
    for k in range(n_in):
        @pl.when((i >= bounds[k]) & (i < bounds[k + 1]))
        def _(k=k):
            _route_tile(refs[k], *refs[n_in:])


def _route_tile(x_ref, g_ref, w_ref, b_ref, eid_ref, cw_ref, hp_ref):
    x = x_ref[...]
    ms = jnp.mean(x * x, axis=-1, keepdims=True)
    hf = x * lax.rsqrt(ms + RMS_EPS) * g_ref[...]
    h = hf.astype(BF16)
    half_d = hf.shape[1] // 2
    packed = _pack_pairs(hf[:, :half_d], hf[:, half_d:])
    for c in range(hp_ref.shape[1]):
        hp_ref[:, c, :] = packed[:, c * LANES:(c + 1) * LANES]
    lg = jnp.dot(h, w_ref[...], preferred_element_type=F32) + b_ref[...]
    lane = lax.broadcasted_iota(jnp.int32, lg.shape, 1)
    big = jnp.int32(LANES)
    in_grp = lane < N_EXPERT_GROUPS
    gl = jnp.where(in_grp, lg, -jnp.inf)
    gmax = jnp.max(gl, axis=-1, keepdims=True)
    gsel = jnp.min(jnp.where(gl == gmax, lane, big), axis=-1, keepdims=True)
    p_group = 1.0 / jnp.sum(jnp.where(in_grp, jnp.exp(gl - gmax), 0.0), axis=-1, keepdims=True)
    lo = N_EXPERT_GROUPS + gsel * EXPERTS_PER_GROUP
    in_sel = (lane >= lo) & (lane < lo + EXPERTS_PER_GROUP)
    el = jnp.where(in_sel, lg, -jnp.inf)
    t1 = jnp.max(el, axis=-1, keepdims=True)
    i1 = jnp.min(jnp.where(el == t1, lane, big), axis=-1, keepdims=True)
    el2 = jnp.where(lane == i1, -jnp.inf, el)
    t2 = jnp.max(el2, axis=-1, keepdims=True)
    i2 = jnp.min(jnp.where(el2 == t2, lane, big), axis=-1, keepdims=True)
    e21 = jnp.exp(t2 - t1)
    p1 = 1.0 / (1.0 + e21)
    p2 = e21 / (1.0 + e21)
    eid_ref[...] = jnp.where(lane == 0, i1 - N_EXPERT_GROUPS, jnp.where(lane == 1, i2 - N_EXPERT_GROUPS, 0))
    cw_ref[...] = jnp.where(lane == 0, p_group * p1, jnp.where(lane == 1, p_group * p2, 0.0))


def route(x1s, g, w_router, b_router, tm):
    D = x1s[0].shape[1]
    bounds = [0]
    for x1 in x1s:
        bounds.append(bounds[-1] + x1.shape[0] // tm)
    T = bounds[-1] * tm

    def x_spec(k):
        lo, n = bounds[k], bounds[k + 1] - bounds[k]
        return pl.BlockSpec((tm, D), lambda i: (jnp.clip(i - lo, 0, n - 1), 0))

    return pl.pallas_call(
        functools.partial(_router_kernel, bounds=tuple(bounds)),
        grid=(bounds[-1],),
        in_specs=[x_spec(k) for k in range(len(x1s))]
        + [pl.BlockSpec((1, D), lambda i: (0, 0)),
           pl.BlockSpec((D, LANES), lambda i: (0, 0)),
           pl.BlockSpec((1, LANES), lambda i: (0, 0))],
        out_specs=[pl.BlockSpec((tm, LANES), lambda i: (i, 0)), pl.BlockSpec((tm, LANES), lambda i: (i, 0)),
                   pl.BlockSpec((tm, D // (2 * LANES), LANES), lambda i: (i, 0, 0))],
        out_shape=[jax.ShapeDtypeStruct((T, LANES), jnp.int32), jax.ShapeDtypeStruct((T, LANES), F32),
                   jax.ShapeDtypeStruct((T, D // (2 * LANES), LANES), U32)],
        compiler_params=_params(("arbitrary",)),
    )(*x1s, g.reshape(1, D).astype(F32), w_router, b_router)


def _expert_up_kernel(src_ref, te_ref, nused_ref, hp_hbm, wg_ref, wu_ref, a_ref, buf, sem, *, tme, nc, n_k):
    i = pl.program_id(0)
    n_used = nused_ref[0]
    half_d = nc * LANES

    def row_copy(tile, slot, r):
        return pltpu.make_async_copy(hp_hbm.at[src_ref[tile * tme + r]], buf.at[slot, pl.ds(r * nc, nc), :],
                                     sem.at[slot])

    def drain(tile, slot):
        def body(r, c):
            row_copy(tile, slot, r).wait()
            return c
        lax.fori_loop(0, tme, body, 0, unroll=8)

    @pl.when(i == 0)
    def _():
        def body(r, c):
            row_copy(0, 0, r).start()
            return c
        lax.fori_loop(0, tme, body, 0, unroll=8)

    @pl.when(i < n_used)
    def _():
        slot = i % 2
        drain(i, slot)
        nxt = jnp.minimum(i + 1, n_used - 1)
        for r in range(tme):
            row_copy(nxt, 1 - slot, r).start(priority=r % 2)
        cpk = nc // n_k
        wk = cpk * LANES
        gate = up = None
        for q in range(n_k):
            parts = [_unpack_pairs(buf[slot, pl.ds(q * cpk + cc, tme, stride=nc), :]) for cc in range(cpk)]
            lo = jnp.concatenate([p[0].astype(BF16) for p in parts], axis=1)
            hi = jnp.concatenate([p[1].astype(BF16) for p in parts], axis=1)
            rows_lo = slice(q * wk, (q + 1) * wk)
            rows_hi = slice(half_d + q * wk, half_d + (q + 1) * wk)
            g = (jnp.dot(lo, wg_ref[0, rows_lo, :], preferred_element_type=F32)
                 + jnp.dot(hi, wg_ref[0, rows_hi, :], preferred_element_type=F32))
            u = (jnp.dot(lo, wu_ref[0, rows_lo, :], preferred_element_type=F32)
                 + jnp.dot(hi, wu_ref[0, rows_hi, :], preferred_element_type=F32))
            gate = g if gate is None else gate + g
            up = u if up is None else up + u
        a_ref[...] = (jax.nn.silu(gate) * up).astype(a_ref.dtype)

    @pl.when(i == n_used - 1)
    def _():
        drain(i, 1 - i % 2)

    @pl.when(i >= n_used)
    def _():
        a_ref[...] = jnp.zeros_like(a_ref)


def _expert_down_kernel(dst_ref, te_ref, nused_ref, a_ref, wd_ref, yt_hbm, ybuf, sem, *, tme, n_split, n_tiles):
    i = pl.program_id(0)
    n_used = nused_ref[0]
    last = n_tiles - 1
    half_d = ybuf.shape[2]
    wc = half_d // n_split

    def row_copy(tile, r):
        slot = (tile + 3) % 3
        return pltpu.make_async_copy(ybuf.at[slot, pl.ds(r, 1), :],
                                     yt_hbm.at[pl.ds(dst_ref[(tile + 1) * tme + r], 1), :], sem.at[slot])

    def drain(tile):
        def body(r, c):
            row_copy(tile, r).wait()
            return c
        lax.fori_loop(0, tme, body, 0, unroll=8)

    @pl.when(i == 0)
    def _():
        ybuf[2] = jnp.zeros(ybuf.shape[1:], ybuf.dtype)

    @pl.when((i >= 2) & (i - 3 < n_used))
    def _():
        drain(i - 3)

    @pl.when(i < n_used)
    def _():
        a = a_ref[...]
        for r in range(tme):
            row_copy(i - 1, r).start(priority=r % 2)
        for k in range(n_split):
            lo = jnp.dot(a, wd_ref[0, :, k * wc:(k + 1) * wc], preferred_element_type=F32)
            hi = jnp.dot(a, wd_ref[0, :, half_d + k * wc:half_d + (k + 1) * wc], preferred_element_type=F32)
            ybuf[i % 3, :, k * wc:(k + 1) * wc] = _pack_pairs(lo, hi)

    @pl.when(i == n_used)
    def _():
        def body(r, c):
            row_copy(i - 1, r).start()
            return c
        lax.fori_loop(0, tme, body, 0, unroll=8)

    @pl.when((i == last) & (last - 2 < n_used))
    def _():
        drain(last - 2)

    @pl.when((i == last) & (last - 1 < n_used))
    def _():
        drain(last - 1)


def expert_ffn(hp, w_gate, w_up, w_down, src, dst, tile_expert, n_used, n_tiles, tme, n_rows_out):
    nc = hp.shape[1]
    D = nc * 2 * LANES
    F = w_gate.shape[2]
    P = n_tiles * tme
    act = pl.pallas_call(
        functools.partial(_expert_up_kernel, tme=tme, nc=nc, n_k=max(1, nc // 4)),
        grid_spec=pltpu.PrefetchScalarGridSpec(
            num_scalar_prefetch=3,
            grid=(n_tiles,),
            in_specs=[pl.BlockSpec(memory_space=pl.ANY),
                      pl.BlockSpec((1, D, F), lambda i, s, te, n: (te[i], 0, 0)),
                      pl.BlockSpec((1, D, F), lambda i, s, te, n: (te[i], 0, 0))],
            out_specs=pl.BlockSpec((tme, F), lambda i, s, te, n: (i, 0)),
            scratch_shapes=[pltpu.VMEM((2, tme * nc, LANES), U32), pltpu.SemaphoreType.DMA((2,))],
        ),
        out_shape=jax.ShapeDtypeStruct((P, F), BF16),
        compiler_params=_params(("arbitrary",)),
    )(src, tile_expert, n_used, hp, w_gate, w_up)
    n_split = max(1, D // 1024)
    spare = n_rows_out - tme + jnp.arange(tme, dtype=jnp.int32)
    return pl.pallas_call(
        functools.partial(_expert_down_kernel, tme=tme, n_split=n_split, n_tiles=n_tiles),
        grid_spec=pltpu.PrefetchScalarGridSpec(
            num_scalar_prefetch=3,
            grid=(n_tiles,),
            in_specs=[pl.BlockSpec((tme, F), lambda i, d, te, n: (i, 0)),
                      pl.BlockSpec((1, F, D), lambda i, d, te, n: (te[i], 0, 0))],
            out_specs=pl.BlockSpec(memory_space=pl.ANY),
            scratch_shapes=[pltpu.VMEM((3, tme, D // 2), U32), pltpu.SemaphoreType.DMA((3,))],
        ),
        out_shape=jax.ShapeDtypeStruct((n_rows_out, D // 2), U32),
        compiler_params=_params(("arbitrary",)),
    )(jnp.concatenate([spare, dst]), tile_expert, n_used, act, w_down)


def _final_kernel(x_ref, cw_ref, y0_ref, y1_ref, g_ref, o_ref):
    half_d = y0_ref.shape[1]
    cw0 = cw_ref[:, 0:1]
    cw1 = cw_ref[:, 1:2]
    lo0, hi0 = _unpack_pairs(y0_ref[...])
    lo1, hi1 = _unpack_pairs(y1_ref[...])
    xl = x_ref[:, :half_d] + (cw0 * lo0 + cw1 * lo1)
    xh = x_ref[:, half_d:] + (cw0 * hi0 + cw1 * hi1)
    ss = jnp.sum(xl * xl, axis=-1, keepdims=True) + jnp.sum(xh * xh, axis=-1, keepdims=True)
    inv = lax.rsqrt(ss / (2 * half_d) + RMS_EPS)
    o_ref[:, :half_d] = xl * inv * g_ref[:, :half_d]
    o_ref[:, half_d:] = xh * inv * g_ref[:, half_d:]


def final_combine(x1, cw, yt, g, tm, tile0, n_tok_tiles):
    T, D = x1.shape
    return pl.pallas_call(
        _final_kernel,
        grid=(T // tm,),
        in_specs=[pl.BlockSpec((tm, D), lambda i: (i, 0)),
                  pl.BlockSpec((tm, LANES), lambda i: (tile0 + i, 0)),
                  pl.BlockSpec((tm, D // 2), lambda i: (tile0 + i, 0)),
                  pl.BlockSpec((tm, D // 2), lambda i: (n_tok_tiles + tile0 + i, 0)),
                  pl.BlockSpec((1, D), lambda i: (0, 0))],
        out_specs=pl.BlockSpec((tm, D), lambda i: (i, 0)),
        out_shape=jax.ShapeDtypeStruct((T, D), F32),
        compiler_params=_params(("parallel",)),
    )(x1, cw, yt, yt, g.reshape(1, D).astype(F32))


def _sorted_layout(eid, tme):
    T = eid.shape[0]
    n_tiles = (2 * T) // tme + N_EXPERTS
    P = n_tiles * tme
    flat_e = eid.reshape(-1)
    onehot = (flat_e[:, None] == jnp.arange(N_EXPERTS, dtype=jnp.int32)[None, :]).astype(BF16)
    blk = _tile(2 * T, 256, 8)
    within = jnp.einsum("ij,bjk->bik", jnp.tril(jnp.ones((blk, blk), BF16)), onehot.reshape(-1, blk, N_EXPERTS),
                        preferred_element_type=F32)
    totals = within[:, -1, :]
    csum = (within + (jnp.cumsum(totals, axis=0) - totals)[:, None, :]).reshape(2 * T, N_EXPERTS).astype(jnp.int32)
    rank = jnp.take_along_axis(csum, flat_e[:, None], axis=1)[:, 0] - 1
    counts = csum[-1]
    padded = ((counts + tme - 1) // tme) * tme
    ends = jnp.cumsum(padded)
    pos = (ends - padded)[flat_e] + rank
    spare = 2 * T + jnp.arange(P, dtype=jnp.int32) % tme
    a = jnp.arange(2 * T, dtype=jnp.int32)
    dst = spare.at[pos].set((a % 2) * T + a // 2, unique_indices=True)
    src = jnp.where(dst < 2 * T, dst % T, 0)
    n_used = (ends[-1] // tme).astype(jnp.int32)
    tile_start = jnp.minimum(jnp.arange(n_tiles, dtype=jnp.int32), n_used - 1) * tme
    tile_expert = jnp.minimum(jnp.searchsorted(ends, tile_start, side="right"), N_EXPERTS - 1).astype(jnp.int32)
    return src, dst, tile_expert, n_used.reshape(1), n_tiles


def _encoder_trunk(x, wts):
    B, S, D = x.shape
    T = B * S
    hpg, cg = wts["hpg"], wts["cg"]
    n_grp = len(ATTN_PATTERNS)
    W = hpg * HEAD_DIM
    f_start = 3 * n_grp * W
    f_width = N_FOURIER_GROUPS * cg
    x2d = x.reshape(T, D)

    dils = sorted({d for _, d in ATTN_PATTERNS if d > 1})
    h, h_cm = rmsnorm_cast(x2d, wts["attn_norm_g"], B, S, dils)
    h_cm[1] = h

    outs, lses = [], []
    for g, (_, dil) in enumerate(ATTN_PATTERNS):
        qkv = in_projection(h_cm[dil], wts["w_in"], wts["b_full"], W, 3, lambda j, g=g: j * n_grp + g)
        o_g, lse_g = band_attention(qkv, B, S, g, hpg, n_grp * hpg)
        outs.append(o_g)
        lses.append(lse_g)

    tn = _tile(math.gcd(math.gcd(f_start, f_width), 2 * D), 1024)
    c0 = f_start // tn
    fg = in_projection(h, wts["w_in"], wts["b_full"], tn, (f_width + 2 * D) // tn, lambda j: c0 + j,
                       gate_tile0=f_width // tn)

    pe, po, ph = fourier_channel_stage(fg, B, S, cg, wts["cos_c"], wts["sin_c"])
    cos_s, sin_s = _dft_tables(S, S // 2)
    fr = fourier_sequence_stage(pe, po, ph, cos_s.astype(BF16), (-sin_s).astype(BF16), S,
                                float(1.0 / math.sqrt(S * cg)))

    merged = merge_branches(outs, lses, fr, wts["w_branch_attn"], wts["w_branch_fourier"], fg, f_width, hpg)
    return out_projection(merged, wts["w_out"], x2d)


def _moe_and_final_norm(x1s, wts):
    t_all = sum(x1.shape[0] for x1 in x1s)
    tm = _tile(math.gcd(*[x1.shape[0] for x1 in x1s]), 256, 8)
    eid, cw, hp = route(x1s, wts["ffn_norm_g"], wts["w_router"], wts["b_router"], tm)
    tme = _tile(2 * t_all, EXPERT_ROW_TILE, 8)
    src, dst, tile_expert, n_used, n_tiles = _sorted_layout(eid[:, :2], tme)
    yt = expert_ffn(hp, wts["w_expert_gate"], wts["w_expert_up"], wts["w_expert_down"], src, dst, tile_expert,
                    n_used, n_tiles, tme, 2 * t_all + tme)
    outs, tile0 = [], 0
    for x1 in x1s:
        outs.append(final_combine(x1, cw, yt, wts["final_norm_g"], tm, tile0, t_all // tm))
        tile0 += x1.shape[0] // tm
    return outs


def kernel(x_prompt, x_sample, attn_norm_g, w_in, w_branch_attn, w_branch_fourier, b_gate, w_out, ffn_norm_g, w_router_group, b_router_group, w_router_expert, b_router_expert, w_expert_gate, w_expert_up, w_expert_down, final_norm_g):
    assert w_in.shape[0] == 1, "the final norm is fused into the layer's last kernel: one layer only"
    l = 0
    D = x_prompt.shape[-1]
    hpg = w_branch_attn.shape[1] // HEAD_DIM
    cg = w_branch_fourier.shape[1] // N_FOURIER_GROUPS
    in_width = w_in.shape[2]
    n_gate = b_gate.shape[1]
    cos_c, sin_c = _dft_tables(cg)
    w_r = jnp.concatenate(
        [w_router_group[l], jnp.transpose(w_router_expert[l], (1, 0, 2)).reshape(D, N_EXPERTS)], axis=1)
    b_r = jnp.concatenate([b_router_group[l], b_router_expert[l].reshape(-1)])
    pad = LANES - w_r.shape[1]
    wts = dict(
        hpg=hpg, cg=cg, cos_c=cos_c.astype(BF16), sin_c=sin_c.astype(BF16),
        attn_norm_g=attn_norm_g[l], ffn_norm_g=ffn_norm_g[l], final_norm_g=final_norm_g,
        w_in=w_in[l].astype(BF16),
        b_full=jnp.concatenate([jnp.zeros((in_width - n_gate,), F32), b_gate[l].astype(F32)]).reshape(1, in_width),
        w_branch_attn=w_branch_attn[l].astype(BF16),
        w_branch_fourier=w_branch_fourier[l].astype(BF16),
        w_out=w_out[l].astype(BF16),
        w_router=jnp.pad(w_r, ((0, 0), (0, pad))).astype(BF16),
        b_router=jnp.pad(b_r, (0, pad)).reshape(1, LANES).astype(F32),
        w_expert_gate=w_expert_gate[l].astype(BF16),
        w_expert_up=w_expert_up[l].astype(BF16),
        w_expert_down=w_expert_down[l].astype(BF16),
    )
    xs = (x_prompt, x_sample)
    outs = _moe_and_final_norm([_encoder_trunk(x, wts) for x in xs], wts)
    return tuple(o.reshape(x.shape) for o, x in zip(outs, xs))
```

```python
import functools
import math

import numpy as np
import jax
import jax.numpy as jnp
from jax import lax
from jax.experimental import pallas as pl
from jax.experimental.pallas import tpu as pltpu

F32 = jnp.float32
BF16 = jnp.bfloat16
U32 = jnp.uint32

RMS_EPS = 1e-6
NEG_INF = -1e30
HEAD_DIM = 128
ATTN_PATTERNS = ((128, 1), (512, 4), (2048, 16))
N_FOURIER_GROUPS = 4
N_EXPERT_GROUPS = 4
EXPERTS_PER_GROUP = 4
N_EXPERTS = N_EXPERT_GROUPS * EXPERTS_PER_GROUP
LANES = 128
V7X_MXU_COLS = 256
V7X_VMEM_LIMIT = 56 * 1024 * 1024
ATTN_Q_TILE = 128
ATTN_ROW_TILE = 1024
EXPERT_ROW_TILE = 512
HI_MASK = 0xFFFF0000


def _tile(n, pref, mult=LANES):
    if n <= pref:
        return n
    t = (pref // mult) * mult
    while t >= mult:
        if n % t == 0:
            return t
        t -= mult
    raise ValueError(f"no tile for {n} <= {pref}")


def _params(sem, vmem=V7X_VMEM_LIMIT):
    return pltpu.CompilerParams(dimension_semantics=sem, vmem_limit_bytes=vmem)


def _pack_pairs(lo, hi):
    w = pltpu.pack_elementwise([lo, hi], packed_dtype=BF16)
    return w if w.dtype == U32 else pltpu.bitcast(w, U32)


def _unpack_pairs(w):
    return (pltpu.unpack_elementwise(w, index=0, packed_dtype=BF16, unpacked_dtype=F32),
            pltpu.unpack_elementwise(w, index=1, packed_dtype=BF16, unpacked_dtype=F32))


def _rms_kernel(x_ref, g_ref, o_ref, *cm_refs, dils):
    x = x_ref[...]
    ms = jnp.mean(x * x, axis=-1, keepdims=True)
    h = (x * lax.rsqrt(ms + RMS_EPS) * g_ref[...]).astype(o_ref.dtype)
    o_ref[...] = h
    tm = x.shape[0]
    row = lax.broadcasted_iota(jnp.int32, (tm, tm), 0)
    col = lax.broadcasted_iota(jnp.int32, (tm, tm), 1)
    for cm_ref, d in zip(cm_refs, dils):
        n = tm // d
        perm = (col == (row % n) * d + row // n).astype(h.dtype)
        hp = jnp.dot(perm, h, preferred_element_type=F32).astype(cm_ref.dtype)
        for r in range(d):
            cm_ref[0, r] = hp[r * n:(r + 1) * n]


def rmsnorm_cast(x, g, B, S, dils):
    T, D = x.shape
    tm = _tile(S, 256, 16 * max(dils, default=1))
    spb = S // tm
    outs = pl.pallas_call(
        functools.partial(_rms_kernel, dils=tuple(dils)),
        grid=(T // tm,),
        in_specs=[pl.BlockSpec((tm, D), lambda i: (i, 0)), pl.BlockSpec((1, D), lambda i: (0, 0))],
        out_specs=[pl.BlockSpec((tm, D), lambda i: (i, 0))]
        + [pl.BlockSpec((1, d, tm // d, D), lambda i: (i // spb, 0, i % spb, 0)) for d in dils],
        out_shape=[jax.ShapeDtypeStruct((T, D), BF16)]
        + [jax.ShapeDtypeStruct((B, d, S // d, D), BF16) for d in dils],
        compiler_params=_params(("parallel",)),
    )(x, g.reshape(1, D).astype(F32))
    return outs[0], {d: o.reshape(T, D) for d, o in zip(dils, outs[1:])}


def _in_proj_kernel(h_ref, w_ref, b_ref, o_ref, *, gate_tile0):
    def plain():
        o_ref[...] = jnp.dot(h_ref[...], w_ref[...], preferred_element_type=F32).astype(o_ref.dtype)

    if gate_tile0 is None:
        plain()
        return
    j = pl.program_id(1)
    pl.when(j < gate_tile0)(plain)

    @pl.when(j >= gate_tile0)
    def _():
        h = h_ref[...]
        for c0 in range(0, o_ref.shape[1], V7X_MXU_COLS):
            cols = slice(c0, c0 + V7X_MXU_COLS)
            acc = jnp.dot(h, w_ref[:, cols], preferred_element_type=F32)
            o_ref[:, cols] = jax.nn.sigmoid(acc + b_ref[:, cols]).astype(o_ref.dtype)


def in_projection(h, w, b_full, tn, n_out, w_col, gate_tile0=None):
    T, K = h.shape
    tm = _tile(T, 1024)
    return pl.pallas_call(
        functools.partial(_in_proj_kernel, gate_tile0=gate_tile0),
        grid=(T // tm, n_out),
        in_specs=[pl.BlockSpec((tm, K), lambda i, j: (i, 0)),
                  pl.BlockSpec((K, tn), lambda i, j: (0, w_col(j))),
                  pl.BlockSpec((1, tn), lambda i, j: (0, w_col(j)))],
        out_specs=pl.BlockSpec((tm, tn), lambda i, j: (i, j)),
        out_shape=jax.ShapeDtypeStruct((T, n_out * tn), BF16),
        compiler_params=_params(("parallel", "arbitrary")),
    )(h, w, b_full)


def _attn_kernel(q_ref, kp_ref, kc_ref, kn_ref, vp_ref, vc_ref, vn_ref, o_ref, lse_ref, so, sl,
                 *, tq, sub, half, hps, dil, cls_len, slopes):
    i = pl.program_id(1)
    hh = pl.program_id(2)
    key_pad = -(sub + 2 * half) % LANES
    span = sub + 2 * half + key_pad
    scale = HEAD_DIM ** -0.5
    qidx = lax.broadcasted_iota(jnp.int32, (sub, span), 0)
    kidx = lax.broadcasted_iota(jnp.int32, (sub, span), 1)
    absrel = jnp.abs(kidx - half - qidx)
    absrel_f = absrel.astype(F32)
    lane = lax.broadcasted_iota(jnp.int32, (sub, LANES), 1)
    n_sub = tq // sub
    valids = []
    for u in range(n_sub):
        kpos = i * tq + (u * sub - half) + kidx
        valids.append((absrel <= half) & (kpos >= 0) & (kpos < cls_len))
    neg_slopes = []
    for h in range(hps):
        s_h = jnp.float32(0.0)
        for b in range(len(slopes) // hps):
            s_h = jnp.where(hh == b, jnp.float32(-slopes[b * hps + h] * dil), s_h)
        neg_slopes.append(s_h)

    @pl.when(hh == 0)
    def _():
        sl[...] = jnp.zeros_like(sl)

    def keys(prev_ref, cur_ref, next_ref, r, u, cols):
        lo = u * sub - half if u > 0 else 0
        hi = (u + 1) * sub + half if u < n_sub - 1 else tq
        parts = [cur_ref[0, r, lo:hi, cols]]
        if u == 0:
            parts.insert(0, prev_ref[0, r, :, cols])
        if u == n_sub - 1:
            parts.append(next_ref[0, r, :, cols])
        if key_pad:
            parts.append(jnp.zeros((key_pad, HEAD_DIM), cur_ref.dtype))
        return parts[0] if len(parts) == 1 else jnp.concatenate(parts, axis=0)

    for r in range(dil):
        for u in range(n_sub):
            rows = pl.ds(u * sub * dil + r, sub, stride=dil) if dil > 1 else pl.ds(u * sub, sub)
            lse_tile = sl[rows, :]
            for h in range(hps):
                cols = slice(h * HEAD_DIM, (h + 1) * HEAD_DIM)
                q = q_ref[0, r, u * sub:(u + 1) * sub, cols]
                k = keys(kp_ref, kc_ref, kn_ref, r, u, cols)
                v = keys(vp_ref, vc_ref, vn_ref, r, u, cols)
                s = lax.dot_general(q, k, (((1,), (1,)), ((), ())), preferred_element_type=F32)
                s = s * scale + absrel_f * neg_slopes[h]
                s = jnp.where(valids[u], s, NEG_INF)
                m = jnp.max(s, axis=-1, keepdims=True)
                p = jnp.exp(s - m)
                den = jnp.sum(p, axis=-1, keepdims=True)
                so[h, rows, :] = jnp.dot(p.astype(v.dtype), v, preferred_element_type=F32) / den
                lse_tile = jnp.where(lane == hh * hps + h, m + jnp.log(den), lse_tile)
            sl[rows, :] = lse_tile
    for h in range(hps):
        o_ref[0, :, h * HEAD_DIM:(h + 1) * HEAD_DIM] = so[h].astype(o_ref.dtype)
    lse_ref[0] = sl[...]


def band_attention(qkv, B, S, group, hpg, n_heads_total):
    window, dil = ATTN_PATTERNS[group]
    half = window // (2 * dil)
    W = hpg * HEAD_DIM
    L = S // dil
    tq = min(L, max(ATTN_Q_TILE, ATTN_ROW_TILE // dil))
    rows = tq * dil
    sub = min(ATTN_Q_TILE, tq)
    hps = max(1, hpg * ATTN_ROW_TILE // max(rows, ATTN_ROW_TILE))
    assert S % dil == 0 and L % tq == 0 and tq % sub == 0 and tq % half == 0 and half % 16 == 0 and hpg % hps == 0
    n_hh = hpg // hps
    Wb = hps * HEAD_DIM
    hb = tq // half
    n_hb = L // half
    s_all = 2.0 ** (-8.0 * np.arange(1, n_heads_total + 1) / n_heads_total)
    slopes = tuple(float(np.float32(v)) for v in s_all[group * hpg:(group + 1) * hpg])
    qv = qkv.reshape(B, dil, L, 3 * W)

    cur = lambda part: pl.BlockSpec((1, dil, tq, Wb), lambda b, i, hh: (b, 0, i, part * n_hh + hh))
    prv = lambda part: pl.BlockSpec(
        (1, dil, half, Wb), lambda b, i, hh: (b, 0, jnp.maximum(i * hb - 1, 0), part * n_hh + hh))
    nxt = lambda part: pl.BlockSpec(
        (1, dil, half, Wb), lambda b, i, hh: (b, 0, jnp.minimum((i + 1) * hb, n_hb - 1), part * n_hh + hh))
    o, lse = pl.pallas_call(
        functools.partial(_attn_kernel, tq=tq, sub=sub, half=half, hps=hps, dil=dil, cls_len=L, slopes=slopes),
        grid=(B, L // tq, n_hh),
        in_specs=[cur(0), prv(1), cur(1), nxt(1), prv(2), cur(2), nxt(2)],
        out_specs=[pl.BlockSpec((1, rows, Wb), lambda b, i, hh: (b, i, hh)),
                   pl.BlockSpec((1, rows, LANES), lambda b, i, hh: (b, i, 0))],
        out_shape=[jax.ShapeDtypeStruct((B, S, W), BF16), jax.ShapeDtypeStruct((B, S, LANES), F32)],
        scratch_shapes=[pltpu.VMEM((hps, rows, LANES), F32), pltpu.VMEM((rows, LANES), F32)],
        compiler_params=_params(("parallel", "arbitrary", "arbitrary")),
    )(qv, qv, qv, qv, qv, qv, qv)
    return o.reshape(B * S, W), lse.reshape(B * S, LANES)


def _dft_tables(n, cols=None):
    cols = n if cols is None else cols
    k = jnp.arange(n, dtype=jnp.int32)

    def thin(s):
        ang = ((k[:, None] * s[None, :]) % n).astype(F32) * np.float32(2.0 * np.pi / n)
        return jnp.cos(ang), jnp.sin(ang)

    m = 64
    if cols <= m or cols % m:
        return thin(jnp.arange(cols, dtype=jnp.int32))
    c_hi, s_hi = thin(jnp.arange(cols // m, dtype=jnp.int32) * m)
    c_lo, s_lo = thin(jnp.arange(m, dtype=jnp.int32))
    cos = c_hi[:, :, None] * c_lo[:, None, :] - s_hi[:, :, None] * s_lo[:, None, :]
    sin = s_hi[:, :, None] * c_lo[:, None, :] + c_hi[:, :, None] * s_lo[:, None, :]
    return cos.reshape(n, cols), sin.reshape(n, cols)


def _fourier_chan_kernel(xd_ref, xa_ref, xb_ref, xh_ref, cos_ref, sin_ref, pe_ref, po_ref, ph_ref):
    i = pl.program_id(1)
    ts = xd_ref.shape[0]
    cg = cos_ref.shape[0]
    row = lax.broadcasted_iota(jnp.int32, (ts, ts), 0)
    col = lax.broadcasted_iota(jnp.int32, (ts, ts), 1)
    flip = (col == ts - row).astype(xa_ref.dtype)
    rev = jnp.dot(flip, xa_ref[...], preferred_element_type=F32)
    first = lax.broadcasted_iota(jnp.int32, rev.shape, 0) == 0
    rev = jnp.where(first, xb_ref[0:1, :].astype(F32), rev)
    xd = xd_ref[...].astype(F32)
    xe = xd + jnp.where(first & (i == 0), 0.0, rev)
    xo = xd - rev
    xe, xo = xe.astype(BF16), xo.astype(BF16)
    for g in range(N_FOURIER_GROUPS):
        cols = slice(g * cg, (g + 1) * cg)
        pe_ref[0, :, cols] = jnp.dot(xe[:, cols], cos_ref[...], preferred_element_type=F32).astype(pe_ref.dtype)
        po_ref[0, :, cols] = jnp.dot(xo[:, cols], sin_ref[...], preferred_element_type=F32).astype(po_ref.dtype)

    @pl.when(i == 0)
    def _():
        for g in range(N_FOURIER_GROUPS):
            cols = slice(g * cg, (g + 1) * cg)
            ph_ref[0, :, cols] = jnp.dot(xh_ref[:, cols], cos_ref[...],
                                         preferred_element_type=F32).astype(ph_ref.dtype)


def fourier_channel_stage(fg, B, S, cg, cos_c, sin_c):
    half = S // 2
    ts = _tile(half, 256, 16)
    nt = half // ts
    n_all = N_FOURIER_GROUPS * cg
    sub = 16
    return pl.pallas_call(
        _fourier_chan_kernel,
        grid=(B, nt),
        in_specs=[pl.BlockSpec((ts, n_all), lambda b, i: (b * (S // ts) + i, 0)),
                  pl.BlockSpec((ts, n_all), lambda b, i: (b * (S // ts) + (S // ts - 1 - i), 0)),
                  pl.BlockSpec((sub, n_all), lambda b, i: (b * (S // sub) + ((S - i * ts) % S) // sub, 0)),
                  pl.BlockSpec((sub, n_all), lambda b, i: (b * (S // sub) + half // sub, 0)),
                  pl.BlockSpec((cg, cg), lambda b, i: (0, 0)),
                  pl.BlockSpec((cg, cg), lambda b, i: (0, 0))],
        out_specs=[pl.BlockSpec((1, ts, n_all), lambda b, i: (b, i, 0)),
                   pl.BlockSpec((1, ts, n_all), lambda b, i: (b, i, 0)),
                   pl.BlockSpec((1, sub, n_all), lambda b, i: (b, 0, 0))],
        out_shape=[jax.ShapeDtypeStruct((B, half, n_all), BF16), jax.ShapeDtypeStruct((B, half, n_all), BF16),
                   jax.ShapeDtypeStruct((B, sub, n_all), BF16)],
        compiler_params=_params(("parallel", "arbitrary")),
    )(fg, fg, fg, fg, cos_c, sin_c)


def _fourier_seq_kernel(cs_ref, ss_ref, pe_ref, po_ref, ph_ref, o_ref, acc_ref, *, scale):
    k = pl.program_id(3)

    @pl.when(k == 0)
    def _():
        acc_ref[...] = jnp.zeros_like(acc_ref)

    acc_ref[...] += (jnp.dot(cs_ref[...], pe_ref[0], preferred_element_type=F32)
                     + jnp.dot(ss_ref[...], po_ref[0], preferred_element_type=F32))

    @pl.when(k == pl.num_programs(3) - 1)
    def _():
        tm = acc_ref.shape[0]
        freq = pl.program_id(1) * tm + lax.broadcasted_iota(jnp.int32, (tm, 1), 0)
        sign = (1 - 2 * (freq & 1)).astype(F32)
        o_ref[0] = ((acc_ref[...] + sign * ph_ref[0, 0:1, :].astype(F32)) * scale).astype(o_ref.dtype)


def fourier_sequence_stage(pe, po, ph, cos_s, neg_sin_s, S, scale):
    B, half, N = pe.shape
    tm, tn, tk = _tile(S, 1024), _tile(N, 1024), _tile(half, 2048)
    out = pl.pallas_call(
        functools.partial(_fourier_seq_kernel, scale=scale),
        grid=(B, S // tm, N // tn, half // tk),
        in_specs=[pl.BlockSpec((tm, tk), lambda b, i, j, k: (i, k)),
                  pl.BlockSpec((tm, tk), lambda b, i, j, k: (i, k)),
                  pl.BlockSpec((1, tk, tn), lambda b, i, j, k: (b, k, j)),
                  pl.BlockSpec((1, tk, tn), lambda b, i, j, k: (b, k, j)),
                  pl.BlockSpec((1, ph.shape[1], tn), lambda b, i, j, k: (b, 0, j))],
        out_specs=pl.BlockSpec((1, tm, tn), lambda b, i, j, k: (b, i, j)),
        out_shape=jax.ShapeDtypeStruct((B, S, N), BF16),
        scratch_shapes=[pltpu.VMEM((tm, tn), F32)],
        compiler_params=_params(("parallel", "parallel", "parallel", "arbitrary")),
    )(cos_s, neg_sin_s, pe, po, ph)
    return out.reshape(B * S, N)


def _merge_kernel(o0_ref, o1_ref, o2_ref, l0_ref, l1_ref, l2_ref, fr_ref, wa_ref, wf_ref, *refs, hpg, n_b):
    ga_refs, gf_refs = refs[:n_b], refs[n_b:2 * n_b]
    m_ref, oc_ref = refs[2 * n_b], refs[2 * n_b + 1]
    bw = ga_refs[0].shape[1]
    fr = fr_ref[...]
    fours = [jnp.dot(fr, wf_ref[:, c * bw:(c + 1) * bw], preferred_element_type=F32) for c in range(n_b)]
    l0, l1, l2 = l0_ref[...], l1_ref[...], l2_ref[...]
    m = jnp.maximum(jnp.maximum(l0, l1), l2)
    e0, e1, e2 = jnp.exp(l0 - m), jnp.exp(l1 - m), jnp.exp(l2 - m)
    den = e0 + e1 + e2
    w0, w1, w2 = e0 / den, e1 / den, e2 / den
    for h in range(hpg):
        cols = slice(h * HEAD_DIM, (h + 1) * HEAD_DIM)
        oc = (w0[:, h:h + 1] * o0_ref[:, cols].astype(F32)
              + w1[:, h:h + 1] * o1_ref[:, cols].astype(F32)
              + w2[:, h:h + 1] * o2_ref[:, cols].astype(F32))
        oc_ref[:, cols] = oc.astype(oc_ref.dtype)
    oc_all = oc_ref[...]
    for c in range(n_b):
        cols = slice(c * bw, (c + 1) * bw)
        attn = jnp.dot(oc_all, wa_ref[:, cols], preferred_element_type=F32)
        m_ref[:, cols] = (ga_refs[c][...].astype(F32) * attn
                          + gf_refs[c][...].astype(F32) * fours[c]).astype(m_ref.dtype)


def merge_branches(os_, lses, fr, wa, wf, fg, gate_start, hpg):
    T, W = os_[0].shape
    D = wa.shape[1]
    FW = fr.shape[1]
    tm = _tile(T, 256)
    bw = math.gcd(D, gate_start)
    n_b, g0 = D // bw, gate_start // bw
    row = lambda w: pl.BlockSpec((tm, w), lambda i: (i, 0))
    resident = lambda r: pl.BlockSpec((r, D), lambda i: (0, 0), pipeline_mode=pl.Buffered(1))
    gate = lambda blk: pl.BlockSpec((tm, bw), lambda i: (i, blk))
    return pl.pallas_call(
        functools.partial(_merge_kernel, hpg=hpg, n_b=n_b),
        grid=(T // tm,),
        in_specs=[row(W), row(W), row(W), row(LANES), row(LANES), row(LANES), row(FW), resident(W), resident(FW)]
        + [gate(g0 + c) for c in range(2 * n_b)],
        out_specs=pl.BlockSpec((tm, D), lambda i: (i, 0)),
        out_shape=jax.ShapeDtypeStruct((T, D), BF16),
        scratch_shapes=[pltpu.VMEM((tm, W), BF16)],
        compiler_params=_params(("parallel",)),
    )(*os_, *lses, fr, wa, wf, *([fg] * (2 * n_b)))


def _out_proj_kernel(m_ref, w_ref, x_ref, o_ref):
    o_ref[...] = x_ref[...] + jnp.dot(m_ref[...], w_ref[...], preferred_element_type=F32)


def out_projection(merged, w, x):
    T, K = merged.shape
    N = w.shape[1]
    tm, tn = _tile(T, 1024), _tile(N, 1024)
    return pl.pallas_call(
        _out_proj_kernel,
        grid=(T // tm, N // tn),
        in_specs=[pl.BlockSpec((tm, K), lambda i, j: (i, 0)),
                  pl.BlockSpec((K, tn), lambda i, j: (0, j)),
                  pl.BlockSpec((tm, tn), lambda i, j: (i, j))],
        out_specs=pl.BlockSpec((tm, tn), lambda i, j: (i, j)),
        out_shape=jax.ShapeDtypeStruct((T, N), F32),
        compiler_params=_params(("parallel", "arbitrary")),
    )(merged, w, x)


def _router_kernel(*refs, bounds):
    n_in = len(bounds) - 1
    i = pl.program_id(0)
    for k in range(n_in):
        @pl.when((i >= bounds[k]) & (i < bounds[k + 1]))
        def _(k=k):
            _route_tile(refs[k], *refs[n_in:])


def _route_tile(x_ref, g_ref, w_ref, b_ref, eid_ref, cw_ref, hp_ref):
    x = x_ref[...]
    ms = jnp.mean(x * x, axis=-1, keepdims=True)
    hf = x * lax.rsqrt(ms + RMS_EPS) * g_ref[...]
    h = hf.astype(BF16)
    half_d = hf.shape[1] // 2
    packed = _pack_pairs(hf[:, :half_d], hf[:, half_d:])
    for c in range(hp_ref.shape[1]):
        hp_ref[:, c, :] = packed[:, c * LANES:(c + 1) * LANES]
    lg = jnp.dot(h, w_ref[...], preferred_element_type=F32) + b_ref[...]
    lane = lax.broadcasted_iota(jnp.int32, lg.shape, 1)
    big = jnp.int32(LANES)
    in_grp = lane < N_EXPERT_GROUPS
    gl = jnp.where(in_grp, lg, -jnp.inf)
    gmax = jnp.max(gl, axis=-1, keepdims=True)
    gsel = jnp.min(jnp.where(gl == gmax, lane, big), axis=-1, keepdims=True)
    p_group = 1.0 / jnp.sum(jnp.where(in_grp, jnp.exp(gl - gmax), 0.0), axis=-1, keepdims=True)
    lo = N_EXPERT_GROUPS + gsel * EXPERTS_PER_GROUP
    in_sel = (lane >= lo) & (lane < lo + EXPERTS_PER_GROUP)
    el = jnp.where(in_sel, lg, -jnp.inf)
    t1 = jnp.max(el, axis=-1, keepdims=True)
    i1 = jnp.min(jnp.where(el == t1, lane, big), axis=-1, keepdims=True)
    el2 = jnp.where(lane == i1, -jnp.inf, el)
    t2 = jnp.max(el2, axis=-1, keepdims=True)
    i2 = jnp.min(jnp.where(el2 == t2, lane, big), axis=-1, keepdims=True)
    e21 = jnp.exp(t2 - t1)
    p1 = 1.0 / (1.0 + e21)
    p2 = e21 / (1.0 + e21)
    eid_ref[...] = jnp.where(lane == 0, i1 - N_EXPERT_GROUPS, jnp.where(lane == 1, i2 - N_EXPERT_GROUPS, 0))
    cw_ref[...] = jnp.where(lane == 0, p_group * p1, jnp.where(lane == 1, p_group * p2, 0.0))


def route(x1s, g, w_router, b_router, tm):
    D = x1s[0].shape[1]
    bounds = [0]
    for x1 in x1s:
        bounds.append(bounds[-1] + x1.shape[0] // tm)
    T = bounds[-1] * tm

    def x_spec(k):
        lo, n = bounds[k], bounds[k + 1] - bounds[k]
        return pl.BlockSpec((tm, D), lambda i: (jnp.clip(i - lo, 0, n - 1), 0))

    return pl.pallas_call(
        functools.partial(_router_kernel, bounds=tuple(bounds)),
        grid=(bounds[-1],),
        in_specs=[x_spec(k) for k in range(len(x1s))]
        + [pl.BlockSpec((1, D), lambda i: (0, 0)),
           pl.BlockSpec((D, LANES), lambda i: (0, 0)),
           pl.BlockSpec((1, LANES), lambda i: (0, 0))],
        out_specs=[pl.BlockSpec((tm, LANES), lambda i: (i, 0)), pl.BlockSpec((tm, LANES), lambda i: (i, 0)),
                   pl.BlockSpec((tm, D // (2 * LANES), LANES), lambda i: (i, 0, 0))],
        out_shape=[jax.ShapeDtypeStruct((T, LANES), jnp.int32), jax.ShapeDtypeStruct((T, LANES), F32),
                   jax.ShapeDtypeStruct((T, D // (2 * LANES), LANES), U32)],
        compiler_params=_params(("arbitrary",)),
    )(*x1s, g.reshape(1, D).astype(F32), w_router, b_router)


def _expert_up_kernel(src_ref, te_ref, nused_ref, hp_hbm, wg_ref, wu_ref, a_ref, buf, sem, *, tme, nc, n_k):
    i = pl.program_id(0)
    n_used = nused_ref[0]
    half_d = nc * LANES

    def row_copy(tile, slot, r):
        return pltpu.make_async_copy(hp_hbm.at[src_ref[tile * tme + r]], buf.at[slot, pl.ds(r * nc, nc), :],
                                     sem.at[slot])

    def drain(tile, slot):
        def body(r, c):
            row_copy(tile, slot, r).wait()
            return c
        lax.fori_loop(0, tme, body, 0, unroll=8)

    @pl.when(i == 0)
    def _():
        def body(r, c):
            row_copy(0, 0, r).start()
            return c
        lax.fori_loop(0, tme, body, 0, unroll=8)

    @pl.when(i < n_used)
    def _():
        slot = i % 2
        drain(i, slot)
        nxt = jnp.minimum(i + 1, n_used - 1)
        for r in range(tme):
            row_copy(nxt, 1 - slot, r).start(priority=r % 2)
        cpk = nc // n_k
        wk = cpk * LANES
        gate = up = None
        for q in range(n_k):
            parts = [_unpack_pairs(buf[slot, pl.ds(q * cpk + cc, tme, stride=nc), :]) for cc in range(cpk)]
            lo = jnp.concatenate([p[0].astype(BF16) for p in parts], axis=1)
            hi = jnp.concatenate([p[1].astype(BF16) for p in parts], axis=1)
            rows_lo = slice(q * wk, (q + 1) * wk)
            rows_hi = slice(half_d + q * wk, half_d + (q + 1) * wk)
            g = (jnp.dot(lo, wg_ref[0, rows_lo, :], preferred_element_type=F32)
                 + jnp.dot(hi, wg_ref[0, rows_hi, :], preferred_element_type=F32))
            u = (jnp.dot(lo, wu_ref[0, rows_lo, :], preferred_element_type=F32)
                 + jnp.dot(hi, wu_ref[0, rows_hi, :], preferred_element_type=F32))
            gate = g if gate is None else gate + g
            up = u if up is None else up + u
        a_ref[...] = (jax.nn.silu(gate) * up).astype(a_ref.dtype)

    @pl.when(i == n_used - 1)
    def _():
        drain(i, 1 - i % 2)

    @pl.when(i >= n_used)
    def _():
        a_ref[...] = jnp.zeros_like(a_ref)


def _expert_down_kernel(dst_ref, te_ref, nused_ref, a_ref, wd_ref, yt_hbm, ybuf, sem, *, tme, n_split, n_tiles):
    i = pl.program_id(0)
    n_used = nused_ref[0]
    last = n_tiles - 1
    half_d = ybuf.shape[2]
    wc = half_d // n_split

    def row_copy(tile, r):
        slot = (tile + 3) % 3
        return pltpu.make_async_copy(ybuf.at[slot, pl.ds(r, 1), :],
                                     yt_hbm.at[pl.ds(dst_ref[(tile + 1) * tme + r], 1), :], sem.at[slot])

    def drain(tile):
        def body(r, c):
            row_copy(tile, r).wait()
            return c
        lax.fori_loop(0, tme, body, 0, unroll=8)

    @pl.when(i == 0)
    def _():
        ybuf[2] = jnp.zeros(ybuf.shape[1:], ybuf.dtype)

    @pl.when((i >= 2) & (i - 3 < n_used))
    def _():
        drain(i - 3)

    @pl.when(i < n_used)
    def _():
        a = a_ref[...]
        for r in range(tme):
            row_copy(i - 1, r).start(priority=r % 2)
        for k in range(n_split):
            lo = jnp.dot(a, wd_ref[0, :, k * wc:(k + 1) * wc], preferred_element_type=F32)
            hi = jnp.dot(a, wd_ref[0, :, half_d + k * wc:half_d + (k + 1) * wc], preferred_element_type=F32)
            ybuf[i % 3, :, k * wc:(k + 1) * wc] = _pack_pairs(lo, hi)

    @pl.when(i == n_used)
    def _():
        def body(r, c):
            row_copy(i - 1, r).start()
            return c
        lax.fori_loop(0, tme, body, 0, unroll=8)

    @pl.when((i == last) & (last - 2 < n_used))
    def _():
        drain(last - 2)

    @pl.when((i == last) & (last - 1 < n_used))
    def _():
        drain(last - 1)


def expert_ffn(hp, w_gate, w_up, w_down, src, dst, tile_expert, n_used, n_tiles, tme, n_rows_out):
    nc = hp.shape[1]
    D = nc * 2 * LANES
    F = w_gate.shape[2]
    P = n_tiles * tme
    act = pl.pallas_call(
        functools.partial(_expert_up_kernel, tme=tme, nc=nc, n_k=max(1, nc // 4)),
        grid_spec=pltpu.PrefetchScalarGridSpec(
            num_scalar_prefetch=3,
            grid=(n_tiles,),
            in_specs=[pl.BlockSpec(memory_space=pl.ANY),
                      pl.BlockSpec((1, D, F), lambda i, s, te, n: (te[i], 0, 0)),
                      pl.BlockSpec((1, D, F), lambda i, s, te, n: (te[i], 0, 0))],
            out_specs=pl.BlockSpec((tme, F), lambda i, s, te, n: (i, 0)),
            scratch_shapes=[pltpu.VMEM((2, tme * nc, LANES), U32), pltpu.SemaphoreType.DMA((2,))],
        ),
        out_shape=jax.ShapeDtypeStruct((P, F), BF16),
        compiler_params=_params(("arbitrary",)),
    )(src, tile_expert, n_used, hp, w_gate, w_up)
    n_split = max(1, D // 1024)
    spare = n_rows_out - tme + jnp.arange(tme, dtype=jnp.int32)
    return pl.pallas_call(
        functools.partial(_expert_down_kernel, tme=tme, n_split=n_split, n_tiles=n_tiles),
        grid_spec=pltpu.PrefetchScalarGridSpec(
            num_scalar_prefetch=3,
            grid=(n_tiles,),
            in_specs=[pl.BlockSpec((tme, F), lambda i, d, te, n: (i, 0)),
                      pl.BlockSpec((1, F, D), lambda i, d, te, n: (te[i], 0, 0))],
            out_specs=pl.BlockSpec(memory_space=pl.ANY),
            scratch_shapes=[pltpu.VMEM((3, tme, D // 2), U32), pltpu.SemaphoreType.DMA((3,))],
        ),
        out_shape=jax.ShapeDtypeStruct((n_rows_out, D // 2), U32),
        compiler_params=_params(("arbitrary",)),
    )(jnp.concatenate([spare, dst]), tile_expert, n_used, act, w_down)


def _final_kernel(x_ref, cw_ref, y0_ref, y1_ref, g_ref, o_ref):
    half_d = y0_ref.shape[1]
    cw0 = cw_ref[:, 0:1]
    cw1 = cw_ref[:, 1:2]
    lo0, hi0 = _unpack_pairs(y0_ref[...])
    lo1, hi1 = _unpack_pairs(y1_ref[...])
    xl = x_ref[:, :half_d] + (cw0 * lo0 + cw1 * lo1)
    xh = x_ref[:, half_d:] + (cw0 * hi0 + cw1 * hi1)
    ss = jnp.sum(xl * xl, axis=-1, keepdims=True) + jnp.sum(xh * xh, axis=-1, keepdims=True)
    inv = lax.rsqrt(ss / (2 * half_d) + RMS_EPS)
    o_ref[:, :half_d] = xl * inv * g_ref[:, :half_d]
    o_ref[:, half_d:] = xh * inv * g_ref[:, half_d:]


def final_combine(x1, cw, yt, g, tm, tile0, n_tok_tiles):
    T, D = x1.shape
    return pl.pallas_call(
        _final_kernel,
        grid=(T // tm,),
        in_specs=[pl.BlockSpec((tm, D), lambda i: (i, 0)),
                  pl.BlockSpec((tm, LANES), lambda i: (tile0 + i, 0)),
                  pl.BlockSpec((tm, D // 2), lambda i: (tile0 + i, 0)),
                  pl.BlockSpec((tm, D // 2), lambda i: (n_tok_tiles + tile0 + i, 0)),
                  pl.BlockSpec((1, D), lambda i: (0, 0))],
        out_specs=pl.BlockSpec((tm, D), lambda i: (i, 0)),
        out_shape=jax.ShapeDtypeStruct((T, D), F32),
        compiler_params=_params(("parallel",)),
    )(x1, cw, yt, yt, g.reshape(1, D).astype(F32))


def _sorted_layout(eid, tme):
    T = eid.shape[0]
    n_tiles = (2 * T) // tme + N_EXPERTS
    P = n_tiles * tme
    flat_e = eid.reshape(-1)
    onehot = (flat_e[:, None] == jnp.arange(N_EXPERTS, dtype=jnp.int32)[None, :]).astype(BF16)
    blk = _tile(2 * T, 256, 8)
    within = jnp.einsum("ij,bjk->bik", jnp.tril(jnp.ones((blk, blk), BF16)), onehot.reshape(-1, blk, N_EXPERTS),
                        preferred_element_type=F32)
    totals = within[:, -1, :]
    csum = (within + (jnp.cumsum(totals, axis=0) - totals)[:, None, :]).reshape(2 * T, N_EXPERTS).astype(jnp.int32)
    rank = jnp.take_along_axis(csum, flat_e[:, None], axis=1)[:, 0] - 1
    counts = csum[-1]
    padded = ((counts + tme - 1) // tme) * tme
    ends = jnp.cumsum(padded)
    pos = (ends - padded)[flat_e] + rank
    spare = 2 * T + jnp.arange(P, dtype=jnp.int32) % tme
    a = jnp.arange(2 * T, dtype=jnp.int32)
    dst = spare.at[pos].set((a % 2) * T + a // 2, unique_indices=True)
    src = jnp.where(dst < 2 * T, dst % T, 0)
    n_used = (ends[-1] // tme).astype(jnp.int32)
    tile_start = jnp.minimum(jnp.arange(n_tiles, dtype=jnp.int32), n_used - 1) * tme
    tile_expert = jnp.minimum(jnp.sum(tile_start[:, None] >= ends[None, :], axis=1), N_EXPERTS - 1).astype(jnp.int32)
    return src, dst, tile_expert, n_used.reshape(1), n_tiles


def _encoder_trunk(x, wts):
    B, S, D = x.shape
    T = B * S
    hpg, cg = wts["hpg"], wts["cg"]
    n_grp = len(ATTN_PATTERNS)
    W = hpg * HEAD_DIM
    f_start = 3 * n_grp * W
    f_width = N_FOURIER_GROUPS * cg
    x2d = x.reshape(T, D)

    dils = sorted({d for _, d in ATTN_PATTERNS if d > 1})
    h, h_cm = rmsnorm_cast(x2d, wts["attn_norm_g"], B, S, dils)
    h_cm[1] = h

    outs, lses = [], []
    for g, (_, dil) in enumerate(ATTN_PATTERNS):
        qkv = in_projection(h_cm[dil], wts["w_in"], wts["b_full"], W, 3, lambda j, g=g: j * n_grp + g)
        o_g, lse_g = band_attention(qkv, B, S, g, hpg, n_grp * hpg)
        outs.append(o_g)
        lses.append(lse_g)

    tn = _tile(math.gcd(math.gcd(f_start, f_width), 2 * D), 1024)
    c0 = f_start // tn
    fg = in_projection(h, wts["w_in"], wts["b_full"], tn, (f_width + 2 * D) // tn, lambda j: c0 + j,
                       gate_tile0=f_width // tn)

    pe, po, ph = fourier_channel_stage(fg, B, S, cg, wts["cos_c"], wts["sin_c"])
    cos_s, sin_s = _dft_tables(S, S // 2)
    fr = fourier_sequence_stage(pe, po, ph, cos_s.astype(BF16), (-sin_s).astype(BF16), S,
                                float(1.0 / math.sqrt(S * cg)))

    merged = merge_branches(outs, lses, fr, wts["w_branch_attn"], wts["w_branch_fourier"], fg, f_width, hpg)
    return out_projection(merged, wts["w_out"], x2d)


def _moe_and_final_norm(x1s, wts):
    t_all = sum(x1.shape[0] for x1 in x1s)
    tm = _tile(math.gcd(*[x1.shape[0] for x1 in x1s]), 256, 8)
    eid, cw, hp = route(x1s, wts["ffn_norm_g"], wts["w_router"], wts["b_router"], tm)
    tme = _tile(2 * t_all, EXPERT_ROW_TILE, 8)
    src, dst, tile_expert, n_used, n_tiles = _sorted_layout(eid[:, :2], tme)
    yt = expert_ffn(hp, wts["w_expert_gate"], wts["w_expert_up"], wts["w_expert_down"], src, dst, tile_expert,
                    n_used, n_tiles, tme, 2 * t_all + tme)
    outs, tile0 = [], 0
    for x1 in x1s:
        outs.append(final_combine(x1, cw, yt, wts["final_norm_g"], tm, tile0, t_all // tm))
        tile0 += x1.shape[0] // tm
    return outs


def kernel(x_prompt, x_sample, attn_norm_g, w_in, w_branch_attn, w_branch_fourier, b_gate, w_out, ffn_norm_g, w_router_group, b_router_group, w_router_expert, b_router_expert, w_expert_gate, w_expert_up, w_expert_down, final_norm_g):
    assert w_in.shape[0] == 1, "the final norm is fused into the layer's last kernel: one layer only"
    l = 0
    D = x_prompt.shape[-1]
    hpg = w_branch_attn.shape[1] // HEAD_DIM
    cg = w_branch_fourier.shape[1] // N_FOURIER_GROUPS
    in_width = w_in.shape[2]
    n_gate = b_gate.shape[1]
    cos_c, sin_c = _dft_tables(cg)
    w_r = jnp.concatenate(
        [w_router_group[l], jnp.transpose(w_router_expert[l], (1, 0, 2)).reshape(D, N_EXPERTS)], axis=1)
    b_r = jnp.concatenate([b_router_group[l], b_router_expert[l].reshape(-1)])
    pad = LANES - w_r.shape[1]
    wts = dict(
        hpg=hpg, cg=cg, cos_c=cos_c.astype(BF16), sin_c=sin_c.astype(BF16),
        attn_norm_g=attn_norm_g[l], ffn_norm_g=ffn_norm_g[l], final_norm_g=final_norm_g,
        w_in=w_in[l].astype(BF16),
        b_full=jnp.concatenate([jnp.zeros((in_width - n_gate,), F32), b_gate[l].astype(F32)]).reshape(1, in_width),
        w_branch_attn=w_branch_attn[l].astype(BF16),
        w_branch_fourier=w_branch_fourier[l].astype(BF16),
        w_out=w_out[l].astype(BF16),
        w_router=jnp.pad(w_r, ((0, 0), (0, pad))).astype(BF16),
        b_router=jnp.pad(b_r, (0, pad)).reshape(1, LANES).astype(F32),
        w_expert_gate=w_expert_gate[l].astype(BF16),
        w_expert_up=w_expert_up[l].astype(BF16),
        w_expert_down=w_expert_down[l].astype(BF16),
    )
    xs = (x_prompt, x_sample)
    outs = _moe_and_final_norm([_encoder_trunk(x, wts) for x in xs], wts)
    return tuple(o.reshape(x.shape) for o, x in zip(outs, xs))
```
